```python
import jax, jax.numpy as jnp
from jax import lax
import numpy as np

D_MODEL = 2048
BATCH = 1
SEQ = 16384
DEPTH = 2

CHUNK = 64
N_MIXERS = 2
EPS = 1e-6
N_HEADS = 16
HEAD_DIM = D_MODEL // N_HEADS
N_KV_GROUPS = 4
HEADS_PER_GROUP = N_HEADS // N_KV_GROUPS
IDX_HEADS = 16
IDX_DIM = 64
TOPK_KEYS = 256
Q_BLOCK = 128
ROPE_THETA = 10000.0
NEG_INF = -1e30
ATTN_SPLITS = (N_HEADS * HEAD_DIM, N_KV_GROUPS * HEAD_DIM, N_KV_GROUPS * HEAD_DIM,
               IDX_HEADS * IDX_DIM, IDX_DIM, IDX_HEADS)
ATTN_IN = sum(ATTN_SPLITS)
CONV_WIDTH = 31
N_GROUPS = 4
EXPERTS_PER_GROUP = 4
N_EXPERTS = N_GROUPS * EXPERTS_PER_GROUP
EXPERT_TOPK = 2
D_EXPERT = D_MODEL // 4
ROW_BLOCK = 128
N_LAYERS_A = (DEPTH + 1) // 2
N_LAYERS_B = DEPTH // 2

kernel_name = "hybrid_dsa_conformer_hmoe_adaln"


def rms_norm(x, g):
    xf = x.astype(jnp.float32)
    xf = xf * lax.rsqrt(jnp.mean(xf * xf, axis=-1, keepdims=True) + EPS)
    return xf.astype(x.dtype) * g


def layer_norm(x, g, b):
    xf = x.astype(jnp.float32)
    mu = jnp.mean(xf, axis=-1, keepdims=True)
    var = jnp.mean(jnp.square(xf - mu), axis=-1, keepdims=True)
    return ((xf - mu) * lax.rsqrt(var + EPS)).astype(x.dtype) * g + b


def rope(x, pos):
    half = x.shape[-1] // 2
    inv = ROPE_THETA ** (-jnp.arange(half, dtype=jnp.float32) / half)
    ang = pos.astype(jnp.float32)[:, None] * inv[None, :]
    cos = jnp.cos(ang)[:, None, :]
    sin = jnp.sin(ang)[:, None, :]
    xf = x.astype(jnp.float32)
    x1, x2 = xf[..., :half], xf[..., half:]
    return jnp.concatenate([x1 * cos - x2 * sin, x2 * cos + x1 * sin], axis=-1).astype(x.dtype)


def dsa_attention(h, w_in, w_out):
    B, T, _ = h.shape
    proj = h @ w_in
    q, k, v, qi, ki, wi = jnp.split(proj, list(np.cumsum(ATTN_SPLITS)[:-1]), axis=-1)
    pos = jnp.arange(T, dtype=jnp.int32)
    q = rope(q.reshape(B, T, N_HEADS, HEAD_DIM), pos)
    q = q.reshape(B, T, N_KV_GROUPS, HEADS_PER_GROUP, HEAD_DIM)
    k = rope(k.reshape(B, T, N_KV_GROUPS, HEAD_DIM), pos)
    v = v.reshape(B, T, N_KV_GROUPS, HEAD_DIM)
    qi = rope(qi.reshape(B, T, IDX_HEADS, IDX_DIM), pos)
    ki = rope(ki.reshape(B, T, 1, IDX_DIM), pos)[:, :, 0].astype(jnp.float32)
    wi = wi.astype(jnp.float32) * (IDX_HEADS ** -0.5)
    n_keep = min(TOPK_KEYS, T // 4)
    nb = T // Q_BLOCK
    key_chunk = pos // CHUNK

    def blockify(a):
        return jnp.moveaxis(a.reshape(B, nb, Q_BLOCK, *a.shape[2:]), 1, 0)

    def one_block(args):
        qb, qib, wib, qpos = args
        qchunk = qpos // CHUNK
        s = jax.nn.relu(jnp.einsum('bqhd,bsd->bqhs', qib.astype(jnp.float32), ki) * (IDX_DIM ** -0.5))
        s = jnp.einsum('bqhs,bqh->bqs', s, wib)
        admissible = key_chunk[None, :] <= qchunk[:, None]
        s = jnp.where(admissible[None], s, NEG_INF)
        _, idx = lax.top_k(s, n_keep)
        valid = (idx // CHUNK) <= qchunk[None, :, None]
        kb = jax.vmap(lambda a, i: a[i])(k, idx)
        vb = jax.vmap(lambda a, i: a[i])(v, idx)
        att = jnp.einsum('bqgrd,bqkgd->bqgrk', qb.astype(jnp.float32), kb.astype(jnp.float32)) * (HEAD_DIM ** -0.5)
        att = jnp.where(valid[:, :, None, None, :], att, NEG_INF)
        p = jax.nn.softmax(att, axis=-1)
        return jnp.einsum('bqgrk,bqkgd->bqgrd', p, vb.astype(jnp.float32)).astype(h.dtype)

    out = lax.map(one_block, (blockify(q), blockify(qi), blockify(wi), pos.reshape(nb, Q_BLOCK)))
    out = jnp.moveaxis(out, 0, 1).reshape(B, T, N_HEADS * HEAD_DIM)
    return out @ w_out


def conformer_conv(h, w_pw1, b_pw1, w_dw, b_dw, ln_g, ln_b, w_pw2, b_pw2):
    u = h @ w_pw1 + b_pw1
    a, gt = jnp.split(u, 2, axis=-1)
    u = a * jax.nn.sigmoid(gt)
    u = lax.conv_general_dilated(u, w_dw[:, None, :], window_strides=(1,),
                                 padding=[(CONV_WIDTH - 1, 0)],
                                 dimension_numbers=('NWC', 'WIO', 'NWC'),
                                 feature_group_count=D_MODEL) + b_dw
    u = jax.nn.silu(layer_norm(u, ln_g, ln_b))
    return u @ w_pw2 + b_pw2


def hier_moe(h, w_rg, b_rg, w_re, b_re, w_gate, w_up, w_down):
    B, T, D = h.shape
    N = B * T
    hf = h.reshape(N, D)
    g_logits = (hf @ w_rg + b_rg).astype(jnp.float32)
    g_prob = jax.nn.softmax(g_logits, axis=-1)
    g_idx = jnp.argmax(g_logits, axis=-1).astype(jnp.int32)
    g_w = jnp.take_along_axis(g_prob, g_idx[:, None], axis=1)
    e_logits = jnp.einsum('nd,gde->nge', hf, w_re) + b_re
    e_logits = jnp.take_along_axis(e_logits, g_idx[:, None, None], axis=1)[:, 0].astype(jnp.float32)
    top_p, top_i = lax.top_k(jax.nn.softmax(e_logits, axis=-1), EXPERT_TOPK)
    weights = g_w * top_p / jnp.sum(top_p, axis=-1, keepdims=True)
    expert_id = g_idx[:, None] * EXPERTS_PER_GROUP + top_i.astype(jnp.int32)

    A = N * EXPERT_TOPK
    flat_e = expert_id.reshape(A)
    flat_tok = jnp.repeat(jnp.arange(N, dtype=jnp.int32), EXPERT_TOPK)
    flat_w = weights.reshape(A)
    order = jnp.argsort(flat_e)
    e_sorted = flat_e[order]
    counts = jnp.bincount(flat_e, length=N_EXPERTS)
    padded = (counts + ROW_BLOCK - 1) // ROW_BLOCK * ROW_BLOCK
    start_sorted = jnp.cumsum(counts) - counts
    pad_end = jnp.cumsum(padded)
    start_pad = pad_end - padded
    dest = start_pad[e_sorted] + jnp.arange(A, dtype=jnp.int32) - start_sorted[e_sorted]
    n_blocks = -(-A // ROW_BLOCK) + N_EXPERTS
    P = n_blocks * ROW_BLOCK
    buf_tok = jnp.full((P,), N, dtype=jnp.int32).at[dest].set(flat_tok[order])
    buf_w = jnp.zeros((P,), jnp.float32).at[dest].set(flat_w[order])
    block_e = jnp.minimum(jnp.searchsorted(pad_end, jnp.arange(n_blocks) * ROW_BLOCK, side='right'),
                          N_EXPERTS - 1)
    h_pad = jnp.concatenate([hf, jnp.zeros((1, D), hf.dtype)], axis=0)

    def run_block(args):
        tok, e = args
        xb = h_pad[tok]
        return (jax.nn.silu(xb @ w_gate[e]) * (xb @ w_up[e])) @ w_down[e]

    yb = lax.map(run_block, (buf_tok.reshape(n_blocks, ROW_BLOCK), block_e))
    y = jnp.zeros((N + 1, D), jnp.float32).at[buf_tok].add(yb.reshape(P, D).astype(jnp.float32) * buf_w[:, None])
    return y[:N].astype(h.dtype).reshape(B, T, D)


def setup_inputs(seed: int = 0) -> dict:
    key = jax.random.key(seed)
    ks = jax.random.split(key, 32)

    def nrm(k, shape, scale):
        return jax.random.normal(k, shape, jnp.float32) * scale

    D = D_MODEL
    return {
        "x": nrm(ks[0], (BATCH, SEQ, D), 1.0),
        "c": nrm(ks[1], (BATCH, D), 1.0),
        "ada_w": nrm(ks[2], (DEPTH, D, 6 * D), 0.5 * D ** -0.5),
        "ada_b": nrm(ks[3], (DEPTH, 6 * D), 0.02),
        "norm1_g": 1.0 + nrm(ks[4], (DEPTH, D), 0.02),
        "norm2_g": 1.0 + nrm(ks[5], (DEPTH, D), 0.02),
        "attn_w_in": nrm(ks[6], (N_LAYERS_A, D, ATTN_IN), D ** -0.5),
        "attn_w_out": nrm(ks[7], (N_LAYERS_A, N_HEADS * HEAD_DIM, D), (N_HEADS * HEAD_DIM) ** -0.5),
        "conv_w_pw1": nrm(ks[8], (N_LAYERS_B, D, 2 * D), D ** -0.5),
        "conv_b_pw1": nrm(ks[9], (N_LAYERS_B, 2 * D), 0.02),
        "conv_w_dw": nrm(ks[10], (N_LAYERS_B, CONV_WIDTH, D), CONV_WIDTH ** -0.5),
        "conv_b_dw": nrm(ks[11], (N_LAYERS_B, D), 0.02),
        "conv_ln_g": 1.0 + nrm(ks[12], (N_LAYERS_B, D), 0.02),
        "conv_ln_b": nrm(ks[13], (N_LAYERS_B, D), 0.02),
        "conv_w_pw2": nrm(ks[14], (N_LAYERS_B, D, D), D ** -0.5),
        "conv_b_pw2": nrm(ks[15], (N_LAYERS_B, D), 0.02),
        "router_w_group": nrm(ks[16], (DEPTH, D, N_GROUPS), D ** -0.5),
        "router_b_group": nrm(ks[17], (DEPTH, N_GROUPS), 0.01),
        "router_w_expert": nrm(ks[18], (DEPTH, N_GROUPS, D, EXPERTS_PER_GROUP), D ** -0.5),
        "router_b_expert": nrm(ks[19], (DEPTH, N_GROUPS, EXPERTS_PER_GROUP), 0.01),
        "exp_w_gate": nrm(ks[20], (DEPTH, N_EXPERTS, D, D_EXPERT), D ** -0.5),
        "exp_w_up": nrm(ks[21], (DEPTH, N_EXPERTS, D, D_EXPERT), D ** -0.5),
        "exp_w_down": nrm(ks[22], (DEPTH, N_EXPERTS, D_EXPERT, D), D_EXPERT ** -0.5),
        "final_g": 1.0 + nrm(ks[23], (D,), 0.02),
    }


def reference(x, c, ada_w, ada_b, norm1_g, norm2_g, attn_w_in, attn_w_out,
              conv_w_pw1, conv_b_pw1, conv_w_dw, conv_b_dw, conv_ln_g, conv_ln_b,
              conv_w_pw2, conv_b_pw2, router_w_group, router_b_group,
              router_w_expert, router_b_expert, exp_w_gate, exp_w_up, exp_w_down,
              final_g):
    c_act = jax.nn.silu(c)
    for i in range(DEPTH):
        mod = c_act @ ada_w[i] + ada_b[i]
        sh1, sc1, g1, sh2, sc2, g2 = [m[:, None, :] for m in jnp.split(mod, 6, axis=-1)]
        hn = rms_norm(x, norm1_g[i]) * (1.0 + sc1) + sh1
        j = i // N_MIXERS
        if i % N_MIXERS == 0:
            y = dsa_attention(hn, attn_w_in[j], attn_w_out[j])
        else:
            y = conformer_conv(hn, conv_w_pw1[j], conv_b_pw1[j], conv_w_dw[j], conv_b_dw[j],
                               conv_ln_g[j], conv_ln_b[j], conv_w_pw2[j], conv_b_pw2[j])
        x = x + g1 * y
        hn = rms_norm(x, norm2_g[i]) * (1.0 + sc2) + sh2
        x = x + g2 * hier_moe(hn, router_w_group[i], router_b_group[i], router_w_expert[i],
                              router_b_expert[i], exp_w_gate[i], exp_w_up[i], exp_w_down[i])
    return rms_norm(x, final_g)
```

```python
import functools

import jax
import jax.numpy as jnp
from jax import lax
from jax.experimental import pallas as pl
from jax.experimental.pallas import tpu as pltpu

EPS = 1e-6
CHUNK = 64
N_HEADS = 16
N_KV_GROUPS = 4
HEADS_PER_GROUP = N_HEADS // N_KV_GROUPS
HEAD_DIM = 128
IDX_HEADS = 16
IDX_DIM = 64
TOPK_KEYS = 256
ROPE_THETA = 10000.0
NEG_INF = -1e30
CONV_WIDTH = 31
N_GROUPS = 4
EXPERTS_PER_GROUP = 4
N_EXPERTS = N_GROUPS * EXPERTS_PER_GROUP

LANES = 128
INT_MIN = -(2 ** 31)
INVALID_SHIFT = 1 << 30
VMEM_LIMIT = 56 * 1024 * 1024

F32 = jnp.float32
BF16 = jnp.bfloat16
I32 = jnp.int32


def _cparams(sem):
    return pltpu.CompilerParams(dimension_semantics=sem, vmem_limit_bytes=VMEM_LIMIT)


def _ada_kernel(c_ref, w_ref, b_ref, o_ref):
    c = c_ref[...]
    ca = c * jax.nn.sigmoid(c)
    o_ref[0] = jnp.dot(ca, w_ref[0], preferred_element_type=F32, precision=lax.Precision.HIGHEST) + b_ref[0]


def _ada_mod(c, ada_w, ada_b):
    depth, d, n = ada_w.shape
    tn = 768
    c8 = jnp.zeros((8, d), F32).at[0].set(c[0])
    out = pl.pallas_call(
        _ada_kernel,
        grid=(depth, n // tn),
        in_specs=[pl.BlockSpec((8, d), lambda i, j: (0, 0)),
                  pl.BlockSpec((1, d, tn), lambda i, j: (i, 0, j)),
                  pl.BlockSpec((1, 1, tn), lambda i, j: (i, 0, j))],
        out_specs=pl.BlockSpec((1, 8, tn), lambda i, j: (i, 0, j)),
        out_shape=jax.ShapeDtypeStruct((depth, 8, n), F32),
        compiler_params=_cparams(("arbitrary", "arbitrary")),
        name="ada_mod",
    )(c8, ada_w, ada_b.reshape(depth, 1, n))
    return out[:, 0, :]


def _normmod_kernel(x_ref, g_ref, sc_ref, sh_ref, o_ref):
    x = x_ref[...]
    xn = x * lax.rsqrt(jnp.mean(x * x, axis=-1, keepdims=True) + EPS)
    o_ref[...] = ((xn * g_ref[...]) * (1.0 + sc_ref[...]) + sh_ref[...]).astype(o_ref.dtype)


def _normmod(x, g, sc, sh, tm=512):
    t, d = x.shape
    vec = pl.BlockSpec((1, d), lambda i: (0, 0))
    return pl.pallas_call(
        _normmod_kernel,
        grid=(t // tm,),
        in_specs=[pl.BlockSpec((tm, d), lambda i: (i, 0)), vec, vec, vec],
        out_specs=pl.BlockSpec((tm, d), lambda i: (i, 0)),
        out_shape=jax.ShapeDtypeStruct((t, d), BF16),
        compiler_params=_cparams(("arbitrary",)),
        name="normmod",
    )(x, g.reshape(1, d), sc.reshape(1, d), sh.reshape(1, d))


def _route_kernel(x_ref, g_ref, sc_ref, sh_ref, wr_ref, br_ref, h_ref, cw_ref):
    x = x_ref[...]
    xn = x * lax.rsqrt(jnp.mean(x * x, axis=-1, keepdims=True) + EPS)
    h = (xn * g_ref[...]) * (1.0 + sc_ref[...]) + sh_ref[...]
    h_ref[...] = h.astype(h_ref.dtype)
    logits = jnp.dot(h, wr_ref[...], preferred_element_type=F32, precision=lax.Precision.HIGHEST) + br_ref[...]
    tm = x.shape[0]
    lane4 = lax.broadcasted_iota(I32, (tm, N_GROUPS), 1)

    def first_argmax(v):
        m = jnp.max(v, axis=-1, keepdims=True)
        return m, jnp.min(jnp.where(v == m, lane4, N_GROUPS), axis=-1, keepdims=True)

    gl = logits[:, 0:N_GROUPS]
    gmax, gidx = first_argmax(gl)
    g_w = 1.0 / jnp.sum(jnp.exp(gl - gmax), axis=-1, keepdims=True)
    el = jnp.zeros((tm, EXPERTS_PER_GROUP), F32)
    for g in range(N_GROUPS):
        lo = N_GROUPS + g * EXPERTS_PER_GROUP
        el = jnp.where(gidx == g, logits[:, lo:lo + EXPERTS_PER_GROUP], el)
    emax = jnp.max(el, axis=-1, keepdims=True)
    ee = jnp.exp(el - emax)
    ep = ee / jnp.sum(ee, axis=-1, keepdims=True)
    p1, i1 = first_argmax(ep)
    rest = jnp.where(lane4 == i1, -1.0, ep)
    p2, i2 = first_argmax(rest)
    den = p1 + p2
    w1 = g_w * p1 / den
    w2 = g_w * p2 / den
    lane = lax.broadcasted_iota(I32, (tm, LANES), 1)
    e1 = gidx * EXPERTS_PER_GROUP + i1
    e2 = gidx * EXPERTS_PER_GROUP + i2
    cw_ref[...] = jnp.where(lane == e1, w1, 0.0) + jnp.where(lane == e2, w2, 0.0)


def _route(x, g, sc, sh, w_rg, b_rg, w_re, b_re, tm=512):
    t, d = x.shape
    w_all = jnp.concatenate([w_rg, jnp.moveaxis(w_re, 0, 1).reshape(d, N_EXPERTS)], axis=1)
    b_all = jnp.concatenate([b_rg, b_re.reshape(N_EXPERTS)])
    npad = LANES - w_all.shape[1]
    w_all = jnp.pad(w_all, ((0, 0), (0, npad)))
    b_all = jnp.pad(b_all, (0, npad)).reshape(1, LANES)
    vec = pl.BlockSpec((1, d), lambda i: (0, 0))
    return pl.pallas_call(
        _route_kernel,
        grid=(t // tm,),
        in_specs=[pl.BlockSpec((tm, d), lambda i: (i, 0)), vec, vec, vec,
                  pl.BlockSpec((d, LANES), lambda i: (0, 0)), pl.BlockSpec((1, LANES), lambda i: (0, 0))],
        out_specs=[pl.BlockSpec((tm, d), lambda i: (i, 0)), pl.BlockSpec((tm, LANES), lambda i: (i, 0))],
        out_shape=[jax.ShapeDtypeStruct((t, d), BF16), jax.ShapeDtypeStruct((t, LANES), F32)],
        compiler_params=_cparams(("arbitrary",)),
        name="route",
    )(x, g.reshape(1, d), sc.reshape(1, d), sh.reshape(1, d), w_all, b_all)


def _rope_block(blk, cos, sin, half):
    if half == HEAD_DIM // 2:
        partner = pltpu.roll(blk, HEAD_DIM // 2, 1)
    else:
        lane = lax.broadcasted_iota(I32, blk.shape, 1)
        first = (lane % IDX_DIM) < (IDX_DIM // 2)
        partner = jnp.where(first, pltpu.roll(blk, LANES - IDX_DIM // 2, 1), pltpu.roll(blk, IDX_DIM // 2, 1))
    return blk * cos + partner * sin


def _proj_rope_kernel(a_ref, w_ref, cos_ref, sin_ref, o_ref, *, half):
    acc = jnp.dot(a_ref[...], w_ref[...], preferred_element_type=F32)
    cos = cos_ref[...]
    sin = sin_ref[...]
    for h in range(acc.shape[1] // LANES):
        sl = slice(h * LANES, (h + 1) * LANES)
        o_ref[:, sl] = _rope_block(acc[:, sl], cos, sin, half).astype(o_ref.dtype)


def _proj_rope(a, w, col_block0, n_out, cos, sin, half, out_dtype, tm=512, tn=512):
    t, d = a.shape
    return pl.pallas_call(
        functools.partial(_proj_rope_kernel, half=half),
        grid=(t // tm, n_out // tn),
        in_specs=[pl.BlockSpec((tm, d), lambda i, j: (i, 0)),
                  pl.BlockSpec((d, tn), lambda i, j: (0, col_block0 + j)),
                  pl.BlockSpec((tm, LANES), lambda i, j: (i, 0)),
                  pl.BlockSpec((tm, LANES), lambda i, j: (i, 0))],
        out_specs=pl.BlockSpec((tm, tn), lambda i, j: (i, j)),
        out_shape=jax.ShapeDtypeStruct((t, n_out), out_dtype),
        compiler_params=_cparams(("arbitrary", "arbitrary")),
        name="proj_rope",
    )(a, w, cos, sin)


def _proj_kv_kernel(a_ref, wk_ref, wv_ref, cos_ref, sin_ref, o_ref):
    a = a_ref[...]
    k = jnp.dot(a, wk_ref[...], preferred_element_type=F32)
    v = jnp.dot(a, wv_ref[...], preferred_element_type=F32)
    cos = cos_ref[...]
    sin = sin_ref[...]
    for h in range(k.shape[1] // LANES):
        sl = slice(h * LANES, (h + 1) * LANES)
        kr = _rope_block(k[:, sl], cos, sin, HEAD_DIM // 2)
        kb = lax.bitcast_convert_type(kr.astype(BF16).astype(F32), I32)
        vb = lax.bitcast_convert_type(v[:, sl].astype(BF16).astype(F32), I32)
        o_ref[:, sl] = kb | lax.shift_right_logical(vb, 16)


def _proj_kv(a, w, kblock, vblock, cos, sin, tm=512):
    t, d = a.shape
    n = N_KV_GROUPS * HEAD_DIM
    return pl.pallas_call(
        _proj_kv_kernel,
        grid=(t // tm,),
        in_specs=[pl.BlockSpec((tm, d), lambda i: (i, 0)),
                  pl.BlockSpec((d, n), lambda i: (0, kblock)),
                  pl.BlockSpec((d, n), lambda i: (0, vblock)),
                  pl.BlockSpec((tm, LANES), lambda i: (i, 0)),
                  pl.BlockSpec((tm, LANES), lambda i: (i, 0))],
        out_specs=pl.BlockSpec((tm, n), lambda i: (i, 0)),
        out_shape=jax.ShapeDtypeStruct((t, n), I32),
        compiler_params=_cparams(("arbitrary",)),
        name="proj_kv",
    )(a, w, w, cos, sin)


QB = 128
TK = 512


def _indexer_kernel(qi_ref, kit_ref, wi_ref, tri_ref, idx_ref, nv_ref, kk_ref, d_ref, *, n_keep):
    b = pl.program_id(0)
    q0 = b * QB
    n_t = (q0 + QB + TK - 1) // TK
    n_c = n_t * (TK // LANES)
    row = lax.broadcasted_iota(I32, (QB, TK), 0)
    col = lax.broadcasted_iota(I32, (QB, TK), 1)
    qchunk = (q0 + row) // CHUNK

    def admissible(k0):
        return ((k0 + col) // CHUNK) <= qchunk

    def score_tile(kt, carry):
        k0 = pl.multiple_of(kt * TK, TK)
        ki_t = kit_ref[:, pl.ds(k0, TK)]
        acc = jnp.zeros((QB, TK), F32)
        for h in range(IDX_HEADS):
            a = jnp.dot(qi_ref[:, h * IDX_DIM:(h + 1) * IDX_DIM], ki_t, preferred_element_type=F32)
            acc = acc + jnp.maximum(a, 0.0) * wi_ref[:, h:h + 1]
        s = jnp.where(admissible(k0), acc + 0.0, NEG_INF)
        u = lax.bitcast_convert_type(s, I32)
        kk_ref[:, pl.ds(k0, TK)] = jnp.where(u < 0, u ^ 0x7FFFFFFF, u)
        return carry

    lax.fori_loop(0, n_t, score_tile, 0)

    def count_where(pred):
        def body(kt, acc):
            k0 = pl.multiple_of(kt * TK, TK)
            hit = pred(kk_ref[:, pl.ds(k0, TK)]).astype(I32)
            for c in range(TK // LANES):
                acc = acc + hit[:, c * LANES:(c + 1) * LANES]
            return acc
        acc = lax.fori_loop(0, n_t, body, jnp.zeros((QB, LANES), I32))
        return jnp.sum(acc, axis=1, keepdims=True)

    def select_bit(it, v):
        bit = jnp.left_shift(jnp.int32(1), 31 - it)
        cand = jnp.where(it == 0, jnp.zeros_like(v), v | bit)
        cnt = count_where(lambda key: key >= cand)
        return jnp.where(cnt >= n_keep, cand, v)

    thr = lax.fori_loop(0, 32, select_bit, jnp.full((QB, 1), INT_MIN, I32))
    need = (n_keep - count_where(lambda key: key > thr)).astype(F32)

    tri = tri_ref[...]

    def rank_tile(kt, carry):
        run_eq, run_sel = carry
        k0 = pl.multiple_of(kt * TK, TK)
        key = kk_ref[:, pl.ds(k0, TK)]
        eq = key == thr
        eq_f = eq.astype(F32)
        pre_eq = jnp.dot(eq_f.astype(BF16), tri, preferred_element_type=F32)
        sel = admissible(k0) & ((key > thr) | (eq & ((run_eq + pre_eq - eq_f) < need)))
        pre_sel = jnp.dot(sel.astype(F32).astype(BF16), tri, preferred_element_type=F32)
        rank = (run_sel + pre_sel).astype(I32) - 1
        d_ref[:, pl.ds(k0, TK)] = jnp.where(sel, (k0 + col) - rank, INVALID_SHIFT)
        return run_eq + pre_eq[:, TK - 1:TK], run_sel + pre_sel[:, TK - 1:TK]

    zero = jnp.zeros((QB, 1), F32)
    _, n_sel = lax.fori_loop(0, n_t, rank_tile, (zero, zero))

    lane = lax.broadcasted_iota(I32, (QB, LANES), 1)

    def load_column(c):
        v = d_ref[:, pl.ds(pl.multiple_of(jnp.minimum(c, n_c - 1) * LANES, LANES), LANES)]
        return jnp.where(c < n_c, v, INVALID_SHIFT)

    n_stages = (d_ref.shape[1] - 1).bit_length()
    for s in range(n_stages):
        sh = 1 << s
        test = sh | INVALID_SHIFT

        def column(c, carry, sh=sh, test=test):
            own = d_ref[:, pl.ds(pl.multiple_of(c * LANES, LANES), LANES)]
            if sh < LANES:
                inc = jnp.where(lane < LANES - sh, pltpu.roll(own, LANES - sh, 1),
                                pltpu.roll(load_column(c + 1), LANES - sh, 1))
            else:
                inc = load_column(c + sh // LANES)
            take = (inc & test) == sh
            stay = (own & test) == 0
            d_ref[:, pl.ds(pl.multiple_of(c * LANES, LANES), LANES)] = jnp.where(
                take, inc, jnp.where(stay, own, INVALID_SHIFT))
            return carry

        lax.fori_loop(0, n_c, column, 0)

    slot = lax.broadcasted_iota(I32, (QB, n_keep), 1)
    dd = d_ref[:, 0:n_keep]
    idx_ref[...] = jnp.where(dd < INVALID_SHIFT, slot + dd, 0)
    nv_ref[...] = n_sel.astype(I32)


def _indexer(qi, kit, wi, n_keep):
    t = qi.shape[0]
    r = lax.broadcasted_iota(I32, (TK, TK), 0)
    c = lax.broadcasted_iota(I32, (TK, TK), 1)
    tri = (r <= c).astype(BF16)
    return pl.pallas_call(
        functools.partial(_indexer_kernel, n_keep=n_keep),
        grid=(t // QB,),
        in_specs=[pl.BlockSpec((QB, IDX_HEADS * IDX_DIM), lambda b: (b, 0)),
                  pl.BlockSpec((IDX_DIM, t), lambda b: (0, 0)),
                  pl.BlockSpec((QB, IDX_HEADS), lambda b: (b, 0)),
                  pl.BlockSpec((TK, TK), lambda b: (0, 0))],
        out_specs=[pl.BlockSpec((QB, n_keep), lambda b: (b, 0)), pl.BlockSpec((QB, 1), lambda b: (b, 0))],
        out_shape=[jax.ShapeDtypeStruct((t, n_keep), I32), jax.ShapeDtypeStruct((t, 1), I32)],
        scratch_shapes=[pltpu.VMEM((QB, t), I32), pltpu.VMEM((QB, t), I32)],
        compiler_params=_cparams(("arbitrary",)),
        name="indexer",
    )(qi, kit, wi, tri)


GQ = 64
KV_ROWS = N_KV_GROUPS


def _gattn_kernel(nv_ref, idx_hbm, q_ref, kv_hbm, o_ref, kv_vmem, idx_smem, stage, sem_kv, sem_idx, *, n_keep):
    i = pl.program_id(0)

    @pl.when(i == 0)
    def _():
        cp = pltpu.make_async_copy(kv_hbm, kv_vmem, sem_kv)
        cp.start()
        cp.wait()

    cp = pltpu.make_async_copy(idx_hbm.at[pl.ds(i * GQ, GQ)], idx_smem, sem_idx)
    cp.start()
    cp.wait()

    n_col = n_keep * KV_ROWS
    colg = lax.broadcasted_iota(I32, (N_HEADS, n_col), 1)
    rowh = lax.broadcasted_iota(I32, (N_HEADS, n_col), 0)
    own_group = (colg % KV_ROWS) == (rowh // HEADS_PER_GROUP)
    col_slot = colg // KV_ROWS
    scale = HEAD_DIM ** -0.5

    def per_query(r, carry):
        for j in range(n_keep):
            key = idx_smem[r, j]
            stage[j * KV_ROWS:(j + 1) * KV_ROWS, :] = kv_vmem[pl.ds(pl.multiple_of(key * KV_ROWS, KV_ROWS), KV_ROWS), :]
        st = stage[...]
        k_all = lax.bitcast_convert_type(st & jnp.int32(-65536), F32).astype(BF16)
        v_all = lax.bitcast_convert_type(jnp.left_shift(st, 16), F32).astype(BF16)
        qh = q_ref[pl.ds(pl.multiple_of(r * N_HEADS, N_HEADS), N_HEADS), :]
        s = lax.dot_general(qh, k_all, (((1,), (1,)), ((), ())), preferred_element_type=F32) * scale
        mask = own_group & (col_slot < nv_ref[i * GQ + r])
        s = jnp.where(mask, s, NEG_INF)
        m = jnp.max(s, axis=1, keepdims=True)
        p = jnp.where(mask, jnp.exp(s - m), 0.0)
        l = jnp.sum(p, axis=1, keepdims=True)
        o = jnp.dot(p.astype(BF16), v_all, preferred_element_type=F32) / l
        o_ref[pl.ds(pl.multiple_of(r * N_HEADS, N_HEADS), N_HEADS), :] = o.astype(o_ref.dtype)
        return carry

    lax.fori_loop(0, GQ, per_query, 0)


def _gattn(nv, idx, q2, kv2, n_keep):
    t = idx.shape[0]
    grid_spec = pltpu.PrefetchScalarGridSpec(
        num_scalar_prefetch=1,
        grid=(t // GQ,),
        in_specs=[pl.BlockSpec(memory_space=pl.ANY),
                  pl.BlockSpec((GQ * N_HEADS, HEAD_DIM), lambda i, nv: (i, 0)),
                  pl.BlockSpec(memory_space=pl.ANY)],
        out_specs=pl.BlockSpec((GQ * N_HEADS, HEAD_DIM), lambda i, nv: (i, 0)),
        scratch_shapes=[pltpu.VMEM(kv2.shape, I32),
                        pltpu.SMEM((GQ, n_keep), I32),
                        pltpu.VMEM((n_keep * KV_ROWS, LANES), I32),
                        pltpu.SemaphoreType.DMA,
                        pltpu.SemaphoreType.DMA],
    )
    return pl.pallas_call(
        functools.partial(_gattn_kernel, n_keep=n_keep),
        grid_spec=grid_spec,
        out_shape=jax.ShapeDtypeStruct((t * N_HEADS, HEAD_DIM), BF16),
        compiler_params=_cparams(("arbitrary",)),
        name="gattn",
    )(nv, idx, q2, kv2)


def _mm_res_kernel(a_ref, w_ref, b_ref, x_ref, g_ref, o_ref):
    y = jnp.dot(a_ref[...], w_ref[...], preferred_element_type=F32) + b_ref[...]
    o_ref[...] = x_ref[...] + g_ref[...] * y


def _mm_res(a, w, bias, x, gate, tm=512, tn=512):
    t, k = a.shape
    n = w.shape[1]
    return pl.pallas_call(
        _mm_res_kernel,
        grid=(t // tm, n // tn),
        in_specs=[pl.BlockSpec((tm, k), lambda i, j: (i, 0)),
                  pl.BlockSpec((k, tn), lambda i, j: (0, j)),
                  pl.BlockSpec((1, tn), lambda i, j: (0, j)),
                  pl.BlockSpec((tm, tn), lambda i, j: (i, j)),
                  pl.BlockSpec((1, tn), lambda i, j: (0, j))],
        out_specs=pl.BlockSpec((tm, tn), lambda i, j: (i, j)),
        out_shape=jax.ShapeDtypeStruct((t, n), F32),
        compiler_params=_cparams(("arbitrary", "arbitrary")),
        name="mm_res",
    )(a, w, bias.reshape(1, n), x, gate.reshape(1, n))


def _glu_kernel(a_ref, wa_ref, wg_ref, ba_ref, bg_ref, o_ref):
    a = a_ref[...]
    ya = jnp.dot(a, wa_ref[...], preferred_element_type=F32) + ba_ref[...]
    yg = jnp.dot(a, wg_ref[...], preferred_element_type=F32) + bg_ref[...]
    o_ref[...] = ya * jax.nn.sigmoid(yg)


def _glu(a, w, b, tm=512, tn=512):
    t, k = a.shape
    n = w.shape[1] // 2
    nb = n // tn
    b2 = b.reshape(1, 2 * n)
    return pl.pallas_call(
        _glu_kernel,
        grid=(t // tm, nb),
        in_specs=[pl.BlockSpec((tm, k), lambda i, j: (i, 0)),
                  pl.BlockSpec((k, tn), lambda i, j: (0, j)),
                  pl.BlockSpec((k, tn), lambda i, j: (0, j + nb)),
                  pl.BlockSpec((1, tn), lambda i, j: (0, j)),
                  pl.BlockSpec((1, tn), lambda i, j: (0, j + nb))],
        out_specs=pl.BlockSpec((tm, tn), lambda i, j: (i, j)),
        out_shape=jax.ShapeDtypeStruct((t, n), F32),
        compiler_params=_cparams(("arbitrary", "arbitrary")),
        name="glu",
    )(a, w, w, b2, b2)


HALO = 32


def _dwconv_ln_kernel(u_ref, prev_ref, w_ref, b_ref, g_ref, beta_ref, o_ref, ext_ref):
    i = pl.program_id(0)
    tm = u_ref.shape[0]
    ext_ref[0:HALO, :] = jnp.where(i == 0, 0.0, prev_ref[...])
    ext_ref[HALO:HALO + tm, :] = u_ref[...]
    acc = jnp.zeros(u_ref.shape, F32) + b_ref[...]
    for k in range(CONV_WIDTH):
        off = HALO - (CONV_WIDTH - 1) + k
        acc = acc + ext_ref[off:off + tm, :] * w_ref[k:k + 1, :]
    mu = jnp.mean(acc, axis=-1, keepdims=True)
    cen = acc - mu
    var = jnp.mean(cen * cen, axis=-1, keepdims=True)
    y = (cen * lax.rsqrt(var + EPS)) * g_ref[...] + beta_ref[...]
    o_ref[...] = (y * jax.nn.sigmoid(y)).astype(o_ref.dtype)


def _dwconv_ln(u, w_dw, b_dw, ln_g, ln_b, tm=256):
    t, d = u.shape
    ratio = tm // HALO
    vec = pl.BlockSpec((1, d), lambda i: (0, 0))
    return pl.pallas_call(
        _dwconv_ln_kernel,
        grid=(t // tm,),
        in_specs=[pl.BlockSpec((tm, d), lambda i: (i, 0)),
                  pl.BlockSpec((HALO, d), lambda i: (jnp.maximum(i * ratio - 1, 0), 0)),
                  pl.BlockSpec((HALO, d), lambda i: (0, 0)),
                  vec, vec, vec],
        out_specs=pl.BlockSpec((tm, d), lambda i: (i, 0)),
        out_shape=jax.ShapeDtypeStruct((t, d), BF16),
        scratch_shapes=[pltpu.VMEM((HALO + tm, d), F32)],
        compiler_params=_cparams(("arbitrary",)),
        name="dwconv_ln",
    )(u, u, jnp.pad(w_dw, ((0, HALO - CONV_WIDTH), (0, 0))), b_dw.reshape(1, d), ln_g.reshape(1, d), ln_b.reshape(1, d))


def _moe_kernel(h_ref, cw_ref, wg_ref, wu_ref, wd_ref, x_ref, g_ref, o_ref, acc_ref):
    e = pl.program_id(1)

    @pl.when(e == 0)
    def _():
        acc_ref[...] = jnp.zeros_like(acc_ref)

    h = h_ref[...]
    cw = cw_ref[...]
    lane = lax.broadcasted_iota(I32, cw.shape, 1)
    w_e = jnp.sum(jnp.where(lane == e, cw, 0.0), axis=1, keepdims=True)
    gate = jnp.dot(h, wg_ref[0], preferred_element_type=F32)
    up = jnp.dot(h, wu_ref[0], preferred_element_type=F32)
    hid = (gate * jax.nn.sigmoid(gate)) * up
    y = jnp.dot(hid.astype(BF16), wd_ref[0], preferred_element_type=F32)
    acc_ref[...] += y * w_e

    @pl.when(e == pl.num_programs(1) - 1)
    def _():
        o_ref[...] = x_ref[...] + g_ref[...] * acc_ref[...]


def _moe(h, cw, w_gate, w_up, w_down, x, gate, tm=512):
    t, d = h.shape
    ne, _, de = w_gate.shape
    return pl.pallas_call(
        _moe_kernel,
        grid=(t // tm, ne),
        in_specs=[pl.BlockSpec((tm, d), lambda i, e: (i, 0)),
                  pl.BlockSpec((tm, LANES), lambda i, e: (i, 0)),
                  pl.BlockSpec((1, d, de), lambda i, e: (e, 0, 0)),
                  pl.BlockSpec((1, d, de), lambda i, e: (e, 0, 0)),
                  pl.BlockSpec((1, de, d), lambda i, e: (e, 0, 0)),
                  pl.BlockSpec((tm, d), lambda i, e: (i, 0)),
                  pl.BlockSpec((1, d), lambda i, e: (0, 0))],
        out_specs=pl.BlockSpec((tm, d), lambda i, e: (i, 0)),
        out_shape=jax.ShapeDtypeStruct((t, d), F32),
        scratch_shapes=[pltpu.VMEM((tm, d), F32)],
        compiler_params=_cparams(("arbitrary", "arbitrary")),
        name="moe",
    )(h, cw, w_gate, w_up, w_down, x, gate.reshape(1, d))


def _rms_kernel(x_ref, g_ref, o_ref):
    x = x_ref[...]
    o_ref[...] = (x * lax.rsqrt(jnp.mean(x * x, axis=-1, keepdims=True) + EPS)) * g_ref[...]


def _final_norm(x, g, tm=512):
    t, d = x.shape
    return pl.pallas_call(
        _rms_kernel,
        grid=(t // tm,),
        in_specs=[pl.BlockSpec((tm, d), lambda i: (i, 0)), pl.BlockSpec((1, d), lambda i: (0, 0))],
        out_specs=pl.BlockSpec((tm, d), lambda i: (i, 0)),
        out_shape=jax.ShapeDtypeStruct((t, d), F32),
        compiler_params=_cparams(("arbitrary",)),
        name="final_norm",
    )(x, g.reshape(1, d))


def _rope_tables(t):
    pos = jnp.arange(t, dtype=I32).astype(F32)[:, None]
    lane = jnp.arange(LANES)

    def table(dim):
        half = dim // 2
        inv = ROPE_THETA ** (-jnp.arange(half, dtype=F32) / half)
        ang = pos * inv[None, :]
        j = lane % dim
        cos = jnp.cos(ang)[:, j % half]
        sin = jnp.sin(ang)[:, j % half] * jnp.where(j < half, -1.0, 1.0)[None, :]
        return cos, sin

    return table(HEAD_DIM), table(IDX_DIM)


def _attention_layer(x, mod, norm_g, w_in, w_out, tables):
    t, d = x.shape
    sh1, sc1, g1 = mod[0:d], mod[d:2 * d], mod[2 * d:3 * d]
    (cos_h, sin_h), (cos_i, sin_i) = tables
    hn = _normmod(x, norm_g, sc1, sh1)
    w_bf = w_in.astype(BF16)
    nq = N_HEADS * HEAD_DIM
    nkv = N_KV_GROUPS * HEAD_DIM
    nqi = IDX_HEADS * IDX_DIM
    tn = 512
    q = _proj_rope(hn, w_bf, 0, nq, cos_h, sin_h, HEAD_DIM // 2, BF16)
    kv = _proj_kv(hn, w_bf, nq // nkv, nq // nkv + 1, cos_h, sin_h)
    qi = _proj_rope(hn, w_bf, (nq + 2 * nkv) // tn, nqi, cos_i, sin_i, IDX_DIM // 2, BF16)
    tail0 = nq + 2 * nkv + nqi
    w_tail = jnp.pad(w_bf[:, tail0:], ((0, 0), (0, LANES - (IDX_DIM + IDX_HEADS))))
    lane = jnp.arange(LANES)
    wi_scale = (IDX_HEADS ** -0.5) * (IDX_DIM ** -0.5)
    cos_t = jnp.where(lane[None, :] < IDX_DIM, cos_i, wi_scale)
    sin_t = jnp.where(lane[None, :] < IDX_DIM, sin_i, 0.0)
    tail = _proj_rope(hn, w_tail, 0, LANES, cos_t, sin_t, IDX_DIM // 2, F32, tn=LANES)
    kit = tail[:, :IDX_DIM].astype(BF16).T
    wi = tail[:, IDX_DIM:IDX_DIM + IDX_HEADS]
    n_keep = min(TOPK_KEYS, t // 4)
    idx, nv = _indexer(qi, kit, wi, n_keep)
    att = _gattn(nv.reshape(t), idx, q.reshape(t * N_HEADS, HEAD_DIM), kv.reshape(t * KV_ROWS, LANES), n_keep)
    att = att.reshape(t, nq)
    return _mm_res(att, w_out.astype(BF16), jnp.zeros((d,), F32), x, g1)


def _conv_layer(x, mod, norm_g, w_pw1, b_pw1, w_dw, b_dw, ln_g, ln_b, w_pw2, b_pw2):
    t, d = x.shape
    sh1, sc1, g1 = mod[0:d], mod[d:2 * d], mod[2 * d:3 * d]
    hn = _normmod(x, norm_g, sc1, sh1)
    u = _glu(hn, w_pw1.astype(BF16), b_pw1)
    v = _dwconv_ln(u, w_dw, b_dw, ln_g, ln_b)
    return _mm_res(v, w_pw2.astype(BF16), b_pw2, x, g1)


def _moe_layer(x, mod, norm_g, w_rg, b_rg, w_re, b_re, w_gate, w_up, w_down):
    t, d = x.shape
    sh2, sc2, g2 = mod[3 * d:4 * d], mod[4 * d:5 * d], mod[5 * d:6 * d]
    h, cw = _route(x, norm_g, sc2, sh2, w_rg, b_rg, w_re, b_re)
    return _moe(h, cw, w_gate.astype(BF16), w_up.astype(BF16), w_down.astype(BF16), x, g2)


def kernel(x, c, ada_w, ada_b, norm1_g, norm2_g, attn_w_in, attn_w_out, conv_w_pw1, conv_b_pw1, conv_w_dw, conv_b_dw, conv_ln_g, conv_ln_b, conv_w_pw2, conv_b_pw2, router_w_group, router_b_group, router_w_expert, router_b_expert, exp_w_gate, exp_w_up, exp_w_down, final_g):
    bsz, t, d = x.shape
    assert bsz == 1 and d == N_HEADS * HEAD_DIM
    depth = ada_w.shape[0]
    mods = _ada_mod(c, ada_w, ada_b)
    tables = _rope_tables(t)
    xf = x.reshape(t, d)
    for i in range(depth):
        j = i // 2
        if i % 2 == 0:
            xf = _attention_layer(xf, mods[i], norm1_g[i], attn_w_in[j], attn_w_out[j], tables)
        else:
            xf = _conv_layer(xf, mods[i], norm1_g[i], conv_w_pw1[j], conv_b_pw1[j], conv_w_dw[j], conv_b_dw[j],
                             conv_ln_g[j], conv_ln_b[j], conv_w_pw2[j], conv_b_pw2[j])
        xf = _moe_layer(xf, mods[i], norm2_g[i], router_w_group[i], router_b_group[i], router_w_expert[i],
                        router_b_expert[i], exp_w_gate[i], exp_w_up[i], exp_w_down[i])
    return _final_norm(xf, final_g).reshape(bsz, t, d)
```

```python
import functools

import jax
import jax.numpy as jnp
from jax import lax
from jax.experimental import pallas as pl
from jax.experimental.pallas import tpu as pltpu

EPS = 1e-6
CHUNK = 64
N_HEADS = 16
N_KV_GROUPS = 4
HEADS_PER_GROUP = N_HEADS // N_KV_GROUPS
HEAD_DIM = 128
IDX_HEADS = 16
IDX_DIM = 64
TOPK_KEYS = 256
ROPE_THETA = 10000.0
NEG_INF = -1e30
CONV_WIDTH = 31
N_GROUPS = 4
EXPERTS_PER_GROUP = 4
N_EXPERTS = N_GROUPS * EXPERTS_PER_GROUP

LANES = 128
KV_ROWS = N_KV_GROUPS
INT_MIN = -(2 ** 31)
INVALID_SHIFT = 1 << 30
VMEM_LIMIT = 56 * 1024 * 1024

F32 = jnp.float32
BF16 = jnp.bfloat16
I32 = jnp.int32


def _cparams(sem):
    return pltpu.CompilerParams(dimension_semantics=sem, vmem_limit_bytes=VMEM_LIMIT)


def _ada_kernel(c_ref, w_ref, b_ref, o_ref):
    c = c_ref[...]
    ca = c * jax.nn.sigmoid(c)
    o_ref[0] = jnp.dot(ca, w_ref[0], preferred_element_type=F32, precision=lax.Precision.HIGHEST) + b_ref[0]


def _ada_mod(c, ada_w, ada_b):
    depth, d, n = ada_w.shape
    tn = 768
    c8 = jnp.zeros((8, d), F32).at[0].set(c[0])
    out = pl.pallas_call(
        _ada_kernel,
        grid=(depth, n // tn),
        in_specs=[pl.BlockSpec((8, d), lambda i, j: (0, 0)),
                  pl.BlockSpec((1, d, tn), lambda i, j: (i, 0, j)),
                  pl.BlockSpec((1, 1, tn), lambda i, j: (i, 0, j))],
        out_specs=pl.BlockSpec((1, 8, tn), lambda i, j: (i, 0, j)),
        out_shape=jax.ShapeDtypeStruct((depth, 8, n), F32),
        compiler_params=_cparams(("arbitrary", "arbitrary")),
        name="ada_mod",
    )(c8, ada_w, ada_b.reshape(depth, 1, n))
    return out[:, 0, :]


def _normmod_kernel(x_ref, g_ref, sc_ref, sh_ref, o_ref):
    x = x_ref[...]
    xn = x * lax.rsqrt(jnp.mean(x * x, axis=-1, keepdims=True) + EPS)
    o_ref[...] = ((xn * g_ref[...]) * (1.0 + sc_ref[...]) + sh_ref[...]).astype(o_ref.dtype)


def _normmod(x, g, sc, sh, tm=512):
    t, d = x.shape
    vec = pl.BlockSpec((1, d), lambda i: (0, 0))
    return pl.pallas_call(
        _normmod_kernel,
        grid=(t // tm,),
        in_specs=[pl.BlockSpec((tm, d), lambda i: (i, 0)), vec, vec, vec],
        out_specs=pl.BlockSpec((tm, d), lambda i: (i, 0)),
        out_shape=jax.ShapeDtypeStruct((t, d), BF16),
        compiler_params=_cparams(("arbitrary",)),
        name="normmod",
    )(x, g.reshape(1, d), sc.reshape(1, d), sh.reshape(1, d))


def _route_kernel(x_ref, g_ref, sc_ref, sh_ref, wr_ref, br_ref, h_ref, cw_ref):
    x = x_ref[...]
    xn = x * lax.rsqrt(jnp.mean(x * x, axis=-1, keepdims=True) + EPS)
    h = (xn * g_ref[...]) * (1.0 + sc_ref[...]) + sh_ref[...]
    h_ref[...] = h.astype(h_ref.dtype)
    logits = jnp.dot(h, wr_ref[...], preferred_element_type=F32, precision=lax.Precision.HIGHEST) + br_ref[...]
    tm = x.shape[0]
    lane4 = lax.broadcasted_iota(I32, (tm, N_GROUPS), 1)

    def first_argmax(v):
        m = jnp.max(v, axis=-1, keepdims=True)
        return m, jnp.min(jnp.where(v == m, lane4, N_GROUPS), axis=-1, keepdims=True)

    gl = logits[:, 0:N_GROUPS]
    gmax, gidx = first_argmax(gl)
    g_w = 1.0 / jnp.sum(jnp.exp(gl - gmax), axis=-1, keepdims=True)
    el = jnp.zeros((tm, EXPERTS_PER_GROUP), F32)
    for g in range(N_GROUPS):
        lo = N_GROUPS + g * EXPERTS_PER_GROUP
        el = jnp.where(gidx == g, logits[:, lo:lo + EXPERTS_PER_GROUP], el)
    emax = jnp.max(el, axis=-1, keepdims=True)
    ee = jnp.exp(el - emax)
    ep = ee / jnp.sum(ee, axis=-1, keepdims=True)
    p1, i1 = first_argmax(ep)
    rest = jnp.where(lane4 == i1, -1.0, ep)
    p2, i2 = first_argmax(rest)
    den = p1 + p2
    w1 = g_w * p1 / den
    w2 = g_w * p2 / den
    lane = lax.broadcasted_iota(I32, (tm, LANES), 1)
    e1 = gidx * EXPERTS_PER_GROUP + i1
    e2 = gidx * EXPERTS_PER_GROUP + i2
    cw_ref[...] = jnp.where(lane == e1, w1, 0.0) + jnp.where(lane == e2, w2, 0.0)


def _route(x, g, sc, sh, w_rg, b_rg, w_re, b_re, tm=512):
    t, d = x.shape
    w_all = jnp.concatenate([w_rg, jnp.moveaxis(w_re, 0, 1).reshape(d, N_EXPERTS)], axis=1)
    b_all = jnp.concatenate([b_rg, b_re.reshape(N_EXPERTS)])
    npad = LANES - w_all.shape[1]
    w_all = jnp.pad(w_all, ((0, 0), (0, npad)))
    b_all = jnp.pad(b_all, (0, npad)).reshape(1, LANES)
    vec = pl.BlockSpec((1, d), lambda i: (0, 0))
    return pl.pallas_call(
        _route_kernel,
        grid=(t // tm,),
        in_specs=[pl.BlockSpec((tm, d), lambda i: (i, 0)), vec, vec, vec,
                  pl.BlockSpec((d, LANES), lambda i: (0, 0)), pl.BlockSpec((1, LANES), lambda i: (0, 0))],
        out_specs=[pl.BlockSpec((tm, d), lambda i: (i, 0)), pl.BlockSpec((tm, LANES), lambda i: (i, 0))],
        out_shape=[jax.ShapeDtypeStruct((t, d), BF16), jax.ShapeDtypeStruct((t, LANES), F32)],
        compiler_params=_cparams(("arbitrary",)),
        name="route",
    )(x, g.reshape(1, d), sc.reshape(1, d), sh.reshape(1, d), w_all, b_all)


def _rope_block(blk, cos, sin, half):
    if half == HEAD_DIM // 2:
        partner = pltpu.roll(blk, HEAD_DIM // 2, 1)
    else:
        lane = lax.broadcasted_iota(I32, blk.shape, 1)
        first = (lane % IDX_DIM) < (IDX_DIM // 2)
        partner = jnp.where(first, pltpu.roll(blk, LANES - IDX_DIM // 2, 1), pltpu.roll(blk, IDX_DIM // 2, 1))
    return blk * cos + partner * sin


def _proj_rope_kernel(a_ref, w_ref, cos_ref, sin_ref, o_ref, *, half):
    acc = jnp.dot(a_ref[...], w_ref[...], preferred_element_type=F32)
    cos = cos_ref[...]
    sin = sin_ref[...]
    for h in range(acc.shape[1] // LANES):
        sl = slice(h * LANES, (h + 1) * LANES)
        o_ref[:, sl] = _rope_block(acc[:, sl], cos, sin, half).astype(o_ref.dtype)


def _proj_rope(a, w, col_block0, n_out, cos, sin, half, out_dtype, tm=512, tn=512):
    t, d = a.shape
    return pl.pallas_call(
        functools.partial(_proj_rope_kernel, half=half),
        grid=(t // tm, n_out // tn),
        in_specs=[pl.BlockSpec((tm, d), lambda i, j: (i, 0)),
                  pl.BlockSpec((d, tn), lambda i, j: (0, col_block0 + j)),
                  pl.BlockSpec((tm, LANES), lambda i, j: (i, 0)),
                  pl.BlockSpec((tm, LANES), lambda i, j: (i, 0))],
        out_specs=pl.BlockSpec((tm, tn), lambda i, j: (i, j)),
        out_shape=jax.ShapeDtypeStruct((t, n_out), out_dtype),
        compiler_params=_cparams(("arbitrary", "arbitrary")),
        name="proj_rope",
    )(a, w, cos, sin)


def _proj_kv_kernel(a_ref, wk_ref, wv_ref, cos_ref, sin_ref, o_ref):
    a = a_ref[...]
    k = jnp.dot(a, wk_ref[...], preferred_element_type=F32)
    v = jnp.dot(a, wv_ref[...], preferred_element_type=F32)
    cos = cos_ref[...]
    sin = sin_ref[...]
    for h in range(k.shape[1] // LANES):
        sl = slice(h * LANES, (h + 1) * LANES)
        kr = _rope_block(k[:, sl], cos, sin, HEAD_DIM // 2)
        kb = lax.bitcast_convert_type(kr.astype(BF16).astype(F32), I32)
        vb = lax.bitcast_convert_type(v[:, sl].astype(BF16).astype(F32), I32)
        o_ref[:, sl] = kb | lax.shift_right_logical(vb, 16)


def _proj_kv(a, w, kblock, vblock, cos, sin, tm=512):
    t, d = a.shape
    n = N_KV_GROUPS * HEAD_DIM
    return pl.pallas_call(
        _proj_kv_kernel,
        grid=(t // tm,),
        in_specs=[pl.BlockSpec((tm, d), lambda i: (i, 0)),
                  pl.BlockSpec((d, n), lambda i: (0, kblock)),
                  pl.BlockSpec((d, n), lambda i: (0, vblock)),
                  pl.BlockSpec((tm, LANES), lambda i: (i, 0)),
                  pl.BlockSpec((tm, LANES), lambda i: (i, 0))],
        out_specs=pl.BlockSpec((tm, n), lambda i: (i, 0)),
        out_shape=jax.ShapeDtypeStruct((t, n), I32),
        compiler_params=_cparams(("arbitrary",)),
        name="proj_kv",
    )(a, w, w, cos, sin)


QB = 256
TK = 512
RC = 64


def _indexer_kernel(qit_ref, ki_ref, wit_ref, tri_ref, idx_ref, nv_ref, d_ref, *, n_keep):
    b = pl.program_id(0)
    q0 = b * QB
    n_t = (q0 + QB + TK - 1) // TK
    n_rows = n_t * TK
    rowi = lax.broadcasted_iota(I32, (TK, QB), 0)
    qchunk = (q0 + lax.broadcasted_iota(I32, (TK, QB), 1)) // CHUNK

    def admissible(k0):
        return ((k0 + rowi) // CHUNK) <= qchunk

    def score_tile(kt, carry):
        k0 = pl.multiple_of(kt * TK, TK)
        ki_t = ki_ref[pl.ds(k0, TK), :]
        acc = jnp.zeros((TK, QB), F32)
        for h in range(IDX_HEADS):
            a = jnp.dot(ki_t, qit_ref[h * IDX_DIM:(h + 1) * IDX_DIM, :], preferred_element_type=F32)
            acc = acc + jnp.maximum(a, 0.0) * wit_ref[h:h + 1, :]
        s = jnp.where(admissible(k0), acc + 0.0, NEG_INF)
        u = lax.bitcast_convert_type(s, I32)
        d_ref[pl.ds(k0, TK), :] = jnp.where(u < 0, u ^ 0x7FFFFFFF, u)
        return carry

    lax.fori_loop(0, n_t, score_tile, 0)

    def count_where(pred):
        def body(kt, acc):
            k0 = pl.multiple_of(kt * TK, TK)
            hit = pred(d_ref[pl.ds(k0, TK), :]).astype(I32)
            return acc + jnp.sum(hit.reshape(TK // 8, 8, QB), axis=0)
        acc = lax.fori_loop(0, n_t, body, jnp.zeros((8, QB), I32))
        return jnp.sum(acc, axis=0, keepdims=True)

    def select_bit(it, v):
        bit = jnp.left_shift(jnp.int32(1), 31 - it)
        cand = jnp.where(it == 0, jnp.zeros_like(v), v | bit)
        cnt = count_where(lambda key: key >= cand)
        return jnp.where(cnt >= n_keep, cand, v)

    thr = lax.fori_loop(0, 32, select_bit, jnp.full((1, QB), INT_MIN, I32))
    need = (n_keep - count_where(lambda key: key > thr)).astype(F32)

    tri = tri_ref[...]

    def rank_tile(kt, carry):
        run_eq, run_sel = carry
        k0 = pl.multiple_of(kt * TK, TK)
        key = d_ref[pl.ds(k0, TK), :]
        eq = key == thr
        eq_f = eq.astype(F32)
        pre_eq = jnp.dot(tri, eq_f.astype(BF16), preferred_element_type=F32)
        sel = admissible(k0) & ((key > thr) | (eq & ((run_eq + pre_eq - eq_f) < need)))
        pre_sel = jnp.dot(tri, sel.astype(F32).astype(BF16), preferred_element_type=F32)
        rank = (run_sel + pre_sel).astype(I32) - 1
        d_ref[pl.ds(k0, TK), :] = jnp.where(sel, (k0 + rowi) - rank, INVALID_SHIFT)
        return run_eq + pre_eq[TK - 1:TK, :], run_sel + pre_sel[TK - 1:TK, :]

    zero = jnp.zeros((1, QB), F32)
    _, n_sel = lax.fori_loop(0, n_t, rank_tile, (zero, zero))

    d_ref[pl.ds(pl.multiple_of(n_rows, RC), RC), :] = jnp.full((RC, QB), INVALID_SHIFT, I32)
    n_chunks = n_rows // RC
    n_stages = (d_ref.shape[0] - RC - 1).bit_length()
    for s in range(n_stages):
        sh = 1 << s
        test = sh | INVALID_SHIFT

        def merge(own, inc, sh=sh, test=test):
            take = (inc & test) == sh
            stay = (own & test) == 0
            return jnp.where(take, inc, jnp.where(stay, own, INVALID_SHIFT))

        def chunk_with_source(c, carry, sh=sh, merge=merge):
            r0 = pl.multiple_of(c * RC, RC)
            if sh < 8:
                x = d_ref[pl.ds(r0, RC + 8), :]
                own, inc = x[0:RC], x[sh:sh + RC]
            else:
                own = d_ref[pl.ds(r0, RC), :]
                inc = d_ref[pl.ds(pl.multiple_of(r0 + sh, 8), RC), :]
            d_ref[pl.ds(r0, RC), :] = merge(own, inc)
            return carry

        def chunk_without_source(c, carry, merge=merge):
            r0 = pl.multiple_of(c * RC, RC)
            own = d_ref[pl.ds(r0, RC), :]
            d_ref[pl.ds(r0, RC), :] = merge(own, jnp.full_like(own, INVALID_SHIFT))
            return carry

        if sh < RC:
            lax.fori_loop(0, n_chunks, chunk_with_source, 0)
        else:
            n_src = jnp.maximum(n_rows - sh, 0) // RC
            lax.fori_loop(0, n_src, chunk_with_source, 0)
            lax.fori_loop(n_src, n_chunks, chunk_without_source, 0)

    slot = lax.broadcasted_iota(I32, (n_keep, QB), 0)
    dd = d_ref[0:n_keep, :]
    idx_ref[...] = jnp.where(dd < INVALID_SHIFT, (slot + dd) * KV_ROWS, 0)
    nv_ref[...] = n_sel.astype(I32)


def _indexer(qit, ki, wit, n_keep):
    t = ki.shape[0]
    r = lax.broadcasted_iota(I32, (TK, TK), 0)
    c = lax.broadcasted_iota(I32, (TK, TK), 1)
    tri = (c <= r).astype(BF16)
    return pl.pallas_call(
        functools.partial(_indexer_kernel, n_keep=n_keep),
        grid=(t // QB,),
        in_specs=[pl.BlockSpec((IDX_HEADS * IDX_DIM, QB), lambda b: (0, b)),
                  pl.BlockSpec((t, IDX_DIM), lambda b: (0, 0)),
                  pl.BlockSpec((IDX_HEADS, QB), lambda b: (0, b)),
                  pl.BlockSpec((TK, TK), lambda b: (0, 0))],
        out_specs=[pl.BlockSpec((n_keep, QB), lambda b: (0, b)), pl.BlockSpec((1, QB), lambda b: (0, b))],
        out_shape=[jax.ShapeDtypeStruct((n_keep, t), I32), jax.ShapeDtypeStruct((1, t), I32)],
        scratch_shapes=[pltpu.VMEM((t + RC, QB), I32)],
        compiler_params=_cparams(("arbitrary",)),
        name="indexer",
    )(qit, ki, wit, tri)


GQ = 64


def _gattn_kernel(nv_ref, idx_hbm, q_ref, kv_hbm, o_ref, kv_vmem, idx_smem, stage_a, stage_b, stage_c, stage_d,
                  s_a, s_b, sem_kv, sem_idx, *, n_keep):
    i = pl.program_id(0)

    @pl.when(i == 0)
    def _():
        cp = pltpu.make_async_copy(kv_hbm, kv_vmem, sem_kv)
        cp.start()
        cp.wait()

    cp = pltpu.make_async_copy(idx_hbm.at[pl.ds(i * (GQ * n_keep), GQ * n_keep)], idx_smem, sem_idx)
    cp.start()
    cp.wait()

    n_col = n_keep * KV_ROWS
    colg = lax.broadcasted_iota(I32, (N_HEADS, n_col), 1)
    rowh = lax.broadcasted_iota(I32, (N_HEADS, n_col), 0)
    own_group = (colg % KV_ROWS) == (rowh // HEADS_PER_GROUP)
    col_slot = colg // KV_ROWS
    scale = HEAD_DIM ** -0.5

    def gather(r, stage):
        base = r * n_keep
        for j in range(n_keep):
            row = pl.multiple_of(idx_smem[base + j], KV_ROWS)
            stage[j * KV_ROWS:(j + 1) * KV_ROWS, :] = kv_vmem[pl.ds(row, KV_ROWS), :]

    def scores(r, stage, s_buf):
        k_all = lax.bitcast_convert_type(stage[...] & jnp.int32(-65536), F32).astype(BF16)
        qh = q_ref[pl.ds(pl.multiple_of(r * N_HEADS, N_HEADS), N_HEADS), :]
        s = lax.dot_general(qh, k_all, (((1,), (1,)), ((), ())), preferred_element_type=F32) * scale
        mask = own_group & (col_slot < nv_ref[i * GQ + r])
        s_buf[...] = jnp.where(mask, s, NEG_INF)

    def combine(r, stage, s_buf):
        s = s_buf[...]
        p = jnp.exp(s - jnp.max(s, axis=1, keepdims=True))
        l = jnp.sum(p, axis=1, keepdims=True)
        v_all = lax.bitcast_convert_type(jnp.left_shift(stage[...], 16), F32).astype(BF16)
        o = jnp.dot(p.astype(BF16), v_all, preferred_element_type=F32) / l
        o_ref[pl.ds(pl.multiple_of(r * N_HEADS, N_HEADS), N_HEADS), :] = o.astype(o_ref.dtype)

    stages = (stage_a, stage_b, stage_c, stage_d)
    s_bufs = (s_a, s_b)
    last = GQ - 1
    gather(0, stages[0])
    gather(1, stages[1])
    scores(0, stages[0], s_bufs[0])

    def query_quad(it, carry):
        r = 4 * it
        for u in range(4):
            gather(jnp.minimum(r + u + 2, last), stages[(u + 2) % 4])
            scores(jnp.minimum(r + u + 1, last), stages[(u + 1) % 4], s_bufs[(u + 1) % 2])
            combine(r + u, stages[u], s_bufs[u % 2])
        return carry

    lax.fori_loop(0, GQ // 4, query_quad, 0)


def _gattn(nv, idx, q2, kv2, n_keep):
    t = nv.shape[0]
    grid_spec = pltpu.PrefetchScalarGridSpec(
        num_scalar_prefetch=1,
        grid=(t // GQ,),
        in_specs=[pl.BlockSpec(memory_space=pl.ANY),
                  pl.BlockSpec((GQ * N_HEADS, HEAD_DIM), lambda i, nv: (i, 0)),
                  pl.BlockSpec(memory_space=pl.ANY)],
        out_specs=pl.BlockSpec((GQ * N_HEADS, HEAD_DIM), lambda i, nv: (i, 0)),
        scratch_shapes=[pltpu.VMEM(kv2.shape, I32),
                        pltpu.SMEM((GQ * n_keep,), I32),
                        pltpu.VMEM((n_keep * KV_ROWS, LANES), I32),
                        pltpu.VMEM((n_keep * KV_ROWS, LANES), I32),
                        pltpu.VMEM((n_keep * KV_ROWS, LANES), I32),
                        pltpu.VMEM((n_keep * KV_ROWS, LANES), I32),
                        pltpu.VMEM((N_HEADS, n_keep * KV_ROWS), F32),
                        pltpu.VMEM((N_HEADS, n_keep * KV_ROWS), F32),
                        pltpu.SemaphoreType.DMA,
                        pltpu.SemaphoreType.DMA],
    )
    return pl.pallas_call(
        functools.partial(_gattn_kernel, n_keep=n_keep),
        grid_spec=grid_spec,
        out_shape=jax.ShapeDtypeStruct((t * N_HEADS, HEAD_DIM), BF16),
        compiler_params=_cparams(("arbitrary",)),
        name="gattn",
    )(nv, idx, q2, kv2)


def _mm_res_kernel(a_ref, w_ref, b_ref, x_ref, g_ref, o_ref):
    y = jnp.dot(a_ref[...], w_ref[...], preferred_element_type=F32) + b_ref[...]
    o_ref[...] = x_ref[...] + g_ref[...] * y


def _mm_res(a, w, bias, x, gate, tm=512, tn=512):
    t, k = a.shape
    n = w.shape[1]
    return pl.pallas_call(
        _mm_res_kernel,
        grid=(t // tm, n // tn),
        in_specs=[pl.BlockSpec((tm, k), lambda i, j: (i, 0)),
                  pl.BlockSpec((k, tn), lambda i, j: (0, j)),
                  pl.BlockSpec((1, tn), lambda i, j: (0, j)),
                  pl.BlockSpec((tm, tn), lambda i, j: (i, j)),
                  pl.BlockSpec((1, tn), lambda i, j: (0, j))],
        out_specs=pl.BlockSpec((tm, tn), lambda i, j: (i, j)),
        out_shape=jax.ShapeDtypeStruct((t, n), F32),
        compiler_params=_cparams(("arbitrary", "arbitrary")),
        name="mm_res",
    )(a, w, bias.reshape(1, n), x, gate.reshape(1, n))


def _glu_kernel(a_ref, wa_ref, wg_ref, ba_ref, bg_ref, o_ref):
    a = a_ref[...]
    ya = jnp.dot(a, wa_ref[...], preferred_element_type=F32) + ba_ref[...]
    yg = jnp.dot(a, wg_ref[...], preferred_element_type=F32) + bg_ref[...]
    o_ref[...] = ya * jax.nn.sigmoid(yg)


def _glu(a, w, b, tm=512, tn=512):
    t, k = a.shape
    n = w.shape[1] // 2
    nb = n // tn
    b2 = b.reshape(1, 2 * n)
    return pl.pallas_call(
        _glu_kernel,
        grid=(t // tm, nb),
        in_specs=[pl.BlockSpec((tm, k), lambda i, j: (i, 0)),
                  pl.BlockSpec((k, tn), lambda i, j: (0, j)),
                  pl.BlockSpec((k, tn), lambda i, j: (0, j + nb)),
                  pl.BlockSpec((1, tn), lambda i, j: (0, j)),
                  pl.BlockSpec((1, tn), lambda i, j: (0, j + nb))],
        out_specs=pl.BlockSpec((tm, tn), lambda i, j: (i, j)),
        out_shape=jax.ShapeDtypeStruct((t, n), F32),
        compiler_params=_cparams(("arbitrary", "arbitrary")),
        name="glu",
    )(a, w, w, b2, b2)


HALO = 32


def _dwconv_ln_kernel(u_ref, prev_ref, w_ref, b_ref, g_ref, beta_ref, o_ref, ext_ref):
    i = pl.program_id(0)
    tm = u_ref.shape[0]
    ext_ref[0:HALO, :] = jnp.where(i == 0, 0.0, prev_ref[...])
    ext_ref[HALO:HALO + tm, :] = u_ref[...]
    acc = jnp.zeros(u_ref.shape, F32) + b_ref[...]
    for k in range(CONV_WIDTH):
        off = HALO - (CONV_WIDTH - 1) + k
        acc = acc + ext_ref[off:off + tm, :] * w_ref[k:k + 1, :]
    mu = jnp.mean(acc, axis=-1, keepdims=True)
    cen = acc - mu
    var = jnp.mean(cen * cen, axis=-1, keepdims=True)
    y = (cen * lax.rsqrt(var + EPS)) * g_ref[...] + beta_ref[...]
    o_ref[...] = (y * jax.nn.sigmoid(y)).astype(o_ref.dtype)


def _dwconv_ln(u, w_dw, b_dw, ln_g, ln_b, tm=256):
    t, d = u.shape
    ratio = tm // HALO
    vec = pl.BlockSpec((1, d), lambda i: (0, 0))
    return pl.pallas_call(
        _dwconv_ln_kernel,
        grid=(t // tm,),
        in_specs=[pl.BlockSpec((tm, d), lambda i: (i, 0)),
                  pl.BlockSpec((HALO, d), lambda i: (jnp.maximum(i * ratio - 1, 0), 0)),
                  pl.BlockSpec((HALO, d), lambda i: (0, 0)),
                  vec, vec, vec],
        out_specs=pl.BlockSpec((tm, d), lambda i: (i, 0)),
        out_shape=jax.ShapeDtypeStruct((t, d), BF16),
        scratch_shapes=[pltpu.VMEM((HALO + tm, d), F32)],
        compiler_params=_cparams(("arbitrary",)),
        name="dwconv_ln",
    )(u, u, jnp.pad(w_dw, ((0, HALO - CONV_WIDTH), (0, 0))), b_dw.reshape(1, d), ln_g.reshape(1, d), ln_b.reshape(1, d))


def _moe_kernel(h_ref, cw_ref, wg_ref, wu_ref, wd_ref, x_ref, g_ref, o_ref, acc_ref):
    e = pl.program_id(1)

    @pl.when(e == 0)
    def _():
        acc_ref[...] = jnp.zeros_like(acc_ref)

    h = h_ref[...]
    cw = cw_ref[...]
    lane = lax.broadcasted_iota(I32, cw.shape, 1)
    w_e = jnp.sum(jnp.where(lane == e, cw, 0.0), axis=1, keepdims=True)
    gate = jnp.dot(h, wg_ref[0], preferred_element_type=F32)
    up = jnp.dot(h, wu_ref[0], preferred_element_type=F32)
    hid = (gate * jax.nn.sigmoid(gate)) * up
    y = jnp.dot(hid.astype(BF16), wd_ref[0], preferred_element_type=F32)
    acc_ref[...] += y * w_e

    @pl.when(e == pl.num_programs(1) - 1)
    def _():
        o_ref[...] = x_ref[...] + g_ref[...] * acc_ref[...]


def _moe(h, cw, w_gate, w_up, w_down, x, gate, tm=512):
    t, d = h.shape
    ne, _, de = w_gate.shape
    return pl.pallas_call(
        _moe_kernel,
        grid=(t // tm, ne),
        in_specs=[pl.BlockSpec((tm, d), lambda i, e: (i, 0)),
                  pl.BlockSpec((tm, LANES), lambda i, e: (i, 0)),
                  pl.BlockSpec((1, d, de), lambda i, e: (e, 0, 0)),
                  pl.BlockSpec((1, d, de), lambda i, e: (e, 0, 0)),
                  pl.BlockSpec((1, de, d), lambda i, e: (e, 0, 0)),
                  pl.BlockSpec((tm, d), lambda i, e: (i, 0)),
                  pl.BlockSpec((1, d), lambda i, e: (0, 0))],
        out_specs=pl.BlockSpec((tm, d), lambda i, e: (i, 0)),
        out_shape=jax.ShapeDtypeStruct((t, d), F32),
        scratch_shapes=[pltpu.VMEM((tm, d), F32)],
        compiler_params=_cparams(("arbitrary", "arbitrary")),
        name="moe",
    )(h, cw, w_gate, w_up, w_down, x, gate.reshape(1, d))


def _rms_kernel(x_ref, g_ref, o_ref):
    x = x_ref[...]
    o_ref[...] = (x * lax.rsqrt(jnp.mean(x * x, axis=-1, keepdims=True) + EPS)) * g_ref[...]


def _final_norm(x, g, tm=512):
    t, d = x.shape
    return pl.pallas_call(
        _rms_kernel,
        grid=(t // tm,),
        in_specs=[pl.BlockSpec((tm, d), lambda i: (i, 0)), pl.BlockSpec((1, d), lambda i: (0, 0))],
        out_specs=pl.BlockSpec((tm, d), lambda i: (i, 0)),
        out_shape=jax.ShapeDtypeStruct((t, d), F32),
        compiler_params=_cparams(("arbitrary",)),
        name="final_norm",
    )(x, g.reshape(1, d))


def _rope_tables(t):
    pos = jnp.arange(t, dtype=I32).astype(F32)[:, None]
    lane = jnp.arange(LANES)

    def table(dim):
        half = dim // 2
        inv = ROPE_THETA ** (-jnp.arange(half, dtype=F32) / half)
        ang = pos * inv[None, :]
        j = lane % dim
        cos = jnp.cos(ang)[:, j % half]
        sin = jnp.sin(ang)[:, j % half] * jnp.where(j < half, -1.0, 1.0)[None, :]
        return cos, sin

    return table(HEAD_DIM), table(IDX_DIM)


def _attention_layer(x, mod, norm_g, w_in, w_out, tables):
    t, d = x.shape
    sh1, sc1, g1 = mod[0:d], mod[d:2 * d], mod[2 * d:3 * d]
    (cos_h, sin_h), (cos_i, sin_i) = tables
    hn = _normmod(x, norm_g, sc1, sh1)
    w_bf = w_in.astype(BF16)
    nq = N_HEADS * HEAD_DIM
    nkv = N_KV_GROUPS * HEAD_DIM
    nqi = IDX_HEADS * IDX_DIM
    tn = 512
    q = _proj_rope(hn, w_bf, 0, nq, cos_h, sin_h, HEAD_DIM // 2, BF16)
    kv = _proj_kv(hn, w_bf, nq // nkv, nq // nkv + 1, cos_h, sin_h)
    qi = _proj_rope(hn, w_bf, (nq + 2 * nkv) // tn, nqi, cos_i, sin_i, IDX_DIM // 2, BF16)
    tail0 = nq + 2 * nkv + nqi
    w_tail = jnp.pad(w_bf[:, tail0:], ((0, 0), (0, LANES - (IDX_DIM + IDX_HEADS))))
    lane = jnp.arange(LANES)
    wi_scale = (IDX_HEADS ** -0.5) * (IDX_DIM ** -0.5)
    cos_t = jnp.where(lane[None, :] < IDX_DIM, cos_i, wi_scale)
    sin_t = jnp.where(lane[None, :] < IDX_DIM, sin_i, 0.0)
    tail = _proj_rope(hn, w_tail, 0, LANES, cos_t, sin_t, IDX_DIM // 2, F32, tn=LANES)
    ki = tail[:, :IDX_DIM].astype(BF16)
    wit = tail[:, IDX_DIM:IDX_DIM + IDX_HEADS].T
    n_keep = min(TOPK_KEYS, t // 4)
    idx_t, nv = _indexer(qi.T, ki, wit, n_keep)
    att = _gattn(nv.reshape(t), idx_t.T.reshape(t * n_keep), q.reshape(t * N_HEADS, HEAD_DIM), kv.reshape(t * KV_ROWS, LANES), n_keep)
    att = att.reshape(t, nq)
    return _mm_res(att, w_out.astype(BF16), jnp.zeros((d,), F32), x, g1)


def _conv_layer(x, mod, norm_g, w_pw1, b_pw1, w_dw, b_dw, ln_g, ln_b, w_pw2, b_pw2):
    t, d = x.shape
    sh1, sc1, g1 = mod[0:d], mod[d:2 * d], mod[2 * d:3 * d]
    hn = _normmod(x, norm_g, sc1, sh1)
    u = _glu(hn, w_pw1.astype(BF16), b_pw1)
    v = _dwconv_ln(u, w_dw, b_dw, ln_g, ln_b)
    return _mm_res(v, w_pw2.astype(BF16), b_pw2, x, g1)


def _moe_layer(x, mod, norm_g, w_rg, b_rg, w_re, b_re, w_gate, w_up, w_down):
    t, d = x.shape
    sh2, sc2, g2 = mod[3 * d:4 * d], mod[4 * d:5 * d], mod[5 * d:6 * d]
    h, cw = _route(x, norm_g, sc2, sh2, w_rg, b_rg, w_re, b_re)
    return _moe(h, cw, w_gate.astype(BF16), w_up.astype(BF16), w_down.astype(BF16), x, g2)


def kernel(x, c, ada_w, ada_b, norm1_g, norm2_g, attn_w_in, attn_w_out, conv_w_pw1, conv_b_pw1, conv_w_dw, conv_b_dw, conv_ln_g, conv_ln_b, conv_w_pw2, conv_b_pw2, router_w_group, router_b_group, router_w_expert, router_b_expert, exp_w_gate, exp_w_up, exp_w_down, final_g):
    bsz, t, d = x.shape
    assert bsz == 1 and d == N_HEADS * HEAD_DIM
    depth = ada_w.shape[0]
    mods = _ada_mod(c, ada_w, ada_b)
    tables = _rope_tables(t)
    xf = x.reshape(t, d)
    for i in range(depth):
        j = i // 2
        if i % 2 == 0:
            xf = _attention_layer(xf, mods[i], norm1_g[i], attn_w_in[j], attn_w_out[j], tables)
        else:
            xf = _conv_layer(xf, mods[i], norm1_g[i], conv_w_pw1[j], conv_b_pw1[j], conv_w_dw[j], conv_b_dw[j],
                             conv_ln_g[j], conv_ln_b[j], conv_w_pw2[j], conv_b_pw2[j])
        xf = _moe_layer(xf, mods[i], norm2_g[i], router_w_group[i], router_b_group[i], router_w_expert[i],
                        router_b_expert[i], exp_w_gate[i], exp_w_up[i], exp_w_down[i])
    return _final_norm(xf, final_g).reshape(bsz, t, d)
```

```python
import functools

import jax
import jax.numpy as jnp
from jax import lax
from jax.experimental import pallas as pl
from jax.experimental.pallas import tpu as pltpu

EPS = 1e-6
CHUNK = 64
N_HEADS = 16
N_KV_GROUPS = 4
HEADS_PER_GROUP = N_HEADS // N_KV_GROUPS
HEAD_DIM = 128
IDX_HEADS = 16
IDX_DIM = 64
TOPK_KEYS = 256
ROPE_THETA = 10000.0
NEG_INF = -1e30
CONV_WIDTH = 31
N_GROUPS = 4
EXPERTS_PER_GROUP = 4
N_EXPERTS = N_GROUPS * EXPERTS_PER_GROUP

LANES = 128
KV_ROWS = N_KV_GROUPS
INT_MIN = -(2 ** 31)
INVALID_SHIFT = 1 << 30
VMEM_LIMIT = 56 * 1024 * 1024

F32 = jnp.float32
BF16 = jnp.bfloat16
I32 = jnp.int32


def _cparams(sem):
    return pltpu.CompilerParams(dimension_semantics=sem, vmem_limit_bytes=VMEM_LIMIT)


def _ada_kernel(c_ref, w_ref, b_ref, o_ref):
    c = c_ref[...]
    ca = c * jax.nn.sigmoid(c)
    o_ref[0] = jnp.dot(ca, w_ref[0], preferred_element_type=F32, precision=lax.Precision.HIGHEST) + b_ref[0]


def _ada_mod(c, ada_w, ada_b):
    depth, d, n = ada_w.shape
    tn = 768
    c8 = jnp.zeros((8, d), F32).at[0].set(c[0])
    out = pl.pallas_call(
        _ada_kernel,
        grid=(depth, n // tn),
        in_specs=[pl.BlockSpec((8, d), lambda i, j: (0, 0)),
                  pl.BlockSpec((1, d, tn), lambda i, j: (i, 0, j)),
                  pl.BlockSpec((1, 1, tn), lambda i, j: (i, 0, j))],
        out_specs=pl.BlockSpec((1, 8, tn), lambda i, j: (i, 0, j)),
        out_shape=jax.ShapeDtypeStruct((depth, 8, n), F32),
        compiler_params=_cparams(("arbitrary", "arbitrary")),
        name="ada_mod",
    )(c8, ada_w, ada_b.reshape(depth, 1, n))
    return out[:, 0, :]


def _normmod_kernel(x_ref, g_ref, sc_ref, sh_ref, o_ref):
    x = x_ref[...]
    xn = x * lax.rsqrt(jnp.mean(x * x, axis=-1, keepdims=True) + EPS)
    o_ref[...] = ((xn * g_ref[...]) * (1.0 + sc_ref[...]) + sh_ref[...]).astype(o_ref.dtype)


def _normmod(x, g, sc, sh, tm=512):
    t, d = x.shape
    vec = pl.BlockSpec((1, d), lambda i: (0, 0))
    return pl.pallas_call(
        _normmod_kernel,
        grid=(t // tm,),
        in_specs=[pl.BlockSpec((tm, d), lambda i: (i, 0)), vec, vec, vec],
        out_specs=pl.BlockSpec((tm, d), lambda i: (i, 0)),
        out_shape=jax.ShapeDtypeStruct((t, d), BF16),
        compiler_params=_cparams(("arbitrary",)),
        name="normmod",
    )(x, g.reshape(1, d), sc.reshape(1, d), sh.reshape(1, d))


N_PAIRS = EXPERTS_PER_GROUP * (EXPERTS_PER_GROUP - 1) // 2
N_BUCKETS = N_GROUPS * N_PAIRS
PAIR_LO = (0, 0, 0, 1, 1, 2)
PAIR_HI = (1, 2, 3, 2, 3, 3)
MOE_TM = 256
AUX_W_LO, AUX_W_HI, AUX_BUCKET, AUX_RANK = 0, 1, 2, 3


def _route_kernel(x_ref, g_ref, sc_ref, sh_ref, wr_ref, br_ref, tri_ref, h_ref, aux_ref, cnt_ref, run_ref):
    @pl.when(pl.program_id(0) == 0)
    def _():
        run_ref[...] = jnp.zeros_like(run_ref)

    x = x_ref[...]
    xn = x * lax.rsqrt(jnp.mean(x * x, axis=-1, keepdims=True) + EPS)
    h = (xn * g_ref[...]) * (1.0 + sc_ref[...]) + sh_ref[...]
    h_ref[...] = h.astype(h_ref.dtype)
    logits = jnp.dot(h, wr_ref[...], preferred_element_type=F32, precision=lax.Precision.HIGHEST) + br_ref[...]
    tm = x.shape[0]
    lane4 = lax.broadcasted_iota(I32, (tm, N_GROUPS), 1)

    def first_argmax(v):
        m = jnp.max(v, axis=-1, keepdims=True)
        return m, jnp.min(jnp.where(v == m, lane4, N_GROUPS), axis=-1, keepdims=True)

    gl = logits[:, 0:N_GROUPS]
    gmax, gidx = first_argmax(gl)
    g_w = 1.0 / jnp.sum(jnp.exp(gl - gmax), axis=-1, keepdims=True)
    el = jnp.zeros((tm, EXPERTS_PER_GROUP), F32)
    for g in range(N_GROUPS):
        lo = N_GROUPS + g * EXPERTS_PER_GROUP
        el = jnp.where(gidx == g, logits[:, lo:lo + EXPERTS_PER_GROUP], el)
    emax = jnp.max(el, axis=-1, keepdims=True)
    ee = jnp.exp(el - emax)
    ep = ee / jnp.sum(ee, axis=-1, keepdims=True)
    p1, i1 = first_argmax(ep)
    rest = jnp.where(lane4 == i1, -1.0, ep)
    p2, i2 = first_argmax(rest)
    den = p1 + p2
    w1 = g_w * p1 / den
    w2 = g_w * p2 / den
    first_lo = i1 < i2
    a = jnp.minimum(i1, i2)
    b = jnp.maximum(i1, i2)
    w_lo = jnp.where(first_lo, w1, w2)
    w_hi = jnp.where(first_lo, w2, w1)
    pair = jnp.where(a == 0, b - 1, jnp.where(a == 1, b + 1, N_PAIRS - 1))
    bucket = gidx * N_PAIRS + pair
    lane = lax.broadcasted_iota(I32, (tm, LANES), 1)
    onehot = lane == bucket
    prefix = jnp.dot(tri_ref[...], onehot.astype(F32).astype(BF16), preferred_element_type=F32)
    run = run_ref[...]
    rank = jnp.sum(jnp.where(onehot, prefix + run, 0.0), axis=1, keepdims=True) - 1.0
    run = run + prefix[tm - 1:tm, :]
    run_ref[...] = run
    cnt_ref[...] = run
    aux_ref[...] = (jnp.where(lane == AUX_W_LO, w_lo, 0.0) + jnp.where(lane == AUX_W_HI, w_hi, 0.0)
                    + jnp.where(lane == AUX_BUCKET, bucket.astype(F32), 0.0) + jnp.where(lane == AUX_RANK, rank, 0.0))


def _route(x, g, sc, sh, w_rg, b_rg, w_re, b_re, tm=512):
    t, d = x.shape
    w_all = jnp.concatenate([w_rg, jnp.moveaxis(w_re, 0, 1).reshape(d, N_EXPERTS)], axis=1)
    b_all = jnp.concatenate([b_rg, b_re.reshape(N_EXPERTS)])
    npad = LANES - w_all.shape[1]
    w_all = jnp.pad(w_all, ((0, 0), (0, npad)))
    b_all = jnp.pad(b_all, (0, npad)).reshape(1, LANES)
    vec = pl.BlockSpec((1, d), lambda i: (0, 0))
    r = lax.broadcasted_iota(I32, (tm, tm), 0)
    c = lax.broadcasted_iota(I32, (tm, tm), 1)
    tri = (c <= r).astype(BF16)
    return pl.pallas_call(
        _route_kernel,
        grid=(t // tm,),
        in_specs=[pl.BlockSpec((tm, d), lambda i: (i, 0)), vec, vec, vec,
                  pl.BlockSpec((d, LANES), lambda i: (0, 0)), pl.BlockSpec((1, LANES), lambda i: (0, 0)),
                  pl.BlockSpec((tm, tm), lambda i: (0, 0))],
        out_specs=[pl.BlockSpec((tm, d), lambda i: (i, 0)), pl.BlockSpec((tm, LANES), lambda i: (i, 0)),
                   pl.BlockSpec((1, LANES), lambda i: (0, 0))],
        out_shape=[jax.ShapeDtypeStruct((t, d), F32), jax.ShapeDtypeStruct((t, LANES), F32),
                   jax.ShapeDtypeStruct((1, LANES), F32)],
        scratch_shapes=[pltpu.VMEM((1, LANES), F32)],
        compiler_params=_cparams(("arbitrary",)),
        name="route",
    )(x, g.reshape(1, d), sc.reshape(1, d), sh.reshape(1, d), w_all, b_all, tri)


def _rope_block(blk, cos, sin, half):
    if half == HEAD_DIM // 2:
        partner = pltpu.roll(blk, HEAD_DIM // 2, 1)
    else:
        lane = lax.broadcasted_iota(I32, blk.shape, 1)
        first = (lane % IDX_DIM) < (IDX_DIM // 2)
        partner = jnp.where(first, pltpu.roll(blk, LANES - IDX_DIM // 2, 1), pltpu.roll(blk, IDX_DIM // 2, 1))
    return blk * cos + partner * sin


def _proj_rope_kernel(a_ref, w_ref, cos_ref, sin_ref, o_ref, *, half):
    acc = jnp.dot(a_ref[...], w_ref[...], preferred_element_type=F32)
    cos = cos_ref[...]
    sin = sin_ref[...]
    for h in range(acc.shape[1] // LANES):
        sl = slice(h * LANES, (h + 1) * LANES)
        o_ref[:, sl] = _rope_block(acc[:, sl], cos, sin, half).astype(o_ref.dtype)


def _proj_rope(a, w, col_block0, n_out, cos, sin, half, out_dtype, tm=512, tn=512):
    t, d = a.shape
    return pl.pallas_call(
        functools.partial(_proj_rope_kernel, half=half),
        grid=(t // tm, n_out // tn),
        in_specs=[pl.BlockSpec((tm, d), lambda i, j: (i, 0)),
                  pl.BlockSpec((d, tn), lambda i, j: (0, col_block0 + j)),
                  pl.BlockSpec((tm, LANES), lambda i, j: (i, 0)),
                  pl.BlockSpec((tm, LANES), lambda i, j: (i, 0))],
        out_specs=pl.BlockSpec((tm, tn), lambda i, j: (i, j)),
        out_shape=jax.ShapeDtypeStruct((t, n_out), out_dtype),
        compiler_params=_cparams(("arbitrary", "arbitrary")),
        name="proj_rope",
    )(a, w, cos, sin)


def _proj_kv_kernel(a_ref, wk_ref, wv_ref, cos_ref, sin_ref, o_ref):
    a = a_ref[...]
    k = jnp.dot(a, wk_ref[...], preferred_element_type=F32)
    v = jnp.dot(a, wv_ref[...], preferred_element_type=F32)
    cos = cos_ref[...]
    sin = sin_ref[...]
    for h in range(k.shape[1] // LANES):
        sl = slice(h * LANES, (h + 1) * LANES)
        kr = _rope_block(k[:, sl], cos, sin, HEAD_DIM // 2)
        kb = lax.bitcast_convert_type(kr.astype(BF16).astype(F32), I32)
        vb = lax.bitcast_convert_type(v[:, sl].astype(BF16).astype(F32), I32)
        o_ref[:, sl] = kb | lax.shift_right_logical(vb, 16)


def _proj_kv(a, w, kblock, vblock, cos, sin, tm=512):
    t, d = a.shape
    n = N_KV_GROUPS * HEAD_DIM
    return pl.pallas_call(
        _proj_kv_kernel,
        grid=(t // tm,),
        in_specs=[pl.BlockSpec((tm, d), lambda i: (i, 0)),
                  pl.BlockSpec((d, n), lambda i: (0, kblock)),
                  pl.BlockSpec((d, n), lambda i: (0, vblock)),
                  pl.BlockSpec((tm, LANES), lambda i: (i, 0)),
                  pl.BlockSpec((tm, LANES), lambda i: (i, 0))],
        out_specs=pl.BlockSpec((tm, n), lambda i: (i, 0)),
        out_shape=jax.ShapeDtypeStruct((t, n), I32),
        compiler_params=_cparams(("arbitrary",)),
        name="proj_kv",
    )(a, w, w, cos, sin)


QB = 256
TK = 512
RC = 64


def _indexer_kernel(qit_ref, ki_ref, wit_ref, tri_ref, idx_ref, nv_ref, d_ref, *, n_keep):
    b = pl.program_id(0)
    q0 = b * QB
    n_t = (q0 + QB + TK - 1) // TK
    n_rows = n_t * TK
    rowi = lax.broadcasted_iota(I32, (TK, QB), 0)
    qchunk = (q0 + lax.broadcasted_iota(I32, (TK, QB), 1)) // CHUNK

    def admissible(k0):
        return ((k0 + rowi) // CHUNK) <= qchunk

    def score_tile(kt, carry):
        k0 = pl.multiple_of(kt * TK, TK)
        ki_t = ki_ref[pl.ds(k0, TK), :]
        acc = jnp.zeros((TK, QB), F32)
        for h in range(IDX_HEADS):
            a = jnp.dot(ki_t, qit_ref[h * IDX_DIM:(h + 1) * IDX_DIM, :], preferred_element_type=F32)
            acc = acc + jnp.maximum(a, 0.0) * wit_ref[h:h + 1, :]
        s = jnp.where(admissible(k0), acc + 0.0, NEG_INF)
        u = lax.bitcast_convert_type(s, I32)
        d_ref[pl.ds(k0, TK), :] = jnp.where(u < 0, u ^ 0x7FFFFFFF, u)
        return carry

    lax.fori_loop(0, n_t, score_tile, 0)

    def count_where(pred):
        def body(kt, acc):
            k0 = pl.multiple_of(kt * TK, TK)
            hit = pred(d_ref[pl.ds(k0, TK), :]).astype(I32)
            return acc + jnp.sum(hit.reshape(TK // 8, 8, QB), axis=0)
        acc = lax.fori_loop(0, n_t, body, jnp.zeros((8, QB), I32))
        return jnp.sum(acc, axis=0, keepdims=True)

    def select_bit(it, v):
        bit = jnp.left_shift(jnp.int32(1), 31 - it)
        cand = jnp.where(it == 0, jnp.zeros_like(v), v | bit)
        cnt = count_where(lambda key: key >= cand)
        return jnp.where(cnt >= n_keep, cand, v)

    thr = lax.fori_loop(0, 32, select_bit, jnp.full((1, QB), INT_MIN, I32))
    need = (n_keep - count_where(lambda key: key > thr)).astype(F32)

    tri = tri_ref[...]

    def rank_tile(kt, carry):
        run_eq, run_sel = carry
        k0 = pl.multiple_of(kt * TK, TK)
        key = d_ref[pl.ds(k0, TK), :]
        eq = key == thr
        eq_f = eq.astype(F32)
        pre_eq = jnp.dot(tri, eq_f.astype(BF16), preferred_element_type=F32)
        sel = admissible(k0) & ((key > thr) | (eq & ((run_eq + pre_eq - eq_f) < need)))
        pre_sel = jnp.dot(tri, sel.astype(F32).astype(BF16), preferred_element_type=F32)
        rank = (run_sel + pre_sel).astype(I32) - 1
        d_ref[pl.ds(k0, TK), :] = jnp.where(sel, (k0 + rowi) - rank, INVALID_SHIFT)
        return run_eq + pre_eq[TK - 1:TK, :], run_sel + pre_sel[TK - 1:TK, :]

    zero = jnp.zeros((1, QB), F32)
    _, n_sel = lax.fori_loop(0, n_t, rank_tile, (zero, zero))

    d_ref[pl.ds(pl.multiple_of(n_rows, RC), RC), :] = jnp.full((RC, QB), INVALID_SHIFT, I32)
    n_chunks = n_rows // RC
    win_chunks = -(-n_keep // RC)
    window = win_chunks * RC
    n_stages = (d_ref.shape[0] - RC - 1).bit_length()
    for s in range(n_stages):
        sh = 1 << s
        test = sh | INVALID_SHIFT

        def merge(own, inc, sh=sh, test=test):
            take = (inc & test) == sh
            stay = (own & test) == 0
            return jnp.where(take, inc, jnp.where(stay, own, INVALID_SHIFT))

        def chunk_with_source(c, carry, sh=sh, merge=merge):
            r0 = pl.multiple_of(c * RC, RC)
            if sh < 8:
                x = d_ref[pl.ds(r0, RC + 8), :]
                own, inc = x[0:RC], x[sh:sh + RC]
            else:
                own = d_ref[pl.ds(r0, RC), :]
                inc = d_ref[pl.ds(pl.multiple_of(r0 + sh, 8), RC), :]
            d_ref[pl.ds(r0, RC), :] = merge(own, inc)
            return carry

        def chunk_without_source(c, carry, merge=merge):
            r0 = pl.multiple_of(c * RC, RC)
            own = d_ref[pl.ds(r0, RC), :]
            d_ref[pl.ds(r0, RC), :] = merge(own, jnp.full_like(own, INVALID_SHIFT))
            return carry

        def window_chunk(c, carry, sh=sh, merge=merge):
            r0 = pl.multiple_of((c // win_chunks) * sh + (c % win_chunks) * RC, RC)
            src = r0 + sh
            own = d_ref[pl.ds(r0, RC), :]
            inc = d_ref[pl.ds(pl.multiple_of(jnp.minimum(src, n_rows), RC), RC), :]
            d_ref[pl.ds(r0, RC), :] = merge(own, inc)
            return carry

        if sh < RC:
            lax.fori_loop(0, n_chunks, chunk_with_source, 0)
        elif sh < 2 * window:
            n_src = jnp.maximum(n_rows - sh, 0) // RC
            lax.fori_loop(0, n_src, chunk_with_source, 0)
            lax.fori_loop(n_src, n_chunks, chunk_without_source, 0)
        else:
            lax.fori_loop(0, ((n_rows + sh - 1) // sh) * win_chunks, window_chunk, 0)

    slot = lax.broadcasted_iota(I32, (n_keep, QB), 0)
    dd = d_ref[0:n_keep, :]
    idx_ref[...] = jnp.where(dd < INVALID_SHIFT, (slot + dd) * KV_ROWS, 0)
    nv_ref[...] = n_sel.astype(I32)


def _indexer(qit, ki, wit, n_keep):
    t = ki.shape[0]
    r = lax.broadcasted_iota(I32, (TK, TK), 0)
    c = lax.broadcasted_iota(I32, (TK, TK), 1)
    tri = (c <= r).astype(BF16)
    return pl.pallas_call(
        functools.partial(_indexer_kernel, n_keep=n_keep),
        grid=(t // QB,),
        in_specs=[pl.BlockSpec((IDX_HEADS * IDX_DIM, QB), lambda b: (0, b)),
                  pl.BlockSpec((t, IDX_DIM), lambda b: (0, 0)),
                  pl.BlockSpec((IDX_HEADS, QB), lambda b: (0, b)),
                  pl.BlockSpec((TK, TK), lambda b: (0, 0))],
        out_specs=[pl.BlockSpec((n_keep, QB), lambda b: (0, b)), pl.BlockSpec((1, QB), lambda b: (0, b))],
        out_shape=[jax.ShapeDtypeStruct((n_keep, t), I32), jax.ShapeDtypeStruct((1, t), I32)],
        scratch_shapes=[pltpu.VMEM((t + RC, QB), I32)],
        compiler_params=_cparams(("arbitrary",)),
        name="indexer",
    )(qit, ki, wit, tri)


GQ = 64


def _gattn_kernel(nv_ref, idx_hbm, q_ref, kv_hbm, o_ref, kv_vmem, idx_smem, stage_a, stage_b, stage_c, stage_d,
                  s_a, s_b, sem_kv, sem_idx, *, n_keep):
    i = pl.program_id(0)

    @pl.when(i == 0)
    def _():
        cp = pltpu.make_async_copy(kv_hbm, kv_vmem, sem_kv)
        cp.start()
        cp.wait()

    cp = pltpu.make_async_copy(idx_hbm.at[pl.ds(i * (GQ * n_keep), GQ * n_keep)], idx_smem, sem_idx)
    cp.start()
    cp.wait()

    n_col = n_keep * KV_ROWS
    colg = lax.broadcasted_iota(I32, (N_HEADS, n_col), 1)
    rowh = lax.broadcasted_iota(I32, (N_HEADS, n_col), 0)
    own_group = (colg % KV_ROWS) == (rowh // HEADS_PER_GROUP)
    col_slot = colg // KV_ROWS
    scale = HEAD_DIM ** -0.5

    def gather(r, stage):
        base = r * n_keep
        for j in range(n_keep):
            row = pl.multiple_of(idx_smem[base + j], KV_ROWS)
            stage[j * KV_ROWS:(j + 1) * KV_ROWS, :] = kv_vmem[pl.ds(row, KV_ROWS), :]

    def scores(r, stage, s_buf):
        k_all = lax.bitcast_convert_type(stage[...] & jnp.int32(-65536), F32).astype(BF16)
        qh = q_ref[pl.ds(pl.multiple_of(r * N_HEADS, N_HEADS), N_HEADS), :]
        s = lax.dot_general(qh, k_all, (((1,), (1,)), ((), ())), preferred_element_type=F32) * scale
        mask = own_group & (col_slot < nv_ref[i * GQ + r])
        s_buf[...] = jnp.where(mask, s, NEG_INF)

    def combine(r, stage, s_buf):
        s = s_buf[...]
        p = jnp.exp(s - jnp.max(s, axis=1, keepdims=True))
        l = jnp.sum(p, axis=1, keepdims=True)
        v_all = lax.bitcast_convert_type(jnp.left_shift(stage[...], 16), F32).astype(BF16)
        o = jnp.dot(p.astype(BF16), v_all, preferred_element_type=F32) / l
        o_ref[pl.ds(pl.multiple_of(r * N_HEADS, N_HEADS), N_HEADS), :] = o.astype(o_ref.dtype)

    stages = (stage_a, stage_b, stage_c, stage_d)
    s_bufs = (s_a, s_b)
    last = GQ - 1
    gather(0, stages[0])
    gather(1, stages[1])
    scores(0, stages[0], s_bufs[0])

    def query_quad(it, carry):
        r = 4 * it
        for u in range(4):
            gather(jnp.minimum(r + u + 2, last), stages[(u + 2) % 4])
            scores(jnp.minimum(r + u + 1, last), stages[(u + 1) % 4], s_bufs[(u + 1) % 2])
            combine(r + u, stages[u], s_bufs[u % 2])
        return carry

    lax.fori_loop(0, GQ // 4, query_quad, 0)


def _gattn(nv, idx, q2, kv2, n_keep):
    t = nv.shape[0]
    grid_spec = pltpu.PrefetchScalarGridSpec(
        num_scalar_prefetch=1,
        grid=(t // GQ,),
        in_specs=[pl.BlockSpec(memory_space=pl.ANY),
                  pl.BlockSpec((GQ * N_HEADS, HEAD_DIM), lambda i, nv: (i, 0)),
                  pl.BlockSpec(memory_space=pl.ANY)],
        out_specs=pl.BlockSpec((GQ * N_HEADS, HEAD_DIM), lambda i, nv: (i, 0)),
        scratch_shapes=[pltpu.VMEM(kv2.shape, I32),
                        pltpu.SMEM((GQ * n_keep,), I32),
                        pltpu.VMEM((n_keep * KV_ROWS, LANES), I32),
                        pltpu.VMEM((n_keep * KV_ROWS, LANES), I32),
                        pltpu.VMEM((n_keep * KV_ROWS, LANES), I32),
                        pltpu.VMEM((n_keep * KV_ROWS, LANES), I32),
                        pltpu.VMEM((N_HEADS, n_keep * KV_ROWS), F32),
                        pltpu.VMEM((N_HEADS, n_keep * KV_ROWS), F32),
                        pltpu.SemaphoreType.DMA,
                        pltpu.SemaphoreType.DMA],
    )
    return pl.pallas_call(
        functools.partial(_gattn_kernel, n_keep=n_keep),
        grid_spec=grid_spec,
        out_shape=jax.ShapeDtypeStruct((t * N_HEADS, HEAD_DIM), BF16),
        compiler_params=_cparams(("arbitrary",)),
        name="gattn",
    )(nv, idx, q2, kv2)


def _mm_res_kernel(a_ref, w_ref, b_ref, x_ref, g_ref, o_ref):
    y = jnp.dot(a_ref[...], w_ref[...], preferred_element_type=F32) + b_ref[...]
    o_ref[...] = x_ref[...] + g_ref[...] * y


def _mm_res(a, w, bias, x, gate, tm=512, tn=512):
    t, k = a.shape
    n = w.shape[1]
    return pl.pallas_call(
        _mm_res_kernel,
        grid=(t // tm, n // tn),
        in_specs=[pl.BlockSpec((tm, k), lambda i, j: (i, 0)),
                  pl.BlockSpec((k, tn), lambda i, j: (0, j)),
                  pl.BlockSpec((1, tn), lambda i, j: (0, j)),
                  pl.BlockSpec((tm, tn), lambda i, j: (i, j)),
                  pl.BlockSpec((1, tn), lambda i, j: (0, j))],
        out_specs=pl.BlockSpec((tm, tn), lambda i, j: (i, j)),
        out_shape=jax.ShapeDtypeStruct((t, n), F32),
        compiler_params=_cparams(("arbitrary", "arbitrary")),
        name="mm_res",
    )(a, w, bias.reshape(1, n), x, gate.reshape(1, n))


def _glu_kernel(a_ref, wa_ref, wg_ref, ba_ref, bg_ref, o_ref):
    a = a_ref[...]
    ya = jnp.dot(a, wa_ref[...], preferred_element_type=F32) + ba_ref[...]
    yg = jnp.dot(a, wg_ref[...], preferred_element_type=F32) + bg_ref[...]
    o_ref[...] = ya * jax.nn.sigmoid(yg)


def _glu(a, w, b, tm=512, tn=512):
    t, k = a.shape
    n = w.shape[1] // 2
    nb = n // tn
    b2 = b.reshape(1, 2 * n)
    return pl.pallas_call(
        _glu_kernel,
        grid=(t // tm, nb),
        in_specs=[pl.BlockSpec((tm, k), lambda i, j: (i, 0)),
                  pl.BlockSpec((k, tn), lambda i, j: (0, j)),
                  pl.BlockSpec((k, tn), lambda i, j: (0, j + nb)),
                  pl.BlockSpec((1, tn), lambda i, j: (0, j)),
                  pl.BlockSpec((1, tn), lambda i, j: (0, j + nb))],
        out_specs=pl.BlockSpec((tm, tn), lambda i, j: (i, j)),
        out_shape=jax.ShapeDtypeStruct((t, n), F32),
        compiler_params=_cparams(("arbitrary", "arbitrary")),
        name="glu",
    )(a, w, w, b2, b2)


HALO = 32


SUBLANES = 8
CONV_ROWS = 32
CONV_COLS = 512


def _dwconv_ln_kernel(u_ref, prev_ref, w_ref, b_ref, g_ref, beta_ref, o_ref, ext_ref, sh_ref, conv_ref):
    i = pl.program_id(0)
    tm, d = u_ref.shape
    ext_ref[0:HALO, :] = jnp.where(i == 0, 0.0, prev_ref[...])
    ext_ref[HALO:HALO + tm, :] = u_ref[...]
    for phase in range(SUBLANES):
        sh_ref[phase, 0:tm + HALO - phase, :] = ext_ref[phase:tm + HALO, :]
    first = HALO - (CONV_WIDTH - 1)

    def conv_rows(c, carry):
        r0 = pl.multiple_of(c * CONV_ROWS, CONV_ROWS)
        for cb in range(d // CONV_COLS):
            cols = slice(cb * CONV_COLS, (cb + 1) * CONV_COLS)
            acc = jnp.zeros((CONV_ROWS, CONV_COLS), F32) + b_ref[:, cols]
            for k in range(CONV_WIDTH):
                phase = (first + k) % SUBLANES
                start = pl.multiple_of(r0 + (first + k - phase), SUBLANES)
                acc = acc + sh_ref[phase, pl.ds(start, CONV_ROWS), cols] * w_ref[k:k + 1, cols]
            conv_ref[pl.ds(r0, CONV_ROWS), cols] = acc
        return carry

    lax.fori_loop(0, tm // CONV_ROWS, conv_rows, 0)
    acc = conv_ref[...]
    mu = jnp.mean(acc, axis=-1, keepdims=True)
    cen = acc - mu
    var = jnp.mean(cen * cen, axis=-1, keepdims=True)
    y = (cen * lax.rsqrt(var + EPS)) * g_ref[...] + beta_ref[...]
    o_ref[...] = (y * jax.nn.sigmoid(y)).astype(o_ref.dtype)


def _dwconv_ln(u, w_dw, b_dw, ln_g, ln_b, tm=256):
    t, d = u.shape
    ratio = tm // HALO
    vec = pl.BlockSpec((1, d), lambda i: (0, 0))
    return pl.pallas_call(
        _dwconv_ln_kernel,
        grid=(t // tm,),
        in_specs=[pl.BlockSpec((tm, d), lambda i: (i, 0)),
                  pl.BlockSpec((HALO, d), lambda i: (jnp.maximum(i * ratio - 1, 0), 0)),
                  pl.BlockSpec((HALO, d), lambda i: (0, 0)),
                  vec, vec, vec],
        out_specs=pl.BlockSpec((tm, d), lambda i: (i, 0)),
        out_shape=jax.ShapeDtypeStruct((t, d), BF16),
        scratch_shapes=[pltpu.VMEM((HALO + tm, d), F32), pltpu.VMEM((SUBLANES, HALO + tm, d), F32),
                        pltpu.VMEM((tm, d), F32)],
        compiler_params=_cparams(("arbitrary",)),
        name="dwconv_ln",
    )(u, u, jnp.pad(w_dw, ((0, HALO - CONV_WIDTH), (0, 0))), b_dw.reshape(1, d), ln_g.reshape(1, d), ln_b.reshape(1, d))


def _row_copy(src, src_row, dst, dst_row, sem):
    return pltpu.make_async_copy(src.at[pl.ds(src_row, 1), :], dst.at[pl.ds(dst_row, 1), :], sem)


def _dispatch_kernel(dest_ref, h_ref, aux_ref, xs_in, ws_in, xs_hbm, ws_hbm, sem_x, sem_w):
    del xs_in, ws_in
    tm = h_ref.shape[0]
    t0 = pl.program_id(0) * tm

    def issue(r, carry):
        slot = dest_ref[t0 + r]
        _row_copy(h_ref, r, xs_hbm, slot, sem_x).start()
        _row_copy(aux_ref, r, ws_hbm, slot, sem_w).start()
        return carry

    def drain(r, carry):
        _row_copy(h_ref, r, xs_hbm, 0, sem_x).wait()
        _row_copy(aux_ref, r, ws_hbm, 0, sem_w).wait()
        return carry

    lax.fori_loop(0, tm, issue, 0)
    lax.fori_loop(0, tm, drain, 0)


def _dispatch(dest, h, aux, n_rows, tm=256):
    t, d = h.shape
    grid_spec = pltpu.PrefetchScalarGridSpec(
        num_scalar_prefetch=1,
        grid=(t // tm,),
        in_specs=[pl.BlockSpec((tm, d), lambda i, dest: (i, 0)),
                  pl.BlockSpec((tm, LANES), lambda i, dest: (i, 0)),
                  pl.BlockSpec(memory_space=pl.ANY),
                  pl.BlockSpec(memory_space=pl.ANY)],
        out_specs=[pl.BlockSpec(memory_space=pl.ANY), pl.BlockSpec(memory_space=pl.ANY)],
        scratch_shapes=[pltpu.SemaphoreType.DMA, pltpu.SemaphoreType.DMA],
    )
    return pl.pallas_call(
        _dispatch_kernel,
        grid_spec=grid_spec,
        out_shape=[jax.ShapeDtypeStruct((n_rows, d), F32), jax.ShapeDtypeStruct((n_rows, LANES), F32)],
        input_output_aliases={3: 0, 4: 1},
        compiler_params=_cparams(("arbitrary",)),
        name="dispatch",
    )(dest, h, aux, jnp.zeros((n_rows, d), F32), jnp.zeros((n_rows, LANES), F32))


def _ffn_kernel(elo_ref, ehi_ref, live_ref, xs_ref, ws_ref, wg_lo, wu_lo, wd_lo, wg_hi, wu_hi, wd_hi, ys_ref):
    b = pl.program_id(0)

    @pl.when(live_ref[b] == 1)
    def _():
        x = xs_ref[...].astype(BF16)
        ws = ws_ref[...]

        def expert(wg, wu, wd):
            gate = jnp.dot(x, wg[0], preferred_element_type=F32)
            up = jnp.dot(x, wu[0], preferred_element_type=F32)
            hid = (gate * jax.nn.sigmoid(gate)) * up
            return jnp.dot(hid.astype(BF16), wd[0], preferred_element_type=F32)

        ys_ref[...] = (expert(wg_lo, wu_lo, wd_lo) * ws[:, AUX_W_LO:AUX_W_LO + 1]
                       + expert(wg_hi, wu_hi, wd_hi) * ws[:, AUX_W_HI:AUX_W_HI + 1])

    @pl.when(live_ref[b] == 0)
    def _():
        ys_ref[...] = jnp.zeros_like(ys_ref)


def _ffn(e_lo, e_hi, live, xs, ws, w_gate, w_up, w_down):
    n_rows, d = xs.shape
    _, _, de = w_gate.shape
    lo_in = pl.BlockSpec((1, d, de), lambda b, lo, hi, live: (lo[b], 0, 0))
    hi_in = pl.BlockSpec((1, d, de), lambda b, lo, hi, live: (hi[b], 0, 0))
    lo_out = pl.BlockSpec((1, de, d), lambda b, lo, hi, live: (lo[b], 0, 0))
    hi_out = pl.BlockSpec((1, de, d), lambda b, lo, hi, live: (hi[b], 0, 0))
    grid_spec = pltpu.PrefetchScalarGridSpec(
        num_scalar_prefetch=3,
        grid=(n_rows // MOE_TM,),
        in_specs=[pl.BlockSpec((MOE_TM, d), lambda b, lo, hi, live: (b, 0)),
                  pl.BlockSpec((MOE_TM, LANES), lambda b, lo, hi, live: (b, 0)),
                  lo_in, lo_in, lo_out, hi_in, hi_in, hi_out],
        out_specs=pl.BlockSpec((MOE_TM, d), lambda b, lo, hi, live: (b, 0)),
    )
    return pl.pallas_call(
        _ffn_kernel,
        grid_spec=grid_spec,
        out_shape=jax.ShapeDtypeStruct((n_rows, d), F32),
        compiler_params=_cparams(("arbitrary",)),
        name="ffn",
    )(e_lo, e_hi, live, xs, ws, w_gate, w_up, w_down, w_gate, w_up, w_down)


def _combine_kernel(dest_ref, x_ref, g_ref, fg_ref, ys_hbm, o_ref, buf, sem, *, final_norm):
    tm = x_ref.shape[0]
    t0 = pl.program_id(0) * tm

    def issue(r, carry):
        _row_copy(ys_hbm, dest_ref[t0 + r], buf, r, sem).start()
        return carry

    def drain(r, carry):
        _row_copy(ys_hbm, 0, buf, r, sem).wait()
        return carry

    lax.fori_loop(0, tm, issue, 0)
    lax.fori_loop(0, tm, drain, 0)
    y = x_ref[...] + g_ref[...] * buf[...]
    if final_norm:
        y = (y * lax.rsqrt(jnp.mean(y * y, axis=-1, keepdims=True) + EPS)) * fg_ref[...]
    o_ref[...] = y


def _combine(dest, x, gate, final_g, ys, final_norm, tm=256):
    t, d = x.shape
    vec = pl.BlockSpec((1, d), lambda i, dest: (0, 0))
    grid_spec = pltpu.PrefetchScalarGridSpec(
        num_scalar_prefetch=1,
        grid=(t // tm,),
        in_specs=[pl.BlockSpec((tm, d), lambda i, dest: (i, 0)), vec, vec, pl.BlockSpec(memory_space=pl.ANY)],
        out_specs=pl.BlockSpec((tm, d), lambda i, dest: (i, 0)),
        scratch_shapes=[pltpu.VMEM((tm, d), F32), pltpu.SemaphoreType.DMA],
    )
    return pl.pallas_call(
        functools.partial(_combine_kernel, final_norm=final_norm),
        grid_spec=grid_spec,
        out_shape=jax.ShapeDtypeStruct((t, d), F32),
        compiler_params=_cparams(("arbitrary",)),
        name="combine",
    )(dest, x, gate.reshape(1, d), final_g.reshape(1, d), ys)


def _rope_tables(t):
    pos = jnp.arange(t, dtype=I32).astype(F32)[:, None]
    lane = jnp.arange(LANES)

    def table(dim):
        half = dim // 2
        inv = ROPE_THETA ** (-jnp.arange(half, dtype=F32) / half)
        ang = pos * inv[None, :]
        j = lane % dim
        cos = jnp.cos(ang)[:, j % half]
        sin = jnp.sin(ang)[:, j % half] * jnp.where(j < half, -1.0, 1.0)[None, :]
        return cos, sin

    return table(HEAD_DIM), table(IDX_DIM)


def _attention_layer(x, mod, norm_g, w_in, w_out, tables):
    t, d = x.shape
    sh1, sc1, g1 = mod[0:d], mod[d:2 * d], mod[2 * d:3 * d]
    (cos_h, sin_h), (cos_i, sin_i) = tables
    hn = _normmod(x, norm_g, sc1, sh1)
    w_bf = w_in.astype(BF16)
    nq = N_HEADS * HEAD_DIM
    nkv = N_KV_GROUPS * HEAD_DIM
    nqi = IDX_HEADS * IDX_DIM
    tn = 512
    q = _proj_rope(hn, w_bf, 0, nq, cos_h, sin_h, HEAD_DIM // 2, BF16)
    kv = _proj_kv(hn, w_bf, nq // nkv, nq // nkv + 1, cos_h, sin_h)
    qi = _proj_rope(hn, w_bf, (nq + 2 * nkv) // tn, nqi, cos_i, sin_i, IDX_DIM // 2, BF16)
    tail0 = nq + 2 * nkv + nqi
    w_tail = jnp.pad(w_bf[:, tail0:], ((0, 0), (0, LANES - (IDX_DIM + IDX_HEADS))))
    lane = jnp.arange(LANES)
    wi_scale = (IDX_HEADS ** -0.5) * (IDX_DIM ** -0.5)
    cos_t = jnp.where(lane[None, :] < IDX_DIM, cos_i, wi_scale)
    sin_t = jnp.where(lane[None, :] < IDX_DIM, sin_i, 0.0)
    tail = _proj_rope(hn, w_tail, 0, LANES, cos_t, sin_t, IDX_DIM // 2, F32, tn=LANES)
    ki = tail[:, :IDX_DIM].astype(BF16)
    wit = tail[:, IDX_DIM:IDX_DIM + IDX_HEADS].T
    n_keep = min(TOPK_KEYS, t // 4)
    idx_t, nv = _indexer(qi.T, ki, wit, n_keep)
    att = _gattn(nv.reshape(t), idx_t.T.reshape(t * n_keep), q.reshape(t * N_HEADS, HEAD_DIM), kv.reshape(t * KV_ROWS, LANES), n_keep)
    att = att.reshape(t, nq)
    return _mm_res(att, w_out.astype(BF16), jnp.zeros((d,), F32), x, g1)


def _conv_layer(x, mod, norm_g, w_pw1, b_pw1, w_dw, b_dw, ln_g, ln_b, w_pw2, b_pw2):
    t, d = x.shape
    sh1, sc1, g1 = mod[0:d], mod[d:2 * d], mod[2 * d:3 * d]
    hn = _normmod(x, norm_g, sc1, sh1)
    u = _glu(hn, w_pw1.astype(BF16), b_pw1)
    v = _dwconv_ln(u, w_dw, b_dw, ln_g, ln_b)
    return _mm_res(v, w_pw2.astype(BF16), b_pw2, x, g1)


def _moe_layer(x, mod, norm_g, w_rg, b_rg, w_re, b_re, w_gate, w_up, w_down, final_g, final_norm):
    t, d = x.shape
    sh2, sc2, g2 = mod[3 * d:4 * d], mod[4 * d:5 * d], mod[5 * d:6 * d]
    h, aux, counts = _route(x, norm_g, sc2, sh2, w_rg, b_rg, w_re, b_re)
    bucket = aux[:, AUX_BUCKET].astype(I32)
    rank = aux[:, AUX_RANK].astype(I32)
    sizes = counts[0, :N_BUCKETS].astype(I32)
    padded = (sizes + MOE_TM - 1) // MOE_TM * MOE_TM
    ends = jnp.cumsum(padded)
    dest = (ends - padded)[bucket] + rank
    n_blocks = t // MOE_TM + N_BUCKETS
    blk_start = jnp.arange(n_blocks, dtype=I32) * MOE_TM
    blk_bucket = jnp.minimum(jnp.searchsorted(ends, blk_start, side='right'), N_BUCKETS - 1).astype(I32)
    live = (blk_start < ends[-1]).astype(I32)
    group0 = (blk_bucket // N_PAIRS) * EXPERTS_PER_GROUP
    e_lo = group0 + jnp.asarray(PAIR_LO, I32)[blk_bucket % N_PAIRS]
    e_hi = group0 + jnp.asarray(PAIR_HI, I32)[blk_bucket % N_PAIRS]
    xs, ws = _dispatch(dest, h, aux, n_blocks * MOE_TM)
    ys = _ffn(e_lo, e_hi, live, xs, ws, w_gate.astype(BF16), w_up.astype(BF16), w_down.astype(BF16))
    return _combine(dest, x, g2, final_g, ys, final_norm)


def kernel(x, c, ada_w, ada_b, norm1_g, norm2_g, attn_w_in, attn_w_out, conv_w_pw1, conv_b_pw1, conv_w_dw, conv_b_dw, conv_ln_g, conv_ln_b, conv_w_pw2, conv_b_pw2, router_w_group, router_b_group, router_w_expert, router_b_expert, exp_w_gate, exp_w_up, exp_w_down, final_g):
    bsz, t, d = x.shape
    assert bsz == 1 and d == N_HEADS * HEAD_DIM
    depth = ada_w.shape[0]
    mods = _ada_mod(c, ada_w, ada_b)
    tables = _rope_tables(t)
    xf = x.reshape(t, d)
    for i in range(depth):
        j = i // 2
        if i % 2 == 0:
            xf = _attention_layer(xf, mods[i], norm1_g[i], attn_w_in[j], attn_w_out[j], tables)
        else:
            xf = _conv_layer(xf, mods[i], norm1_g[i], conv_w_pw1[j], conv_b_pw1[j], conv_w_dw[j], conv_b_dw[j],
                             conv_ln_g[j], conv_ln_b[j], conv_w_pw2[j], conv_b_pw2[j])
        xf = _moe_layer(xf, mods[i], norm2_g[i], router_w_group[i], router_b_group[i], router_w_expert[i],
                        router_b_expert[i], exp_w_gate[i], exp_w_up[i], exp_w_down[i], final_g, i == depth - 1)
    return xf.reshape(bsz, t, d)
```

```python
import functools

import jax
import jax.numpy as jnp
from jax import lax
from jax.experimental import pallas as pl
from jax.experimental.pallas import tpu as pltpu

EPS = 1e-6
CHUNK = 64
N_HEADS = 16
N_KV_GROUPS = 4
HEADS_PER_GROUP = N_HEADS // N_KV_GROUPS
HEAD_DIM = 128
IDX_HEADS = 16
IDX_DIM = 64
TOPK_KEYS = 256
ROPE_THETA = 10000.0
NEG_INF = -1e30
CONV_WIDTH = 31
N_GROUPS = 4
EXPERTS_PER_GROUP = 4
N_EXPERTS = N_GROUPS * EXPERTS_PER_GROUP

LANES = 128
KV_ROWS = N_KV_GROUPS
INT_MIN = -(2 ** 31)
INVALID_SHIFT = 1 << 30
VMEM_LIMIT = 56 * 1024 * 1024

F32 = jnp.float32
BF16 = jnp.bfloat16
I32 = jnp.int32


def _cparams(sem):
    return pltpu.CompilerParams(dimension_semantics=sem, vmem_limit_bytes=VMEM_LIMIT)


def _ada_kernel(c_ref, w_ref, b_ref, o_ref):
    c = c_ref[...]
    ca = c * jax.nn.sigmoid(c)
    o_ref[0] = jnp.dot(ca, w_ref[0], preferred_element_type=F32, precision=lax.Precision.HIGHEST) + b_ref[0]


def _ada_mod(c, ada_w, ada_b):
    depth, d, n = ada_w.shape
    tn = 768
    c8 = jnp.zeros((8, d), F32).at[0].set(c[0])
    out = pl.pallas_call(
        _ada_kernel,
        grid=(depth, n // tn),
        in_specs=[pl.BlockSpec((8, d), lambda i, j: (0, 0)),
                  pl.BlockSpec((1, d, tn), lambda i, j: (i, 0, j)),
                  pl.BlockSpec((1, 1, tn), lambda i, j: (i, 0, j))],
        out_specs=pl.BlockSpec((1, 8, tn), lambda i, j: (i, 0, j)),
        out_shape=jax.ShapeDtypeStruct((depth, 8, n), F32),
        compiler_params=_cparams(("arbitrary", "arbitrary")),
        name="ada_mod",
    )(c8, ada_w, ada_b.reshape(depth, 1, n))
    return out[:, 0, :]


def _normmod_kernel(x_ref, g_ref, sc_ref, sh_ref, o_ref):
    x = x_ref[...]
    xn = x * lax.rsqrt(jnp.mean(x * x, axis=-1, keepdims=True) + EPS)
    o_ref[...] = ((xn * g_ref[...]) * (1.0 + sc_ref[...]) + sh_ref[...]).astype(o_ref.dtype)


def _normmod(x, g, sc, sh, tm=512):
    t, d = x.shape
    vec = pl.BlockSpec((1, d), lambda i: (0, 0))
    return pl.pallas_call(
        _normmod_kernel,
        grid=(t // tm,),
        in_specs=[pl.BlockSpec((tm, d), lambda i: (i, 0)), vec, vec, vec],
        out_specs=pl.BlockSpec((tm, d), lambda i: (i, 0)),
        out_shape=jax.ShapeDtypeStruct((t, d), BF16),
        compiler_params=_cparams(("arbitrary",)),
        name="normmod",
    )(x, g.reshape(1, d), sc.reshape(1, d), sh.reshape(1, d))


N_PAIRS = EXPERTS_PER_GROUP * (EXPERTS_PER_GROUP - 1) // 2
N_BUCKETS = N_GROUPS * N_PAIRS
PAIR_LO = (0, 0, 0, 1, 1, 2)
PAIR_HI = (1, 2, 3, 2, 3, 3)
MOE_TM = 256
AUX_W_LO, AUX_W_HI, AUX_BUCKET, AUX_RANK = 0, 1, 2, 3


def _route_kernel(x_ref, g_ref, sc_ref, sh_ref, wr_ref, br_ref, tri_ref, h_ref, aux_ref, cnt_ref, run_ref):
    @pl.when(pl.program_id(0) == 0)
    def _():
        run_ref[...] = jnp.zeros_like(run_ref)

    x = x_ref[...]
    xn = x * lax.rsqrt(jnp.mean(x * x, axis=-1, keepdims=True) + EPS)
    h = (xn * g_ref[...]) * (1.0 + sc_ref[...]) + sh_ref[...]
    h_ref[...] = h.astype(h_ref.dtype)
    logits = jnp.dot(h, wr_ref[...], preferred_element_type=F32, precision=lax.Precision.HIGHEST) + br_ref[...]
    tm = x.shape[0]
    lane4 = lax.broadcasted_iota(I32, (tm, N_GROUPS), 1)

    def first_argmax(v):
        m = jnp.max(v, axis=-1, keepdims=True)
        return m, jnp.min(jnp.where(v == m, lane4, N_GROUPS), axis=-1, keepdims=True)

    gl = logits[:, 0:N_GROUPS]
    gmax, gidx = first_argmax(gl)
    g_w = 1.0 / jnp.sum(jnp.exp(gl - gmax), axis=-1, keepdims=True)
    el = jnp.zeros((tm, EXPERTS_PER_GROUP), F32)
    for g in range(N_GROUPS):
        lo = N_GROUPS + g * EXPERTS_PER_GROUP
        el = jnp.where(gidx == g, logits[:, lo:lo + EXPERTS_PER_GROUP], el)
    emax = jnp.max(el, axis=-1, keepdims=True)
    ee = jnp.exp(el - emax)
    ep = ee / jnp.sum(ee, axis=-1, keepdims=True)
    p1, i1 = first_argmax(ep)
    rest = jnp.where(lane4 == i1, -1.0, ep)
    p2, i2 = first_argmax(rest)
    den = p1 + p2
    w1 = g_w * p1 / den
    w2 = g_w * p2 / den
    first_lo = i1 < i2
    a = jnp.minimum(i1, i2)
    b = jnp.maximum(i1, i2)
    w_lo = jnp.where(first_lo, w1, w2)
    w_hi = jnp.where(first_lo, w2, w1)
    pair = jnp.where(a == 0, b - 1, jnp.where(a == 1, b + 1, N_PAIRS - 1))
    bucket = gidx * N_PAIRS + pair
    lane = lax.broadcasted_iota(I32, (tm, LANES), 1)
    onehot = lane == bucket
    prefix = jnp.dot(tri_ref[...], onehot.astype(F32).astype(BF16), preferred_element_type=F32)
    run = run_ref[...]
    rank = jnp.sum(jnp.where(onehot, prefix + run, 0.0), axis=1, keepdims=True) - 1.0
    run = run + prefix[tm - 1:tm, :]
    run_ref[...] = run
    cnt_ref[...] = run
    aux_ref[...] = (jnp.where(lane == AUX_W_LO, w_lo, 0.0) + jnp.where(lane == AUX_W_HI, w_hi, 0.0)
                    + jnp.where(lane == AUX_BUCKET, bucket.astype(F32), 0.0) + jnp.where(lane == AUX_RANK, rank, 0.0))


def _route(x, g, sc, sh, w_rg, b_rg, w_re, b_re, tm=512):
    t, d = x.shape
    w_all = jnp.concatenate([w_rg, jnp.moveaxis(w_re, 0, 1).reshape(d, N_EXPERTS)], axis=1)
    b_all = jnp.concatenate([b_rg, b_re.reshape(N_EXPERTS)])
    npad = LANES - w_all.shape[1]
    w_all = jnp.pad(w_all, ((0, 0), (0, npad)))
    b_all = jnp.pad(b_all, (0, npad)).reshape(1, LANES)
    vec = pl.BlockSpec((1, d), lambda i: (0, 0))
    r = lax.broadcasted_iota(I32, (tm, tm), 0)
    c = lax.broadcasted_iota(I32, (tm, tm), 1)
    tri = (c <= r).astype(BF16)
    return pl.pallas_call(
        _route_kernel,
        grid=(t // tm,),
        in_specs=[pl.BlockSpec((tm, d), lambda i: (i, 0)), vec, vec, vec,
                  pl.BlockSpec((d, LANES), lambda i: (0, 0)), pl.BlockSpec((1, LANES), lambda i: (0, 0)),
                  pl.BlockSpec((tm, tm), lambda i: (0, 0))],
        out_specs=[pl.BlockSpec((tm, d), lambda i: (i, 0)), pl.BlockSpec((tm, LANES), lambda i: (i, 0)),
                   pl.BlockSpec((1, LANES), lambda i: (0, 0))],
        out_shape=[jax.ShapeDtypeStruct((t, d), F32), jax.ShapeDtypeStruct((t, LANES), F32),
                   jax.ShapeDtypeStruct((1, LANES), F32)],
        scratch_shapes=[pltpu.VMEM((1, LANES), F32)],
        compiler_params=_cparams(("arbitrary",)),
        name="route",
    )(x, g.reshape(1, d), sc.reshape(1, d), sh.reshape(1, d), w_all, b_all, tri)


def _rope_block(blk, cos, sin, half):
    if half == HEAD_DIM // 2:
        partner = pltpu.roll(blk, HEAD_DIM // 2, 1)
    else:
        lane = lax.broadcasted_iota(I32, blk.shape, 1)
        first = (lane % IDX_DIM) < (IDX_DIM // 2)
        partner = jnp.where(first, pltpu.roll(blk, LANES - IDX_DIM // 2, 1), pltpu.roll(blk, IDX_DIM // 2, 1))
    return blk * cos + partner * sin


def _proj_rope_kernel(a_ref, w_ref, cos_ref, sin_ref, o_ref, *, half):
    acc = jnp.dot(a_ref[...], w_ref[...], preferred_element_type=F32)
    cos = cos_ref[...]
    sin = sin_ref[...]
    for h in range(acc.shape[1] // LANES):
        sl = slice(h * LANES, (h + 1) * LANES)
        o_ref[:, sl] = _rope_block(acc[:, sl], cos, sin, half).astype(o_ref.dtype)


def _proj_rope(a, w, col_block0, n_out, cos, sin, half, out_dtype, tm=512, tn=512):
    t, d = a.shape
    return pl.pallas_call(
        functools.partial(_proj_rope_kernel, half=half),
        grid=(t // tm, n_out // tn),
        in_specs=[pl.BlockSpec((tm, d), lambda i, j: (i, 0)),
                  pl.BlockSpec((d, tn), lambda i, j: (0, col_block0 + j)),
                  pl.BlockSpec((tm, LANES), lambda i, j: (i, 0)),
                  pl.BlockSpec((tm, LANES), lambda i, j: (i, 0))],
        out_specs=pl.BlockSpec((tm, tn), lambda i, j: (i, j)),
        out_shape=jax.ShapeDtypeStruct((t, n_out), out_dtype),
        compiler_params=_cparams(("arbitrary", "arbitrary")),
        name="proj_rope",
    )(a, w, cos, sin)


def _proj_kv_kernel(a_ref, wk_ref, wv_ref, cos_ref, sin_ref, o_ref):
    a = a_ref[...]
    k = jnp.dot(a, wk_ref[...], preferred_element_type=F32)
    v = jnp.dot(a, wv_ref[...], preferred_element_type=F32)
    cos = cos_ref[...]
    sin = sin_ref[...]
    for h in range(k.shape[1] // LANES):
        sl = slice(h * LANES, (h + 1) * LANES)
        kr = _rope_block(k[:, sl], cos, sin, HEAD_DIM // 2)
        kb = lax.bitcast_convert_type(kr.astype(BF16).astype(F32), I32)
        vb = lax.bitcast_convert_type(v[:, sl].astype(BF16).astype(F32), I32)
        o_ref[:, sl] = kb | lax.shift_right_logical(vb, 16)


def _proj_kv(a, w, kblock, vblock, cos, sin, tm=512):
    t, d = a.shape
    n = N_KV_GROUPS * HEAD_DIM
    return pl.pallas_call(
        _proj_kv_kernel,
        grid=(t // tm,),
        in_specs=[pl.BlockSpec((tm, d), lambda i: (i, 0)),
                  pl.BlockSpec((d, n), lambda i: (0, kblock)),
                  pl.BlockSpec((d, n), lambda i: (0, vblock)),
                  pl.BlockSpec((tm, LANES), lambda i: (i, 0)),
                  pl.BlockSpec((tm, LANES), lambda i: (i, 0))],
        out_specs=pl.BlockSpec((tm, n), lambda i: (i, 0)),
        out_shape=jax.ShapeDtypeStruct((t, n), I32),
        compiler_params=_cparams(("arbitrary",)),
        name="proj_kv",
    )(a, w, w, cos, sin)


QB = 256
TK = 512
RC = 64


def _indexer_kernel(qit_ref, ki_ref, wit_ref, tri_ref, idx_ref, nv_ref, d_ref, *, n_keep):
    b = pl.program_id(0)
    q0 = b * QB
    n_t = (q0 + QB + TK - 1) // TK
    n_rows = n_t * TK
    rowi = lax.broadcasted_iota(I32, (TK, QB), 0)
    qchunk = (q0 + lax.broadcasted_iota(I32, (TK, QB), 1)) // CHUNK

    def admissible(k0):
        return ((k0 + rowi) // CHUNK) <= qchunk

    def score_tile(kt, carry):
        k0 = pl.multiple_of(kt * TK, TK)
        ki_t = ki_ref[pl.ds(k0, TK), :]
        acc = jnp.zeros((TK, QB), F32)
        for h in range(IDX_HEADS):
            a = jnp.dot(ki_t, qit_ref[h * IDX_DIM:(h + 1) * IDX_DIM, :], preferred_element_type=F32)
            acc = acc + jnp.maximum(a, 0.0) * wit_ref[h:h + 1, :]
        s = jnp.where(admissible(k0), acc + 0.0, NEG_INF)
        u = lax.bitcast_convert_type(s, I32)
        d_ref[pl.ds(k0, TK), :] = jnp.where(u < 0, u ^ 0x7FFFFFFF, u)
        return carry

    lax.fori_loop(0, n_t, score_tile, 0)

    def count_where(pred):
        def body(kt, acc):
            k0 = pl.multiple_of(kt * TK, TK)
            hit = pred(d_ref[pl.ds(k0, TK), :]).astype(I32)
            return acc + jnp.sum(hit.reshape(TK // 8, 8, QB), axis=0)
        acc = lax.fori_loop(0, n_t, body, jnp.zeros((8, QB), I32))
        return jnp.sum(acc, axis=0, keepdims=True)

    def select_bit(it, carry):
        v, cnt_v = carry
        bit = jnp.left_shift(jnp.int32(1), 31 - it)
        cand = jnp.where(it == 0, jnp.zeros_like(v), v | bit)
        cnt = count_where(lambda key: key >= cand)
        keep = cnt >= n_keep
        return jnp.where(keep, cand, v), jnp.where(keep, cnt, cnt_v)

    thr, cnt_thr = lax.fori_loop(0, 32, select_bit,
                                 (jnp.full((1, QB), INT_MIN, I32), jnp.full((1, QB), n_keep + 1, I32)))

    tri = tri_ref[...]
    zero = jnp.zeros((1, QB), F32)

    def place(k0, sel, run_sel):
        pre_sel = jnp.dot(tri, sel.astype(F32).astype(BF16), preferred_element_type=F32)
        rank = (run_sel + pre_sel).astype(I32) - 1
        d_ref[pl.ds(k0, TK), :] = jnp.where(sel, (k0 + rowi) - rank, INVALID_SHIFT)
        return run_sel + pre_sel[TK - 1:TK, :]

    def rank_exact_count():
        def tile(kt, run_sel):
            k0 = pl.multiple_of(kt * TK, TK)
            return place(k0, admissible(k0) & (d_ref[pl.ds(k0, TK), :] >= thr), run_sel)
        return lax.fori_loop(0, n_t, tile, zero)

    def rank_with_ties():
        need = (n_keep - count_where(lambda key: key > thr)).astype(F32)

        def tile(kt, carry):
            run_eq, run_sel = carry
            k0 = pl.multiple_of(kt * TK, TK)
            key = d_ref[pl.ds(k0, TK), :]
            eq = key == thr
            eq_f = eq.astype(F32)
            pre_eq = jnp.dot(tri, eq_f.astype(BF16), preferred_element_type=F32)
            sel = admissible(k0) & ((key > thr) | (eq & ((run_eq + pre_eq - eq_f) < need)))
            return run_eq + pre_eq[TK - 1:TK, :], place(k0, sel, run_sel)
        return lax.fori_loop(0, n_t, tile, (zero, zero))[1]

    n_sel = lax.cond(jnp.max(jnp.abs(cnt_thr - n_keep)) == 0, rank_exact_count, rank_with_ties)

    d_ref[pl.ds(pl.multiple_of(n_rows, RC), RC), :] = jnp.full((RC, QB), INVALID_SHIFT, I32)
    n_chunks = n_rows // RC
    win_chunks = -(-n_keep // RC)
    window = win_chunks * RC
    n_stages = (d_ref.shape[0] - RC - 1).bit_length()
    for s in range(n_stages):
        sh = 1 << s
        test = sh | INVALID_SHIFT

        def merge(own, inc, sh=sh, test=test):
            take = (inc & test) == sh
            stay = (own & test) == 0
            return jnp.where(take, inc, jnp.where(stay, own, INVALID_SHIFT))

        def chunk_with_source(c, carry, sh=sh, merge=merge):
            r0 = pl.multiple_of(c * RC, RC)
            if sh < 8:
                x = d_ref[pl.ds(r0, RC + 8), :]
                own, inc = x[0:RC], x[sh:sh + RC]
            else:
                own = d_ref[pl.ds(r0, RC), :]
                inc = d_ref[pl.ds(pl.multiple_of(r0 + sh, 8), RC), :]
            d_ref[pl.ds(r0, RC), :] = merge(own, inc)
            return carry

        def chunk_without_source(c, carry, merge=merge):
            r0 = pl.multiple_of(c * RC, RC)
            own = d_ref[pl.ds(r0, RC), :]
            d_ref[pl.ds(r0, RC), :] = merge(own, jnp.full_like(own, INVALID_SHIFT))
            return carry

        def window_chunk(c, carry, sh=sh, merge=merge):
            r0 = pl.multiple_of((c // win_chunks) * sh + (c % win_chunks) * RC, RC)
            src = r0 + sh
            own = d_ref[pl.ds(r0, RC), :]
            inc = d_ref[pl.ds(pl.multiple_of(jnp.minimum(src, n_rows), RC), RC), :]
            d_ref[pl.ds(r0, RC), :] = merge(own, inc)
            return carry

        if sh < RC:
            lax.fori_loop(0, n_chunks, chunk_with_source, 0)
        elif sh < 2 * window:
            n_src = jnp.maximum(n_rows - sh, 0) // RC
            lax.fori_loop(0, n_src, chunk_with_source, 0)
            lax.fori_loop(n_src, n_chunks, chunk_without_source, 0)
        else:
            lax.fori_loop(0, ((n_rows + sh - 1) // sh) * win_chunks, window_chunk, 0)

    slot = lax.broadcasted_iota(I32, (n_keep, QB), 0)
    dd = d_ref[0:n_keep, :]
    idx_ref[...] = jnp.where(dd < INVALID_SHIFT, (slot + dd) * KV_ROWS, 0)
    nv_ref[...] = n_sel.astype(I32)


def _indexer(qit, ki, wit, n_keep):
    t = ki.shape[0]
    r = lax.broadcasted_iota(I32, (TK, TK), 0)
    c = lax.broadcasted_iota(I32, (TK, TK), 1)
    tri = (c <= r).astype(BF16)
    return pl.pallas_call(
        functools.partial(_indexer_kernel, n_keep=n_keep),
        grid=(t // QB,),
        in_specs=[pl.BlockSpec((IDX_HEADS * IDX_DIM, QB), lambda b: (0, b)),
                  pl.BlockSpec((t, IDX_DIM), lambda b: (0, 0)),
                  pl.BlockSpec((IDX_HEADS, QB), lambda b: (0, b)),
                  pl.BlockSpec((TK, TK), lambda b: (0, 0))],
        out_specs=[pl.BlockSpec((n_keep, QB), lambda b: (0, b)), pl.BlockSpec((1, QB), lambda b: (0, b))],
        out_shape=[jax.ShapeDtypeStruct((n_keep, t), I32), jax.ShapeDtypeStruct((1, t), I32)],
        scratch_shapes=[pltpu.VMEM((t + RC, QB), I32)],
        compiler_params=_cparams(("arbitrary",)),
        name="indexer",
    )(qit, ki, wit, tri)


GQ = 64


def _gattn_kernel(nv_ref, idx_ref, q_ref, kv_hbm, o_ref, kv_vmem, stage_a, stage_b, stage_c, stage_d,
                  s_a, s_b, row_lists, sem_kv, sem_rows, *, n_keep):
    i = pl.program_id(0)

    @pl.when(i == 0)
    def _():
        cp = pltpu.make_async_copy(kv_hbm, kv_vmem, sem_kv)
        cp.start()
        cp.wait()

    cp = pltpu.make_async_copy(idx_ref, row_lists, sem_rows)
    cp.start()
    cp.wait()

    n_col = n_keep * KV_ROWS
    colg = lax.broadcasted_iota(I32, (N_HEADS, n_col), 1)
    rowh = lax.broadcasted_iota(I32, (N_HEADS, n_col), 0)
    own_group = (colg % KV_ROWS) == (rowh // HEADS_PER_GROUP)
    col_slot = colg // KV_ROWS
    scale = HEAD_DIM ** -0.5

    last = GQ - 1
    stages = (stage_a, stage_b, stage_c, stage_d)
    s_bufs = (s_a, s_b)

    def gather(r, stage):
        rows = row_lists.at[pl.ds(jnp.minimum(r, last), 1)]
        for j in range(n_keep):
            row = pl.multiple_of(rows[0, j], KV_ROWS)
            stage[j * KV_ROWS:(j + 1) * KV_ROWS, :] = kv_vmem[pl.ds(row, KV_ROWS), :]

    def scores(r, stage, s_buf):
        k_all = lax.bitcast_convert_type(stage[...] & jnp.int32(-65536), F32).astype(BF16)
        qh = q_ref[pl.ds(pl.multiple_of(r * N_HEADS, N_HEADS), N_HEADS), :]
        s = lax.dot_general(qh, k_all, (((1,), (1,)), ((), ())), preferred_element_type=F32) * scale
        mask = own_group & (col_slot < nv_ref[i * GQ + r])
        s_buf[...] = jnp.where(mask, s, NEG_INF)

    def combine(r, stage, s_buf):
        s = s_buf[...]
        p = jnp.exp(s - jnp.max(s, axis=1, keepdims=True))
        l = jnp.sum(p, axis=1, keepdims=True)
        v_all = lax.bitcast_convert_type(jnp.left_shift(stage[...], 16), F32).astype(BF16)
        o = jnp.dot(p.astype(BF16), v_all, preferred_element_type=F32) / l
        o_ref[pl.ds(pl.multiple_of(r * N_HEADS, N_HEADS), N_HEADS), :] = o.astype(o_ref.dtype)

    gather(0, stages[0])
    gather(1, stages[1])
    scores(0, stages[0], s_bufs[0])

    def query_quad(it, carry):
        r = 4 * it
        for u in range(4):
            gather(r + u + 2, stages[(u + 2) % 4])
            scores(jnp.minimum(r + u + 1, last), stages[(u + 1) % 4], s_bufs[(u + 1) % 2])
            combine(r + u, stages[u], s_bufs[u % 2])
        return carry

    lax.fori_loop(0, GQ // 4, query_quad, 0)


def _gattn(nv, idx, q2, kv2, n_keep):
    t = nv.shape[0]
    grid_spec = pltpu.PrefetchScalarGridSpec(
        num_scalar_prefetch=1,
        grid=(t // GQ,),
        in_specs=[pl.BlockSpec((GQ, n_keep), lambda i, nv: (i, 0)),
                  pl.BlockSpec((GQ * N_HEADS, HEAD_DIM), lambda i, nv: (i, 0)),
                  pl.BlockSpec(memory_space=pl.ANY)],
        out_specs=pl.BlockSpec((GQ * N_HEADS, HEAD_DIM), lambda i, nv: (i, 0)),
        scratch_shapes=[pltpu.VMEM(kv2.shape, I32),
                        pltpu.VMEM((n_keep * KV_ROWS, LANES), I32),
                        pltpu.VMEM((n_keep * KV_ROWS, LANES), I32),
                        pltpu.VMEM((n_keep * KV_ROWS, LANES), I32),
                        pltpu.VMEM((n_keep * KV_ROWS, LANES), I32),
                        pltpu.VMEM((N_HEADS, n_keep * KV_ROWS), F32),
                        pltpu.VMEM((N_HEADS, n_keep * KV_ROWS), F32),
                        pltpu.SMEM((GQ, n_keep), I32),
                        pltpu.SemaphoreType.DMA,
                        pltpu.SemaphoreType.DMA],
    )
    return pl.pallas_call(
        functools.partial(_gattn_kernel, n_keep=n_keep),
        grid_spec=grid_spec,
        out_shape=jax.ShapeDtypeStruct((t * N_HEADS, HEAD_DIM), BF16),
        compiler_params=_cparams(("arbitrary",)),
        name="gattn",
    )(nv, idx, q2, kv2)


def _mm_res_kernel(a_ref, w_ref, b_ref, x_ref, g_ref, o_ref):
    y = jnp.dot(a_ref[...], w_ref[...], preferred_element_type=F32) + b_ref[...]
    o_ref[...] = x_ref[...] + g_ref[...] * y


def _mm_res(a, w, bias, x, gate, tm=512, tn=512):
    t, k = a.shape
    n = w.shape[1]
    return pl.pallas_call(
        _mm_res_kernel,
        grid=(t // tm, n // tn),
        in_specs=[pl.BlockSpec((tm, k), lambda i, j: (i, 0)),
                  pl.BlockSpec((k, tn), lambda i, j: (0, j)),
                  pl.BlockSpec((1, tn), lambda i, j: (0, j)),
                  pl.BlockSpec((tm, tn), lambda i, j: (i, j)),
                  pl.BlockSpec((1, tn), lambda i, j: (0, j))],
        out_specs=pl.BlockSpec((tm, tn), lambda i, j: (i, j)),
        out_shape=jax.ShapeDtypeStruct((t, n), F32),
        compiler_params=_cparams(("arbitrary", "arbitrary")),
        name="mm_res",
    )(a, w, bias.reshape(1, n), x, gate.reshape(1, n))


def _glu_kernel(a_ref, wa_ref, wg_ref, ba_ref, bg_ref, o_ref):
    a = a_ref[...]
    ya = jnp.dot(a, wa_ref[...], preferred_element_type=F32) + ba_ref[...]
    yg = jnp.dot(a, wg_ref[...], preferred_element_type=F32) + bg_ref[...]
    o_ref[...] = ya * jax.nn.sigmoid(yg)


def _glu(a, w, b, tm=512, tn=512):
    t, k = a.shape
    n = w.shape[1] // 2
    nb = n // tn
    b2 = b.reshape(1, 2 * n)
    return pl.pallas_call(
        _glu_kernel,
        grid=(t // tm, nb),
        in_specs=[pl.BlockSpec((tm, k), lambda i, j: (i, 0)),
                  pl.BlockSpec((k, tn), lambda i, j: (0, j)),
                  pl.BlockSpec((k, tn), lambda i, j: (0, j + nb)),
                  pl.BlockSpec((1, tn), lambda i, j: (0, j)),
                  pl.BlockSpec((1, tn), lambda i, j: (0, j + nb))],
        out_specs=pl.BlockSpec((tm, tn), lambda i, j: (i, j)),
        out_shape=jax.ShapeDtypeStruct((t, n), F32),
        compiler_params=_cparams(("arbitrary", "arbitrary")),
        name="glu",
    )(a, w, w, b2, b2)


HALO = 32


SUBLANES = 8
CONV_ROWS = 32
CONV_COLS = 512


def _dwconv_ln_kernel(u_ref, prev_ref, w_ref, b_ref, g_ref, beta_ref, o_ref, ext_ref, sh_ref, conv_ref):
    i = pl.program_id(0)
    tm, d = u_ref.shape
    ext_ref[0:HALO, :] = jnp.where(i == 0, 0.0, prev_ref[...])
    ext_ref[HALO:HALO + tm, :] = u_ref[...]
    for phase in range(SUBLANES):
        sh_ref[phase, 0:tm + HALO - phase, :] = ext_ref[phase:tm + HALO, :]
    first = HALO - (CONV_WIDTH - 1)

    def conv_rows(c, carry):
        r0 = pl.multiple_of(c * CONV_ROWS, CONV_ROWS)
        for cb in range(d // CONV_COLS):
            cols = slice(cb * CONV_COLS, (cb + 1) * CONV_COLS)
            acc = jnp.zeros((CONV_ROWS, CONV_COLS), F32) + b_ref[:, cols]
            for k in range(CONV_WIDTH):
                phase = (first + k) % SUBLANES
                start = pl.multiple_of(r0 + (first + k - phase), SUBLANES)
                acc = acc + sh_ref[phase, pl.ds(start, CONV_ROWS), cols] * w_ref[k:k + 1, cols]
            conv_ref[pl.ds(r0, CONV_ROWS), cols] = acc
        return carry

    lax.fori_loop(0, tm // CONV_ROWS, conv_rows, 0)
    acc = conv_ref[...]
    mu = jnp.mean(acc, axis=-1, keepdims=True)
    cen = acc - mu
    var = jnp.mean(cen * cen, axis=-1, keepdims=True)
    y = (cen * lax.rsqrt(var + EPS)) * g_ref[...] + beta_ref[...]
    o_ref[...] = (y * jax.nn.sigmoid(y)).astype(o_ref.dtype)


def _dwconv_ln(u, w_dw, b_dw, ln_g, ln_b, tm=256):
    t, d = u.shape
    ratio = tm // HALO
    vec = pl.BlockSpec((1, d), lambda i: (0, 0))
    return pl.pallas_call(
        _dwconv_ln_kernel,
        grid=(t // tm,),
        in_specs=[pl.BlockSpec((tm, d), lambda i: (i, 0)),
                  pl.BlockSpec((HALO, d), lambda i: (jnp.maximum(i * ratio - 1, 0), 0)),
                  pl.BlockSpec((HALO, d), lambda i: (0, 0)),
                  vec, vec, vec],
        out_specs=pl.BlockSpec((tm, d), lambda i: (i, 0)),
        out_shape=jax.ShapeDtypeStruct((t, d), BF16),
        scratch_shapes=[pltpu.VMEM((HALO + tm, d), F32), pltpu.VMEM((SUBLANES, HALO + tm, d), F32),
                        pltpu.VMEM((tm, d), F32)],
        compiler_params=_cparams(("arbitrary",)),
        name="dwconv_ln",
    )(u, u, jnp.pad(w_dw, ((0, HALO - CONV_WIDTH), (0, 0))), b_dw.reshape(1, d), ln_g.reshape(1, d), ln_b.reshape(1, d))


def _row_copy(src, src_row, dst, dst_row, sem):
    return pltpu.make_async_copy(src.at[pl.ds(src_row, 1), :], dst.at[pl.ds(dst_row, 1), :], sem)


ROW_UNROLL = 8


def _for_rows(n, fn):
    def group(gi, carry):
        for u in range(ROW_UNROLL):
            fn(gi * ROW_UNROLL + u, carry)
        return carry

    lax.fori_loop(0, n // ROW_UNROLL, group, 0)


def _dispatch_kernel(dest_ref, h_ref, aux_ref, xs_in, ws_in, xs_hbm, ws_hbm, sem_x, sem_w):
    del xs_in, ws_in
    tm = h_ref.shape[0]
    t0 = pl.program_id(0) * tm

    def issue(r, carry):
        slot = dest_ref[t0 + r]
        _row_copy(h_ref, r, xs_hbm, slot, sem_x).start()
        _row_copy(aux_ref, r, ws_hbm, slot, sem_w).start()
        return carry

    def drain(r, carry):
        _row_copy(h_ref, r, xs_hbm, 0, sem_x).wait()
        _row_copy(aux_ref, r, ws_hbm, 0, sem_w).wait()
        return carry

    _for_rows(tm, issue)
    _for_rows(tm, drain)


def _dispatch(dest, h, aux, n_rows, tm=256):
    t, d = h.shape
    grid_spec = pltpu.PrefetchScalarGridSpec(
        num_scalar_prefetch=1,
        grid=(t // tm,),
        in_specs=[pl.BlockSpec((tm, d), lambda i, dest: (i, 0)),
                  pl.BlockSpec((tm, LANES), lambda i, dest: (i, 0)),
                  pl.BlockSpec(memory_space=pl.ANY),
                  pl.BlockSpec(memory_space=pl.ANY)],
        out_specs=[pl.BlockSpec(memory_space=pl.ANY), pl.BlockSpec(memory_space=pl.ANY)],
        scratch_shapes=[pltpu.SemaphoreType.DMA, pltpu.SemaphoreType.DMA],
    )
    return pl.pallas_call(
        _dispatch_kernel,
        grid_spec=grid_spec,
        out_shape=[jax.ShapeDtypeStruct((n_rows, d), F32), jax.ShapeDtypeStruct((n_rows, LANES), F32)],
        input_output_aliases={3: 0, 4: 1},
        compiler_params=_cparams(("arbitrary",)),
        name="dispatch",
    )(dest, h, aux, jnp.zeros((n_rows, d), F32), jnp.zeros((n_rows, LANES), F32))


def _ffn_kernel(elo_ref, ehi_ref, live_ref, xs_ref, ws_ref, wg_lo, wu_lo, wd_lo, wg_hi, wu_hi, wd_hi, ys_ref):
    b = pl.program_id(0)

    @pl.when(live_ref[b] == 1)
    def _():
        x = xs_ref[...].astype(BF16)
        ws = ws_ref[...]

        def expert(wg, wu, wd):
            gate = jnp.dot(x, wg[0], preferred_element_type=F32)
            up = jnp.dot(x, wu[0], preferred_element_type=F32)
            hid = (gate * jax.nn.sigmoid(gate)) * up
            return jnp.dot(hid.astype(BF16), wd[0], preferred_element_type=F32)

        ys_ref[...] = (expert(wg_lo, wu_lo, wd_lo) * ws[:, AUX_W_LO:AUX_W_LO + 1]
                       + expert(wg_hi, wu_hi, wd_hi) * ws[:, AUX_W_HI:AUX_W_HI + 1])

    @pl.when(live_ref[b] == 0)
    def _():
        ys_ref[...] = jnp.zeros_like(ys_ref)


def _ffn(e_lo, e_hi, live, xs, ws, w_gate, w_up, w_down):
    n_rows, d = xs.shape
    _, _, de = w_gate.shape
    lo_in = pl.BlockSpec((1, d, de), lambda b, lo, hi, live: (lo[b], 0, 0))
    hi_in = pl.BlockSpec((1, d, de), lambda b, lo, hi, live: (hi[b], 0, 0))
    lo_out = pl.BlockSpec((1, de, d), lambda b, lo, hi, live: (lo[b], 0, 0))
    hi_out = pl.BlockSpec((1, de, d), lambda b, lo, hi, live: (hi[b], 0, 0))
    grid_spec = pltpu.PrefetchScalarGridSpec(
        num_scalar_prefetch=3,
        grid=(n_rows // MOE_TM,),
        in_specs=[pl.BlockSpec((MOE_TM, d), lambda b, lo, hi, live: (b, 0)),
                  pl.BlockSpec((MOE_TM, LANES), lambda b, lo, hi, live: (b, 0)),
                  lo_in, lo_in, lo_out, hi_in, hi_in, hi_out],
        out_specs=pl.BlockSpec((MOE_TM, d), lambda b, lo, hi, live: (b, 0)),
    )
    return pl.pallas_call(
        _ffn_kernel,
        grid_spec=grid_spec,
        out_shape=jax.ShapeDtypeStruct((n_rows, d), F32),
        compiler_params=_cparams(("arbitrary",)),
        name="ffn",
    )(e_lo, e_hi, live, xs, ws, w_gate, w_up, w_down, w_gate, w_up, w_down)


def _combine_kernel(dest_ref, x_ref, g_ref, fg_ref, ys_hbm, o_ref, buf, sem, *, final_norm):
    tm = x_ref.shape[0]
    t0 = pl.program_id(0) * tm

    def issue(r, carry):
        _row_copy(ys_hbm, dest_ref[t0 + r], buf, r, sem).start()
        return carry

    def drain(r, carry):
        _row_copy(ys_hbm, 0, buf, r, sem).wait()
        return carry

    _for_rows(tm, issue)
    _for_rows(tm, drain)
    y = x_ref[...] + g_ref[...] * buf[...]
    if final_norm:
        y = (y * lax.rsqrt(jnp.mean(y * y, axis=-1, keepdims=True) + EPS)) * fg_ref[...]
    o_ref[...] = y


def _combine(dest, x, gate, final_g, ys, final_norm, tm=256):
    t, d = x.shape
    vec = pl.BlockSpec((1, d), lambda i, dest: (0, 0))
    grid_spec = pltpu.PrefetchScalarGridSpec(
        num_scalar_prefetch=1,
        grid=(t // tm,),
        in_specs=[pl.BlockSpec((tm, d), lambda i, dest: (i, 0)), vec, vec, pl.BlockSpec(memory_space=pl.ANY)],
        out_specs=pl.BlockSpec((tm, d), lambda i, dest: (i, 0)),
        scratch_shapes=[pltpu.VMEM((tm, d), F32), pltpu.SemaphoreType.DMA],
    )
    return pl.pallas_call(
        functools.partial(_combine_kernel, final_norm=final_norm),
        grid_spec=grid_spec,
        out_shape=jax.ShapeDtypeStruct((t, d), F32),
        compiler_params=_cparams(("arbitrary",)),
        name="combine",
    )(dest, x, gate.reshape(1, d), final_g.reshape(1, d), ys)


def _rope_tables(t):
    pos = jnp.arange(t, dtype=I32).astype(F32)[:, None]
    lane = jnp.arange(LANES)

    def table(dim):
        half = dim // 2
        inv = ROPE_THETA ** (-jnp.arange(half, dtype=F32) / half)
        ang = pos * inv[None, :]
        j = lane % dim
        cos = jnp.cos(ang)[:, j % half]
        sin = jnp.sin(ang)[:, j % half] * jnp.where(j < half, -1.0, 1.0)[None, :]
        return cos, sin

    return table(HEAD_DIM), table(IDX_DIM)


def _attention_layer(x, mod, norm_g, w_in, w_out, tables):
    t, d = x.shape
    sh1, sc1, g1 = mod[0:d], mod[d:2 * d], mod[2 * d:3 * d]
    (cos_h, sin_h), (cos_i, sin_i) = tables
    hn = _normmod(x, norm_g, sc1, sh1)
    w_bf = w_in.astype(BF16)
    nq = N_HEADS * HEAD_DIM
    nkv = N_KV_GROUPS * HEAD_DIM
    nqi = IDX_HEADS * IDX_DIM
    tn = 512
    q = _proj_rope(hn, w_bf, 0, nq, cos_h, sin_h, HEAD_DIM // 2, BF16)
    kv = _proj_kv(hn, w_bf, nq // nkv, nq // nkv + 1, cos_h, sin_h)
    qi = _proj_rope(hn, w_bf, (nq + 2 * nkv) // tn, nqi, cos_i, sin_i, IDX_DIM // 2, BF16)
    tail0 = nq + 2 * nkv + nqi
    w_tail = jnp.pad(w_bf[:, tail0:], ((0, 0), (0, LANES - (IDX_DIM + IDX_HEADS))))
    lane = jnp.arange(LANES)
    wi_scale = (IDX_HEADS ** -0.5) * (IDX_DIM ** -0.5)
    cos_t = jnp.where(lane[None, :] < IDX_DIM, cos_i, wi_scale)
    sin_t = jnp.where(lane[None, :] < IDX_DIM, sin_i, 0.0)
    tail = _proj_rope(hn, w_tail, 0, LANES, cos_t, sin_t, IDX_DIM // 2, F32, tn=LANES)
    ki = tail[:, :IDX_DIM].astype(BF16)
    wit = tail[:, IDX_DIM:IDX_DIM + IDX_HEADS].T
    n_keep = min(TOPK_KEYS, t // 4)
    idx_t, nv = _indexer(qi.T, ki, wit, n_keep)
    att = _gattn(nv.reshape(t), idx_t.T, q.reshape(t * N_HEADS, HEAD_DIM), kv.reshape(t * KV_ROWS, LANES), n_keep)
    att = att.reshape(t, nq)
    return _mm_res(att, w_out.astype(BF16), jnp.zeros((d,), F32), x, g1)


def _conv_layer(x, mod, norm_g, w_pw1, b_pw1, w_dw, b_dw, ln_g, ln_b, w_pw2, b_pw2):
    t, d = x.shape
    sh1, sc1, g1 = mod[0:d], mod[d:2 * d], mod[2 * d:3 * d]
    hn = _normmod(x, norm_g, sc1, sh1)
    u = _glu(hn, w_pw1.astype(BF16), b_pw1)
    v = _dwconv_ln(u, w_dw, b_dw, ln_g, ln_b)
    return _mm_res(v, w_pw2.astype(BF16), b_pw2, x, g1)


def _moe_layer(x, mod, norm_g, w_rg, b_rg, w_re, b_re, w_gate, w_up, w_down, final_g, final_norm):
    t, d = x.shape
    sh2, sc2, g2 = mod[3 * d:4 * d], mod[4 * d:5 * d], mod[5 * d:6 * d]
    h, aux, counts = _route(x, norm_g, sc2, sh2, w_rg, b_rg, w_re, b_re)
    bucket = aux[:, AUX_BUCKET].astype(I32)
    rank = aux[:, AUX_RANK].astype(I32)
    sizes = counts[0, :N_BUCKETS].astype(I32)
    padded = (sizes + MOE_TM - 1) // MOE_TM * MOE_TM
    ends = jnp.cumsum(padded)
    dest = (ends - padded)[bucket] + rank
    n_blocks = t // MOE_TM + N_BUCKETS
    blk_start = jnp.arange(n_blocks, dtype=I32) * MOE_TM
    blk_bucket = jnp.minimum(jnp.searchsorted(ends, blk_start, side='right'), N_BUCKETS - 1).astype(I32)
    live = (blk_start < ends[-1]).astype(I32)
    group0 = (blk_bucket // N_PAIRS) * EXPERTS_PER_GROUP
    e_lo = group0 + jnp.asarray(PAIR_LO, I32)[blk_bucket % N_PAIRS]
    e_hi = group0 + jnp.asarray(PAIR_HI, I32)[blk_bucket % N_PAIRS]
    xs, ws = _dispatch(dest, h, aux, n_blocks * MOE_TM)
    ys = _ffn(e_lo, e_hi, live, xs, ws, w_gate.astype(BF16), w_up.astype(BF16), w_down.astype(BF16))
    return _combine(dest, x, g2, final_g, ys, final_norm)


def kernel(x, c, ada_w, ada_b, norm1_g, norm2_g, attn_w_in, attn_w_out, conv_w_pw1, conv_b_pw1, conv_w_dw, conv_b_dw, conv_ln_g, conv_ln_b, conv_w_pw2, conv_b_pw2, router_w_group, router_b_group, router_w_expert, router_b_expert, exp_w_gate, exp_w_up, exp_w_down, final_g):
    bsz, t, d = x.shape
    assert bsz == 1 and d == N_HEADS * HEAD_DIM
    depth = ada_w.shape[0]
    mods = _ada_mod(c, ada_w, ada_b)
    tables = _rope_tables(t)
    xf = x.reshape(t, d)
    for i in range(depth):
        j = i // 2
        if i % 2 == 0:
            xf = _attention_layer(xf, mods[i], norm1_g[i], attn_w_in[j], attn_w_out[j], tables)
        else:
            xf = _conv_layer(xf, mods[i], norm1_g[i], conv_w_pw1[j], conv_b_pw1[j], conv_w_dw[j], conv_b_dw[j],
                             conv_ln_g[j], conv_ln_b[j], conv_w_pw2[j], conv_b_pw2[j])
        xf = _moe_layer(xf, mods[i], norm2_g[i], router_w_group[i], router_b_group[i], router_w_expert[i],
                        router_b_expert[i], exp_w_gate[i], exp_w_up[i], exp_w_down[i], final_g, i == depth - 1)
    return xf.reshape(bsz, t, d)
```

```python
import functools

import jax
import jax.numpy as jnp
from jax import lax
from jax.experimental import pallas as pl
from jax.experimental.pallas import tpu as pltpu

EPS = 1e-6
CHUNK = 64
N_HEADS = 16
N_KV_GROUPS = 4
HEADS_PER_GROUP = N_HEADS // N_KV_GROUPS
HEAD_DIM = 128
IDX_HEADS = 16
IDX_DIM = 64
TOPK_KEYS = 256
ROPE_THETA = 10000.0
NEG_INF = -1e30
CONV_WIDTH = 31
N_GROUPS = 4
EXPERTS_PER_GROUP = 4
N_EXPERTS = N_GROUPS * EXPERTS_PER_GROUP

LANES = 128
KV_ROWS = N_KV_GROUPS
INT_MIN = -(2 ** 31)
INVALID_SHIFT = 1 << 30
VMEM_LIMIT = 56 * 1024 * 1024

F32 = jnp.float32
BF16 = jnp.bfloat16
I32 = jnp.int32


def _cparams(sem):
    return pltpu.CompilerParams(dimension_semantics=sem, vmem_limit_bytes=VMEM_LIMIT)


def _ada_kernel(c_ref, w_ref, b_ref, o_ref):
    c = c_ref[...]
    ca = c * jax.nn.sigmoid(c)
    o_ref[0] = jnp.dot(ca, w_ref[0], preferred_element_type=F32, precision=lax.Precision.HIGHEST) + b_ref[0]


def _ada_mod(c, ada_w, ada_b):
    depth, d, n = ada_w.shape
    tn = 768
    c8 = jnp.zeros((8, d), F32).at[0].set(c[0])
    out = pl.pallas_call(
        _ada_kernel,
        grid=(depth, n // tn),
        in_specs=[pl.BlockSpec((8, d), lambda i, j: (0, 0)),
                  pl.BlockSpec((1, d, tn), lambda i, j: (i, 0, j)),
                  pl.BlockSpec((1, 1, tn), lambda i, j: (i, 0, j))],
        out_specs=pl.BlockSpec((1, 8, tn), lambda i, j: (i, 0, j)),
        out_shape=jax.ShapeDtypeStruct((depth, 8, n), F32),
        compiler_params=_cparams(("arbitrary", "arbitrary")),
        name="ada_mod",
    )(c8, ada_w, ada_b.reshape(depth, 1, n))
    return out[:, 0, :]


def _normmod_kernel(x_ref, g_ref, sc_ref, sh_ref, o_ref):
    x = x_ref[...]
    xn = x * lax.rsqrt(jnp.mean(x * x, axis=-1, keepdims=True) + EPS)
    o_ref[...] = ((xn * g_ref[...]) * (1.0 + sc_ref[...]) + sh_ref[...]).astype(o_ref.dtype)


def _normmod(x, g, sc, sh, tm=512):
    t, d = x.shape
    vec = pl.BlockSpec((1, d), lambda i: (0, 0))
    return pl.pallas_call(
        _normmod_kernel,
        grid=(t // tm,),
        in_specs=[pl.BlockSpec((tm, d), lambda i: (i, 0)), vec, vec, vec],
        out_specs=pl.BlockSpec((tm, d), lambda i: (i, 0)),
        out_shape=jax.ShapeDtypeStruct((t, d), BF16),
        compiler_params=_cparams(("arbitrary",)),
        name="normmod",
    )(x, g.reshape(1, d), sc.reshape(1, d), sh.reshape(1, d))


N_PAIRS = EXPERTS_PER_GROUP * (EXPERTS_PER_GROUP - 1) // 2
N_BUCKETS = N_GROUPS * N_PAIRS
PAIR_LO = (0, 0, 0, 1, 1, 2)
PAIR_HI = (1, 2, 3, 2, 3, 3)
MOE_TM = 256
AUX_W_LO, AUX_W_HI, AUX_BUCKET, AUX_RANK = 0, 1, 2, 3


def _route_kernel(x_ref, g_ref, sc_ref, sh_ref, wr_ref, br_ref, tri_ref, h_ref, aux_ref, cnt_ref, run_ref):
    @pl.when(pl.program_id(0) == 0)
    def _():
        run_ref[...] = jnp.zeros_like(run_ref)

    x = x_ref[...]
    xn = x * lax.rsqrt(jnp.mean(x * x, axis=-1, keepdims=True) + EPS)
    h = (xn * g_ref[...]) * (1.0 + sc_ref[...]) + sh_ref[...]
    h_ref[...] = h.astype(h_ref.dtype)
    h_hi = h.astype(BF16)
    h_lo = (h - h_hi.astype(F32)).astype(BF16)
    w = wr_ref[...]
    w_hi = w.astype(BF16)
    w_lo = (w - w_hi.astype(F32)).astype(BF16)
    logits = (jnp.dot(h_hi, w_hi, preferred_element_type=F32) + jnp.dot(h_lo, w_hi, preferred_element_type=F32)
              + jnp.dot(h_hi, w_lo, preferred_element_type=F32)) + br_ref[...]
    tm = x.shape[0]
    lane4 = lax.broadcasted_iota(I32, (tm, N_GROUPS), 1)

    def first_argmax(v):
        m = jnp.max(v, axis=-1, keepdims=True)
        return m, jnp.min(jnp.where(v == m, lane4, N_GROUPS), axis=-1, keepdims=True)

    gl = logits[:, 0:N_GROUPS]
    gmax, gidx = first_argmax(gl)
    g_w = 1.0 / jnp.sum(jnp.exp(gl - gmax), axis=-1, keepdims=True)
    el = jnp.zeros((tm, EXPERTS_PER_GROUP), F32)
    for g in range(N_GROUPS):
        lo = N_GROUPS + g * EXPERTS_PER_GROUP
        el = jnp.where(gidx == g, logits[:, lo:lo + EXPERTS_PER_GROUP], el)
    emax = jnp.max(el, axis=-1, keepdims=True)
    ee = jnp.exp(el - emax)
    ep = ee / jnp.sum(ee, axis=-1, keepdims=True)
    p1, i1 = first_argmax(ep)
    rest = jnp.where(lane4 == i1, -1.0, ep)
    p2, i2 = first_argmax(rest)
    den = p1 + p2
    w1 = g_w * p1 / den
    w2 = g_w * p2 / den
    first_lo = i1 < i2
    a = jnp.minimum(i1, i2)
    b = jnp.maximum(i1, i2)
    w_lo = jnp.where(first_lo, w1, w2)
    w_hi = jnp.where(first_lo, w2, w1)
    pair = jnp.where(a == 0, b - 1, jnp.where(a == 1, b + 1, N_PAIRS - 1))
    bucket = gidx * N_PAIRS + pair
    lane = lax.broadcasted_iota(I32, (tm, LANES), 1)
    onehot = lane == bucket
    prefix = jnp.dot(tri_ref[...], onehot.astype(F32).astype(BF16), preferred_element_type=F32)
    run = run_ref[...]
    rank = jnp.sum(jnp.where(onehot, prefix + run, 0.0), axis=1, keepdims=True) - 1.0
    run = run + prefix[tm - 1:tm, :]
    run_ref[...] = run
    cnt_ref[...] = run
    aux_ref[...] = (jnp.where(lane == AUX_W_LO, w_lo, 0.0) + jnp.where(lane == AUX_W_HI, w_hi, 0.0)
                    + jnp.where(lane == AUX_BUCKET, bucket.astype(F32), 0.0) + jnp.where(lane == AUX_RANK, rank, 0.0))


def _route(x, g, sc, sh, w_rg, b_rg, w_re, b_re, tm=512):
    t, d = x.shape
    w_all = jnp.concatenate([w_rg, jnp.moveaxis(w_re, 0, 1).reshape(d, N_EXPERTS)], axis=1)
    b_all = jnp.concatenate([b_rg, b_re.reshape(N_EXPERTS)])
    npad = LANES - w_all.shape[1]
    w_all = jnp.pad(w_all, ((0, 0), (0, npad)))
    b_all = jnp.pad(b_all, (0, npad)).reshape(1, LANES)
    vec = pl.BlockSpec((1, d), lambda i: (0, 0))
    r = lax.broadcasted_iota(I32, (tm, tm), 0)
    c = lax.broadcasted_iota(I32, (tm, tm), 1)
    tri = (c <= r).astype(BF16)
    return pl.pallas_call(
        _route_kernel,
        grid=(t // tm,),
        in_specs=[pl.BlockSpec((tm, d), lambda i: (i, 0)), vec, vec, vec,
                  pl.BlockSpec((d, LANES), lambda i: (0, 0)), pl.BlockSpec((1, LANES), lambda i: (0, 0)),
                  pl.BlockSpec((tm, tm), lambda i: (0, 0))],
        out_specs=[pl.BlockSpec((tm, d), lambda i: (i, 0)), pl.BlockSpec((tm, LANES), lambda i: (i, 0)),
                   pl.BlockSpec((1, LANES), lambda i: (0, 0))],
        out_shape=[jax.ShapeDtypeStruct((t, d), F32), jax.ShapeDtypeStruct((t, LANES), F32),
                   jax.ShapeDtypeStruct((1, LANES), F32)],
        scratch_shapes=[pltpu.VMEM((1, LANES), F32)],
        compiler_params=_cparams(("arbitrary",)),
        name="route",
    )(x, g.reshape(1, d), sc.reshape(1, d), sh.reshape(1, d), w_all, b_all, tri)


def _rope_block(blk, cos, sin, half):
    if half == HEAD_DIM // 2:
        partner = pltpu.roll(blk, HEAD_DIM // 2, 1)
    else:
        lane = lax.broadcasted_iota(I32, blk.shape, 1)
        first = (lane % IDX_DIM) < (IDX_DIM // 2)
        partner = jnp.where(first, pltpu.roll(blk, LANES - IDX_DIM // 2, 1), pltpu.roll(blk, IDX_DIM // 2, 1))
    return blk * cos + partner * sin


def _proj_rope_kernel(a_ref, w_ref, cos_ref, sin_ref, o_ref, *, half):
    acc = jnp.dot(a_ref[...], w_ref[...], preferred_element_type=F32)
    cos = cos_ref[...]
    sin = sin_ref[...]
    for h in range(acc.shape[1] // LANES):
        sl = slice(h * LANES, (h + 1) * LANES)
        o_ref[:, sl] = _rope_block(acc[:, sl], cos, sin, half).astype(o_ref.dtype)


def _proj_rope(a, w, col_block0, n_out, cos, sin, half, out_dtype, tm=512, tn=512):
    t, d = a.shape
    return pl.pallas_call(
        functools.partial(_proj_rope_kernel, half=half),
        grid=(t // tm, n_out // tn),
        in_specs=[pl.BlockSpec((tm, d), lambda i, j: (i, 0)),
                  pl.BlockSpec((d, tn), lambda i, j: (0, col_block0 + j)),
                  pl.BlockSpec((tm, LANES), lambda i, j: (i, 0)),
                  pl.BlockSpec((tm, LANES), lambda i, j: (i, 0))],
        out_specs=pl.BlockSpec((tm, tn), lambda i, j: (i, j)),
        out_shape=jax.ShapeDtypeStruct((t, n_out), out_dtype),
        compiler_params=_cparams(("arbitrary", "arbitrary")),
        name="proj_rope",
    )(a, w, cos, sin)


def _proj_kv_kernel(a_ref, wk_ref, wv_ref, cos_ref, sin_ref, o_ref):
    a = a_ref[...]
    k = jnp.dot(a, wk_ref[...], preferred_element_type=F32)
    v = jnp.dot(a, wv_ref[...], preferred_element_type=F32)
    cos = cos_ref[...]
    sin = sin_ref[...]
    for h in range(k.shape[1] // LANES):
        sl = slice(h * LANES, (h + 1) * LANES)
        kr = _rope_block(k[:, sl], cos, sin, HEAD_DIM // 2)
        kb = lax.bitcast_convert_type(kr.astype(BF16).astype(F32), I32)
        vb = lax.bitcast_convert_type(v[:, sl].astype(BF16).astype(F32), I32)
        o_ref[:, sl] = kb | lax.shift_right_logical(vb, 16)


def _proj_kv(a, w, kblock, vblock, cos, sin, tm=512):
    t, d = a.shape
    n = N_KV_GROUPS * HEAD_DIM
    return pl.pallas_call(
        _proj_kv_kernel,
        grid=(t // tm,),
        in_specs=[pl.BlockSpec((tm, d), lambda i: (i, 0)),
                  pl.BlockSpec((d, n), lambda i: (0, kblock)),
                  pl.BlockSpec((d, n), lambda i: (0, vblock)),
                  pl.BlockSpec((tm, LANES), lambda i: (i, 0)),
                  pl.BlockSpec((tm, LANES), lambda i: (i, 0))],
        out_specs=pl.BlockSpec((tm, n), lambda i: (i, 0)),
        out_shape=jax.ShapeDtypeStruct((t, n), I32),
        compiler_params=_cparams(("arbitrary",)),
        name="proj_kv",
    )(a, w, w, cos, sin)


QB = 256
TK = 512
RC = 64


def _indexer_kernel(qit_ref, ki_ref, wit_ref, tri_ref, idx_ref, nv_ref, d_ref, gm_ref, *, n_keep):
    b = pl.program_id(0)
    q0 = b * QB
    n_t = (q0 + QB + TK - 1) // TK
    n_rows = n_t * TK
    rowi = lax.broadcasted_iota(I32, (TK, QB), 0)
    qchunk = (q0 + lax.broadcasted_iota(I32, (TK, QB), 1)) // CHUNK

    def admissible(k0):
        return ((k0 + rowi) // CHUNK) <= qchunk

    def score_tile(kt, carry):
        k0 = pl.multiple_of(kt * TK, TK)
        ki_t = ki_ref[pl.ds(k0, TK), :]
        acc = jnp.zeros((TK, QB), F32)
        for h in range(IDX_HEADS):
            a = jnp.dot(ki_t, qit_ref[h * IDX_DIM:(h + 1) * IDX_DIM, :], preferred_element_type=F32)
            acc = acc + jnp.maximum(a, 0.0) * wit_ref[h:h + 1, :]
        s = jnp.where(admissible(k0), acc + 0.0, NEG_INF)
        u = lax.bitcast_convert_type(s, I32)
        key = jnp.where(u < 0, u ^ 0x7FFFFFFF, u)
        d_ref[pl.ds(k0, TK), :] = key
        gm = gm_ref[...]
        for c in range(TK // groups):
            gm = jnp.maximum(gm, key[c * groups:(c + 1) * groups, :])
        gm_ref[...] = gm
        return carry

    groups = gm_ref.shape[0]
    gm_ref[...] = jnp.full(gm_ref.shape, INT_MIN, I32)
    lax.fori_loop(0, n_t, score_tile, 0)
    gm = gm_ref[...]
    lo_u = jnp.min(gm, axis=0, keepdims=True) ^ INT_MIN
    hi_u = jnp.max(gm, axis=0, keepdims=True) ^ INT_MIN
    spread = jnp.max((lo_u ^ hi_u) ^ INT_MIN) ^ INT_MIN

    def count_where(pred):
        def body(kt, acc):
            k0 = pl.multiple_of(kt * TK, TK)
            hit = pred(d_ref[pl.ds(k0, TK), :]).astype(I32)
            return acc + jnp.sum(hit.reshape(TK // 8, 8, QB), axis=0)
        acc = lax.fori_loop(0, n_t, body, jnp.zeros((8, QB), I32))
        return jnp.sum(acc, axis=0, keepdims=True)

    def select_bit(it, carry):
        v_u, cnt_v = carry
        bit = jnp.left_shift(jnp.int32(1), 31 - it)
        cand_u = v_u | bit

        def copy_bit():
            return jnp.where((lo_u & bit) != 0, cand_u, v_u), cnt_v

        def search_bit():
            cand = cand_u ^ INT_MIN
            cnt = count_where(lambda key: key >= cand)
            keep = cnt >= n_keep
            return jnp.where(keep, cand_u, v_u), jnp.where(keep, cnt, cnt_v)

        return lax.cond(lax.shift_right_logical(spread, 31 - it) == 0, copy_bit, search_bit)

    thr_u, cnt_thr = lax.fori_loop(0, 32, select_bit,
                                   (jnp.zeros((1, QB), I32), jnp.full((1, QB), n_keep + 1, I32)))
    thr = thr_u ^ INT_MIN

    tri = tri_ref[...]
    zero = jnp.zeros((1, QB), F32)

    def place(k0, sel, run_sel):
        pre_sel = jnp.dot(tri, sel.astype(F32).astype(BF16), preferred_element_type=F32)
        rank = (run_sel + pre_sel).astype(I32) - 1
        d_ref[pl.ds(k0, TK), :] = jnp.where(sel, (k0 + rowi) - rank, INVALID_SHIFT)
        return run_sel + pre_sel[TK - 1:TK, :]

    def rank_exact_count():
        def tile(kt, run_sel):
            k0 = pl.multiple_of(kt * TK, TK)
            return place(k0, admissible(k0) & (d_ref[pl.ds(k0, TK), :] >= thr), run_sel)
        return lax.fori_loop(0, n_t, tile, zero)

    def rank_with_ties():
        need = (n_keep - count_where(lambda key: key > thr)).astype(F32)

        def tile(kt, carry):
            run_eq, run_sel = carry
            k0 = pl.multiple_of(kt * TK, TK)
            key = d_ref[pl.ds(k0, TK), :]
            eq = key == thr
            eq_f = eq.astype(F32)
            pre_eq = jnp.dot(tri, eq_f.astype(BF16), preferred_element_type=F32)
            sel = admissible(k0) & ((key > thr) | (eq & ((run_eq + pre_eq - eq_f) < need)))
            return run_eq + pre_eq[TK - 1:TK, :], place(k0, sel, run_sel)
        return lax.fori_loop(0, n_t, tile, (zero, zero))[1]

    n_sel = lax.cond(jnp.max(jnp.abs(cnt_thr - n_keep)) == 0, rank_exact_count, rank_with_ties)

    d_ref[pl.ds(pl.multiple_of(n_rows, RC), RC), :] = jnp.full((RC, QB), INVALID_SHIFT, I32)
    n_chunks = n_rows // RC
    win_chunks = -(-n_keep // RC)
    window = win_chunks * RC
    n_stages = (d_ref.shape[0] - RC - 1).bit_length()
    for s in range(n_stages):
        sh = 1 << s
        test = sh | INVALID_SHIFT

        def merge(own, inc, sh=sh, test=test):
            take = (inc & test) == sh
            stay = (own & test) == 0
            return jnp.where(take, inc, jnp.where(stay, own, INVALID_SHIFT))

        def chunk_with_source(c, carry, sh=sh, merge=merge):
            r0 = pl.multiple_of(c * RC, RC)
            if sh < 8:
                x = d_ref[pl.ds(r0, RC + 8), :]
                own, inc = x[0:RC], x[sh:sh + RC]
            else:
                own = d_ref[pl.ds(r0, RC), :]
                inc = d_ref[pl.ds(pl.multiple_of(r0 + sh, 8), RC), :]
            d_ref[pl.ds(r0, RC), :] = merge(own, inc)
            return carry

        def chunk_without_source(c, carry, merge=merge):
            r0 = pl.multiple_of(c * RC, RC)
            own = d_ref[pl.ds(r0, RC), :]
            d_ref[pl.ds(r0, RC), :] = merge(own, jnp.full_like(own, INVALID_SHIFT))
            return carry

        def window_chunk(c, carry, sh=sh, merge=merge):
            r0 = pl.multiple_of((c // win_chunks) * sh + (c % win_chunks) * RC, RC)
            src = r0 + sh
            own = d_ref[pl.ds(r0, RC), :]
            inc = d_ref[pl.ds(pl.multiple_of(jnp.minimum(src, n_rows), RC), RC), :]
            d_ref[pl.ds(r0, RC), :] = merge(own, inc)
            return carry

        if sh < RC:
            lax.fori_loop(0, n_chunks, chunk_with_source, 0)
        elif sh < 2 * window:
            n_src = jnp.maximum(n_rows - sh, 0) // RC
            lax.fori_loop(0, n_src, chunk_with_source, 0)
            lax.fori_loop(n_src, n_chunks, chunk_without_source, 0)
        else:
            lax.fori_loop(0, ((n_rows + sh - 1) // sh) * win_chunks, window_chunk, 0)

    slot = lax.broadcasted_iota(I32, (n_keep, QB), 0)
    dd = d_ref[0:n_keep, :]
    idx_ref[...] = jnp.where(dd < INVALID_SHIFT, (slot + dd) * KV_ROWS, 0)
    nv_ref[...] = n_sel.astype(I32)


def _indexer(qit, ki, wit, n_keep):
    t = ki.shape[0]
    r = lax.broadcasted_iota(I32, (TK, TK), 0)
    c = lax.broadcasted_iota(I32, (TK, TK), 1)
    tri = (c <= r).astype(BF16)
    return pl.pallas_call(
        functools.partial(_indexer_kernel, n_keep=n_keep),
        grid=(t // QB,),
        in_specs=[pl.BlockSpec((IDX_HEADS * IDX_DIM, QB), lambda b: (0, b)),
                  pl.BlockSpec((t, IDX_DIM), lambda b: (0, 0)),
                  pl.BlockSpec((IDX_HEADS, QB), lambda b: (0, b)),
                  pl.BlockSpec((TK, TK), lambda b: (0, 0))],
        out_specs=[pl.BlockSpec((n_keep, QB), lambda b: (0, b)), pl.BlockSpec((1, QB), lambda b: (0, b))],
        out_shape=[jax.ShapeDtypeStruct((n_keep, t), I32), jax.ShapeDtypeStruct((1, t), I32)],
        scratch_shapes=[pltpu.VMEM((t + RC, QB), I32), pltpu.VMEM((-(-n_keep // RC) * RC, QB), I32)],
        compiler_params=_cparams(("arbitrary",)),
        name="indexer",
    )(qit, ki, wit, tri)


GQ = 64
GATTN_UNROLL = 8


def _gattn_kernel(nv_ref, idx_ref, q_ref, kv_hbm, o_ref, kv_vmem, stage_a, stage_b, stage_c, stage_d,
                  s_a, s_b, row_lists, sem_kv, sem_rows, *, n_keep):
    i = pl.program_id(0)

    @pl.when(i == 0)
    def _():
        cp = pltpu.make_async_copy(kv_hbm, kv_vmem, sem_kv)
        cp.start()
        cp.wait()

    cp = pltpu.make_async_copy(idx_ref, row_lists, sem_rows)
    cp.start()
    cp.wait()

    n_col = n_keep * KV_ROWS
    colg = lax.broadcasted_iota(I32, (N_HEADS, n_col), 1)
    rowh = lax.broadcasted_iota(I32, (N_HEADS, n_col), 0)
    own_group = (colg % KV_ROWS) == (rowh // HEADS_PER_GROUP)
    col_slot = colg // KV_ROWS
    scale = HEAD_DIM ** -0.5

    last = GQ - 1
    stages = (stage_a, stage_b, stage_c, stage_d)
    s_bufs = (s_a, s_b)

    def gather(r, stage):
        rows = row_lists.at[pl.ds(jnp.minimum(r, last), 1)]
        for j in range(n_keep):
            row = pl.multiple_of(rows[0, j], KV_ROWS)
            stage[j * KV_ROWS:(j + 1) * KV_ROWS, :] = kv_vmem[pl.ds(row, KV_ROWS), :]

    def scores(r, stage, s_buf):
        k_all = lax.bitcast_convert_type(stage[...] & jnp.int32(-65536), F32).astype(BF16)
        qh = q_ref[pl.ds(pl.multiple_of(r * N_HEADS, N_HEADS), N_HEADS), :]
        s = lax.dot_general(qh, k_all, (((1,), (1,)), ((), ())), preferred_element_type=F32) * scale
        mask = own_group & (col_slot < nv_ref[i * GQ + r])
        s_buf[...] = jnp.where(mask, s, NEG_INF)

    def combine(r, stage, s_buf):
        s = s_buf[...]
        p = jnp.exp(s - jnp.max(s, axis=1, keepdims=True))
        l = jnp.sum(p, axis=1, keepdims=True)
        v_all = lax.bitcast_convert_type(jnp.left_shift(stage[...], 16), F32).astype(BF16)
        o = jnp.dot(p.astype(BF16), v_all, preferred_element_type=F32) / l
        o_ref[pl.ds(pl.multiple_of(r * N_HEADS, N_HEADS), N_HEADS), :] = o.astype(o_ref.dtype)

    gather(0, stages[0])
    gather(1, stages[1])
    scores(0, stages[0], s_bufs[0])

    def query_group(it, carry):
        r = GATTN_UNROLL * it
        for u in range(GATTN_UNROLL):
            gather(r + u + 2, stages[(u + 2) % 4])
            scores(jnp.minimum(r + u + 1, last), stages[(u + 1) % 4], s_bufs[(u + 1) % 2])
            combine(r + u, stages[u % 4], s_bufs[u % 2])
        return carry

    lax.fori_loop(0, GQ // GATTN_UNROLL, query_group, 0)


def _gattn(nv, idx, q2, kv2, n_keep):
    t = nv.shape[0]
    grid_spec = pltpu.PrefetchScalarGridSpec(
        num_scalar_prefetch=1,
        grid=(t // GQ,),
        in_specs=[pl.BlockSpec((GQ, n_keep), lambda i, nv: (i, 0)),
                  pl.BlockSpec((GQ * N_HEADS, HEAD_DIM), lambda i, nv: (i, 0)),
                  pl.BlockSpec(memory_space=pl.ANY)],
        out_specs=pl.BlockSpec((GQ * N_HEADS, HEAD_DIM), lambda i, nv: (i, 0)),
        scratch_shapes=[pltpu.VMEM(kv2.shape, I32),
                        pltpu.VMEM((n_keep * KV_ROWS, LANES), I32),
                        pltpu.VMEM((n_keep * KV_ROWS, LANES), I32),
                        pltpu.VMEM((n_keep * KV_ROWS, LANES), I32),
                        pltpu.VMEM((n_keep * KV_ROWS, LANES), I32),
                        pltpu.VMEM((N_HEADS, n_keep * KV_ROWS), F32),
                        pltpu.VMEM((N_HEADS, n_keep * KV_ROWS), F32),
                        pltpu.SMEM((GQ, n_keep), I32),
                        pltpu.SemaphoreType.DMA,
                        pltpu.SemaphoreType.DMA],
    )
    return pl.pallas_call(
        functools.partial(_gattn_kernel, n_keep=n_keep),
        grid_spec=grid_spec,
        out_shape=jax.ShapeDtypeStruct((t * N_HEADS, HEAD_DIM), BF16),
        compiler_params=_cparams(("arbitrary",)),
        name="gattn",
    )(nv, idx, q2, kv2)


def _mm_res_kernel(a_ref, w_ref, b_ref, x_ref, g_ref, o_ref):
    y = jnp.dot(a_ref[...], w_ref[...], preferred_element_type=F32) + b_ref[...]
    o_ref[...] = x_ref[...] + g_ref[...] * y


def _mm_res(a, w, bias, x, gate, tm=512, tn=512):
    t, k = a.shape
    n = w.shape[1]
    return pl.pallas_call(
        _mm_res_kernel,
        grid=(t // tm, n // tn),
        in_specs=[pl.BlockSpec((tm, k), lambda i, j: (i, 0)),
                  pl.BlockSpec((k, tn), lambda i, j: (0, j)),
                  pl.BlockSpec((1, tn), lambda i, j: (0, j)),
                  pl.BlockSpec((tm, tn), lambda i, j: (i, j)),
                  pl.BlockSpec((1, tn), lambda i, j: (0, j))],
        out_specs=pl.BlockSpec((tm, tn), lambda i, j: (i, j)),
        out_shape=jax.ShapeDtypeStruct((t, n), F32),
        compiler_params=_cparams(("arbitrary", "arbitrary")),
        name="mm_res",
    )(a, w, bias.reshape(1, n), x, gate.reshape(1, n))


def _glu_kernel(a_ref, wa_ref, wg_ref, ba_ref, bg_ref, o_ref):
    a = a_ref[...]
    ya = jnp.dot(a, wa_ref[...], preferred_element_type=F32) + ba_ref[...]
    yg = jnp.dot(a, wg_ref[...], preferred_element_type=F32) + bg_ref[...]
    o_ref[...] = ya * jax.nn.sigmoid(yg)


def _glu(a, w, b, tm=512, tn=512):
    t, k = a.shape
    n = w.shape[1] // 2
    nb = n // tn
    b2 = b.reshape(1, 2 * n)
    return pl.pallas_call(
        _glu_kernel,
        grid=(t // tm, nb),
        in_specs=[pl.BlockSpec((tm, k), lambda i, j: (i, 0)),
                  pl.BlockSpec((k, tn), lambda i, j: (0, j)),
                  pl.BlockSpec((k, tn), lambda i, j: (0, j + nb)),
                  pl.BlockSpec((1, tn), lambda i, j: (0, j)),
                  pl.BlockSpec((1, tn), lambda i, j: (0, j + nb))],
        out_specs=pl.BlockSpec((tm, tn), lambda i, j: (i, j)),
        out_shape=jax.ShapeDtypeStruct((t, n), F32),
        compiler_params=_cparams(("arbitrary", "arbitrary")),
        name="glu",
    )(a, w, w, b2, b2)


HALO = 32


SUBLANES = 8
CONV_ROWS = 32
CONV_COLS = 512


def _dwconv_ln_kernel(u_ref, prev_ref, w_ref, b_ref, g_ref, beta_ref, o_ref, ext_ref, sh_ref, conv_ref):
    i = pl.program_id(0)
    tm, d = u_ref.shape
    ext_ref[0:HALO, :] = jnp.where(i == 0, 0.0, prev_ref[...])
    ext_ref[HALO:HALO + tm, :] = u_ref[...]
    for phase in range(SUBLANES):
        sh_ref[phase, 0:tm + HALO - phase, :] = ext_ref[phase:tm + HALO, :]
    first = HALO - (CONV_WIDTH - 1)

    def conv_rows(c, carry):
        r0 = pl.multiple_of(c * CONV_ROWS, CONV_ROWS)
        for cb in range(d // CONV_COLS):
            cols = slice(cb * CONV_COLS, (cb + 1) * CONV_COLS)
            acc = jnp.zeros((CONV_ROWS, CONV_COLS), F32) + b_ref[:, cols]
            for k in range(CONV_WIDTH):
                phase = (first + k) % SUBLANES
                start = pl.multiple_of(r0 + (first + k - phase), SUBLANES)
                acc = acc + sh_ref[phase, pl.ds(start, CONV_ROWS), cols] * w_ref[k:k + 1, cols]
            conv_ref[pl.ds(r0, CONV_ROWS), cols] = acc
        return carry

    lax.fori_loop(0, tm // CONV_ROWS, conv_rows, 0)
    acc = conv_ref[...]
    mu = jnp.mean(acc, axis=-1, keepdims=True)
    cen = acc - mu
    var = jnp.mean(cen * cen, axis=-1, keepdims=True)
    y = (cen * lax.rsqrt(var + EPS)) * g_ref[...] + beta_ref[...]
    o_ref[...] = (y * jax.nn.sigmoid(y)).astype(o_ref.dtype)


def _dwconv_ln(u, w_dw, b_dw, ln_g, ln_b, tm=256):
    t, d = u.shape
    ratio = tm // HALO
    vec = pl.BlockSpec((1, d), lambda i: (0, 0))
    return pl.pallas_call(
        _dwconv_ln_kernel,
        grid=(t // tm,),
        in_specs=[pl.BlockSpec((tm, d), lambda i: (i, 0)),
                  pl.BlockSpec((HALO, d), lambda i: (jnp.maximum(i * ratio - 1, 0), 0)),
                  pl.BlockSpec((HALO, d), lambda i: (0, 0)),
                  vec, vec, vec],
        out_specs=pl.BlockSpec((tm, d), lambda i: (i, 0)),
        out_shape=jax.ShapeDtypeStruct((t, d), BF16),
        scratch_shapes=[pltpu.VMEM((HALO + tm, d), F32), pltpu.VMEM((SUBLANES, HALO + tm, d), F32),
                        pltpu.VMEM((tm, d), F32)],
        compiler_params=_cparams(("arbitrary",)),
        name="dwconv_ln",
    )(u, u, jnp.pad(w_dw, ((0, HALO - CONV_WIDTH), (0, 0))), b_dw.reshape(1, d), ln_g.reshape(1, d), ln_b.reshape(1, d))


def _row_copy(src, src_row, dst, dst_row, sem):
    return pltpu.make_async_copy(src.at[pl.ds(src_row, 1), :], dst.at[pl.ds(dst_row, 1), :], sem)


ROW_UNROLL = 8


def _for_rows(n, fn):
    def group(gi, carry):
        for u in range(ROW_UNROLL):
            fn(gi * ROW_UNROLL + u, carry)
        return carry

    lax.fori_loop(0, n // ROW_UNROLL, group, 0)


def _dispatch_kernel(dest_ref, h_ref, aux_ref, xs_in, ws_in, xs_hbm, ws_hbm, sem_x, sem_w):
    del xs_in, ws_in
    tm = h_ref.shape[0]
    t0 = pl.program_id(0) * tm

    def issue(r, carry):
        slot = dest_ref[t0 + r]
        _row_copy(h_ref, r, xs_hbm, slot, sem_x).start()
        _row_copy(aux_ref, r, ws_hbm, slot, sem_w).start()
        return carry

    def drain(r, carry):
        _row_copy(h_ref, r, xs_hbm, 0, sem_x).wait()
        _row_copy(aux_ref, r, ws_hbm, 0, sem_w).wait()
        return carry

    _for_rows(tm, issue)
    _for_rows(tm, drain)


def _dispatch(dest, h, aux, n_rows, tm=256):
    t, d = h.shape
    grid_spec = pltpu.PrefetchScalarGridSpec(
        num_scalar_prefetch=1,
        grid=(t // tm,),
        in_specs=[pl.BlockSpec((tm, d), lambda i, dest: (i, 0)),
                  pl.BlockSpec((tm, LANES), lambda i, dest: (i, 0)),
                  pl.BlockSpec(memory_space=pl.ANY),
                  pl.BlockSpec(memory_space=pl.ANY)],
        out_specs=[pl.BlockSpec(memory_space=pl.ANY), pl.BlockSpec(memory_space=pl.ANY)],
        scratch_shapes=[pltpu.SemaphoreType.DMA, pltpu.SemaphoreType.DMA],
    )
    return pl.pallas_call(
        _dispatch_kernel,
        grid_spec=grid_spec,
        out_shape=[jax.ShapeDtypeStruct((n_rows, d), F32), jax.ShapeDtypeStruct((n_rows, LANES), F32)],
        input_output_aliases={3: 0, 4: 1},
        compiler_params=_cparams(("arbitrary",)),
        name="dispatch",
    )(dest, h, aux, jnp.zeros((n_rows, d), F32), jnp.zeros((n_rows, LANES), F32))


def _ffn_kernel(elo_ref, ehi_ref, live_ref, xs_ref, ws_ref, wg_lo, wu_lo, wd_lo, wg_hi, wu_hi, wd_hi, ys_ref):
    b = pl.program_id(0)

    @pl.when(live_ref[b] == 1)
    def _():
        x = xs_ref[...].astype(BF16)
        ws = ws_ref[...]

        def expert(wg, wu, wd):
            gate = jnp.dot(x, wg[0], preferred_element_type=F32)
            up = jnp.dot(x, wu[0], preferred_element_type=F32)
            hid = (gate * jax.nn.sigmoid(gate)) * up
            return jnp.dot(hid.astype(BF16), wd[0], preferred_element_type=F32)

        ys_ref[...] = (expert(wg_lo, wu_lo, wd_lo) * ws[:, AUX_W_LO:AUX_W_LO + 1]
                       + expert(wg_hi, wu_hi, wd_hi) * ws[:, AUX_W_HI:AUX_W_HI + 1])

    @pl.when(live_ref[b] == 0)
    def _():
        ys_ref[...] = jnp.zeros_like(ys_ref)


def _ffn(e_lo, e_hi, live, xs, ws, w_gate, w_up, w_down):
    n_rows, d = xs.shape
    _, _, de = w_gate.shape
    lo_in = pl.BlockSpec((1, d, de), lambda b, lo, hi, live: (lo[b], 0, 0))
    hi_in = pl.BlockSpec((1, d, de), lambda b, lo, hi, live: (hi[b], 0, 0))
    lo_out = pl.BlockSpec((1, de, d), lambda b, lo, hi, live: (lo[b], 0, 0))
    hi_out = pl.BlockSpec((1, de, d), lambda b, lo, hi, live: (hi[b], 0, 0))
    grid_spec = pltpu.PrefetchScalarGridSpec(
        num_scalar_prefetch=3,
        grid=(n_rows // MOE_TM,),
        in_specs=[pl.BlockSpec((MOE_TM, d), lambda b, lo, hi, live: (b, 0)),
                  pl.BlockSpec((MOE_TM, LANES), lambda b, lo, hi, live: (b, 0)),
                  lo_in, lo_in, lo_out, hi_in, hi_in, hi_out],
        out_specs=pl.BlockSpec((MOE_TM, d), lambda b, lo, hi, live: (b, 0)),
    )
    return pl.pallas_call(
        _ffn_kernel,
        grid_spec=grid_spec,
        out_shape=jax.ShapeDtypeStruct((n_rows, d), F32),
        compiler_params=_cparams(("arbitrary",)),
        name="ffn",
    )(e_lo, e_hi, live, xs, ws, w_gate, w_up, w_down, w_gate, w_up, w_down)


def _combine_kernel(dest_ref, x_ref, g_ref, fg_ref, ys_hbm, o_ref, buf, sem, *, final_norm):
    tm = x_ref.shape[0]
    t0 = pl.program_id(0) * tm

    def issue(r, carry):
        _row_copy(ys_hbm, dest_ref[t0 + r], buf, r, sem).start()
        return carry

    def drain(r, carry):
        _row_copy(ys_hbm, 0, buf, r, sem).wait()
        return carry

    _for_rows(tm, issue)
    _for_rows(tm, drain)
    y = x_ref[...] + g_ref[...] * buf[...]
    if final_norm:
        y = (y * lax.rsqrt(jnp.mean(y * y, axis=-1, keepdims=True) + EPS)) * fg_ref[...]
    o_ref[...] = y


def _combine(dest, x, gate, final_g, ys, final_norm, tm=256):
    t, d = x.shape
    vec = pl.BlockSpec((1, d), lambda i, dest: (0, 0))
    grid_spec = pltpu.PrefetchScalarGridSpec(
        num_scalar_prefetch=1,
        grid=(t // tm,),
        in_specs=[pl.BlockSpec((tm, d), lambda i, dest: (i, 0)), vec, vec, pl.BlockSpec(memory_space=pl.ANY)],
        out_specs=pl.BlockSpec((tm, d), lambda i, dest: (i, 0)),
        scratch_shapes=[pltpu.VMEM((tm, d), F32), pltpu.SemaphoreType.DMA],
    )
    return pl.pallas_call(
        functools.partial(_combine_kernel, final_norm=final_norm),
        grid_spec=grid_spec,
        out_shape=jax.ShapeDtypeStruct((t, d), F32),
        compiler_params=_cparams(("arbitrary",)),
        name="combine",
    )(dest, x, gate.reshape(1, d), final_g.reshape(1, d), ys)


def _rope_tables(t):
    pos = jnp.arange(t, dtype=I32).astype(F32)[:, None]
    lane = jnp.arange(LANES)

    def table(dim):
        half = dim // 2
        inv = ROPE_THETA ** (-jnp.arange(half, dtype=F32) / half)
        ang = pos * inv[None, :]
        j = lane % dim
        cos = jnp.cos(ang)[:, j % half]
        sin = jnp.sin(ang)[:, j % half] * jnp.where(j < half, -1.0, 1.0)[None, :]
        return cos, sin

    return table(HEAD_DIM), table(IDX_DIM)


def _attention_layer(x, mod, norm_g, w_in, w_out, tables):
    t, d = x.shape
    sh1, sc1, g1 = mod[0:d], mod[d:2 * d], mod[2 * d:3 * d]
    (cos_h, sin_h), (cos_i, sin_i) = tables
    hn = _normmod(x, norm_g, sc1, sh1)
    w_bf = w_in.astype(BF16)
    nq = N_HEADS * HEAD_DIM
    nkv = N_KV_GROUPS * HEAD_DIM
    nqi = IDX_HEADS * IDX_DIM
    tn = 512
    q = _proj_rope(hn, w_bf, 0, nq, cos_h, sin_h, HEAD_DIM // 2, BF16)
    kv = _proj_kv(hn, w_bf, nq // nkv, nq // nkv + 1, cos_h, sin_h)
    qi = _proj_rope(hn, w_bf, (nq + 2 * nkv) // tn, nqi, cos_i, sin_i, IDX_DIM // 2, BF16)
    tail0 = nq + 2 * nkv + nqi
    w_tail = jnp.pad(w_bf[:, tail0:], ((0, 0), (0, LANES - (IDX_DIM + IDX_HEADS))))
    lane = jnp.arange(LANES)
    wi_scale = (IDX_HEADS ** -0.5) * (IDX_DIM ** -0.5)
    cos_t = jnp.where(lane[None, :] < IDX_DIM, cos_i, wi_scale)
    sin_t = jnp.where(lane[None, :] < IDX_DIM, sin_i, 0.0)
    tail = _proj_rope(hn, w_tail, 0, LANES, cos_t, sin_t, IDX_DIM // 2, F32, tn=LANES)
    ki = tail[:, :IDX_DIM].astype(BF16)
    wit = tail[:, IDX_DIM:IDX_DIM + IDX_HEADS].T
    n_keep = min(TOPK_KEYS, t // 4)
    idx_t, nv = _indexer(qi.T, ki, wit, n_keep)
    att = _gattn(nv.reshape(t), idx_t.T, q.reshape(t * N_HEADS, HEAD_DIM), kv.reshape(t * KV_ROWS, LANES), n_keep)
    att = att.reshape(t, nq)
    return _mm_res(att, w_out.astype(BF16), jnp.zeros((d,), F32), x, g1)


def _conv_layer(x, mod, norm_g, w_pw1, b_pw1, w_dw, b_dw, ln_g, ln_b, w_pw2, b_pw2):
    t, d = x.shape
    sh1, sc1, g1 = mod[0:d], mod[d:2 * d], mod[2 * d:3 * d]
    hn = _normmod(x, norm_g, sc1, sh1)
    u = _glu(hn, w_pw1.astype(BF16), b_pw1)
    v = _dwconv_ln(u, w_dw, b_dw, ln_g, ln_b)
    return _mm_res(v, w_pw2.astype(BF16), b_pw2, x, g1)


def _moe_layer(x, mod, norm_g, w_rg, b_rg, w_re, b_re, w_gate, w_up, w_down, final_g, final_norm):
    t, d = x.shape
    sh2, sc2, g2 = mod[3 * d:4 * d], mod[4 * d:5 * d], mod[5 * d:6 * d]
    h, aux, counts = _route(x, norm_g, sc2, sh2, w_rg, b_rg, w_re, b_re)
    bucket = aux[:, AUX_BUCKET].astype(I32)
    rank = aux[:, AUX_RANK].astype(I32)
    sizes = counts[0, :N_BUCKETS].astype(I32)
    padded = (sizes + MOE_TM - 1) // MOE_TM * MOE_TM
    ends = jnp.cumsum(padded)
    dest = (ends - padded)[bucket] + rank
    n_blocks = t // MOE_TM + N_BUCKETS
    blk_start = jnp.arange(n_blocks, dtype=I32) * MOE_TM
    blk_bucket = jnp.minimum(jnp.searchsorted(ends, blk_start, side='right'), N_BUCKETS - 1).astype(I32)
    live = (blk_start < ends[-1]).astype(I32)
    group0 = (blk_bucket // N_PAIRS) * EXPERTS_PER_GROUP
    e_lo = group0 + jnp.asarray(PAIR_LO, I32)[blk_bucket % N_PAIRS]
    e_hi = group0 + jnp.asarray(PAIR_HI, I32)[blk_bucket % N_PAIRS]
    xs, ws = _dispatch(dest, h, aux, n_blocks * MOE_TM)
    ys = _ffn(e_lo, e_hi, live, xs, ws, w_gate.astype(BF16), w_up.astype(BF16), w_down.astype(BF16))
    return _combine(dest, x, g2, final_g, ys, final_norm)


def kernel(x, c, ada_w, ada_b, norm1_g, norm2_g, attn_w_in, attn_w_out, conv_w_pw1, conv_b_pw1, conv_w_dw, conv_b_dw, conv_ln_g, conv_ln_b, conv_w_pw2, conv_b_pw2, router_w_group, router_b_group, router_w_expert, router_b_expert, exp_w_gate, exp_w_up, exp_w_down, final_g):
    bsz, t, d = x.shape
    assert bsz == 1 and d == N_HEADS * HEAD_DIM
    depth = ada_w.shape[0]
    mods = _ada_mod(c, ada_w, ada_b)
    tables = _rope_tables(t)
    xf = x.reshape(t, d)
    for i in range(depth):
        j = i // 2
        if i % 2 == 0:
            xf = _attention_layer(xf, mods[i], norm1_g[i], attn_w_in[j], attn_w_out[j], tables)
        else:
            xf = _conv_layer(xf, mods[i], norm1_g[i], conv_w_pw1[j], conv_b_pw1[j], conv_w_dw[j], conv_b_dw[j],
                             conv_ln_g[j], conv_ln_b[j], conv_w_pw2[j], conv_b_pw2[j])
        xf = _moe_layer(xf, mods[i], norm2_g[i], router_w_group[i], router_b_group[i], router_w_expert[i],
                        router_b_expert[i], exp_w_gate[i], exp_w_up[i], exp_w_down[i], final_g, i == depth - 1)
    return xf.reshape(bsz, t, d)
```

```python
import functools

import jax
import jax.numpy as jnp
from jax import lax
from jax.experimental import pallas as pl
from jax.experimental.pallas import tpu as pltpu

EPS = 1e-6
CHUNK = 64
N_HEADS = 16
N_KV_GROUPS = 4
HEADS_PER_GROUP = N_HEADS // N_KV_GROUPS
HEAD_DIM = 128
IDX_HEADS = 16
IDX_DIM = 64
TOPK_KEYS = 256
ROPE_THETA = 10000.0
NEG_INF = -1e30
CONV_WIDTH = 31
N_GROUPS = 4
EXPERTS_PER_GROUP = 4
N_EXPERTS = N_GROUPS * EXPERTS_PER_GROUP

LANES = 128
KV_ROWS = N_KV_GROUPS
INT_MIN = -(2 ** 31)
INVALID_SHIFT = 1 << 30
VMEM_LIMIT = 56 * 1024 * 1024

F32 = jnp.float32
BF16 = jnp.bfloat16
I32 = jnp.int32


def _cparams(sem):
    return pltpu.CompilerParams(dimension_semantics=sem, vmem_limit_bytes=VMEM_LIMIT)


def _ada_kernel(c_ref, w_ref, b_ref, o_ref):
    c = c_ref[...]
    ca = c * jax.nn.sigmoid(c)
    o_ref[0] = jnp.dot(ca, w_ref[0], preferred_element_type=F32, precision=lax.Precision.HIGHEST) + b_ref[0]


def _ada_mod(c, ada_w, ada_b):
    depth, d, n = ada_w.shape
    tn = 768
    c8 = jnp.zeros((8, d), F32).at[0].set(c[0])
    out = pl.pallas_call(
        _ada_kernel,
        grid=(depth, n // tn),
        in_specs=[pl.BlockSpec((8, d), lambda i, j: (0, 0)),
                  pl.BlockSpec((1, d, tn), lambda i, j: (i, 0, j)),
                  pl.BlockSpec((1, 1, tn), lambda i, j: (i, 0, j))],
        out_specs=pl.BlockSpec((1, 8, tn), lambda i, j: (i, 0, j)),
        out_shape=jax.ShapeDtypeStruct((depth, 8, n), F32),
        compiler_params=_cparams(("arbitrary", "arbitrary")),
        name="ada_mod",
    )(c8, ada_w, ada_b.reshape(depth, 1, n))
    return out[:, 0, :]


def _normmod_kernel(x_ref, g_ref, sc_ref, sh_ref, o_ref):
    x = x_ref[...]
    xn = x * lax.rsqrt(jnp.mean(x * x, axis=-1, keepdims=True) + EPS)
    o_ref[...] = ((xn * g_ref[...]) * (1.0 + sc_ref[...]) + sh_ref[...]).astype(o_ref.dtype)


def _normmod(x, g, sc, sh, tm=512):
    t, d = x.shape
    vec = pl.BlockSpec((1, d), lambda i: (0, 0))
    return pl.pallas_call(
        _normmod_kernel,
        grid=(t // tm,),
        in_specs=[pl.BlockSpec((tm, d), lambda i: (i, 0)), vec, vec, vec],
        out_specs=pl.BlockSpec((tm, d), lambda i: (i, 0)),
        out_shape=jax.ShapeDtypeStruct((t, d), BF16),
        compiler_params=_cparams(("arbitrary",)),
        name="normmod",
    )(x, g.reshape(1, d), sc.reshape(1, d), sh.reshape(1, d))


N_PAIRS = EXPERTS_PER_GROUP * (EXPERTS_PER_GROUP - 1) // 2
N_BUCKETS = N_GROUPS * N_PAIRS
PAIR_LO = (0, 0, 0, 1, 1, 2)
PAIR_HI = (1, 2, 3, 2, 3, 3)
MOE_TM = 256
AUX_W_LO, AUX_W_HI, AUX_BUCKET, AUX_RANK = 0, 1, 2, 3


def _route_kernel(x_ref, g_ref, sc_ref, sh_ref, wr_ref, br_ref, tri_ref, h_ref, aux_ref, cnt_ref, run_ref):
    @pl.when(pl.program_id(0) == 0)
    def _():
        run_ref[...] = jnp.zeros_like(run_ref)

    x = x_ref[...]
    xn = x * lax.rsqrt(jnp.mean(x * x, axis=-1, keepdims=True) + EPS)
    h = (xn * g_ref[...]) * (1.0 + sc_ref[...]) + sh_ref[...]
    h_ref[...] = h.astype(h_ref.dtype)
    h_hi = h.astype(BF16)
    h_lo = (h - h_hi.astype(F32)).astype(BF16)
    w = wr_ref[...]
    w_hi = w.astype(BF16)
    w_lo = (w - w_hi.astype(F32)).astype(BF16)
    logits = (jnp.dot(h_hi, w_hi, preferred_element_type=F32) + jnp.dot(h_lo, w_hi, preferred_element_type=F32)
              + jnp.dot(h_hi, w_lo, preferred_element_type=F32)) + br_ref[...]
    tm = x.shape[0]
    lane4 = lax.broadcasted_iota(I32, (tm, N_GROUPS), 1)

    def first_argmax(v):
        m = jnp.max(v, axis=-1, keepdims=True)
        return m, jnp.min(jnp.where(v == m, lane4, N_GROUPS), axis=-1, keepdims=True)

    gl = logits[:, 0:N_GROUPS]
    gmax, gidx = first_argmax(gl)
    g_w = 1.0 / jnp.sum(jnp.exp(gl - gmax), axis=-1, keepdims=True)
    el = jnp.zeros((tm, EXPERTS_PER_GROUP), F32)
    for g in range(N_GROUPS):
        lo = N_GROUPS + g * EXPERTS_PER_GROUP
        el = jnp.where(gidx == g, logits[:, lo:lo + EXPERTS_PER_GROUP], el)
    emax = jnp.max(el, axis=-1, keepdims=True)
    ee = jnp.exp(el - emax)
    ep = ee / jnp.sum(ee, axis=-1, keepdims=True)
    p1, i1 = first_argmax(ep)
    rest = jnp.where(lane4 == i1, -1.0, ep)
    p2, i2 = first_argmax(rest)
    den = p1 + p2
    w1 = g_w * p1 / den
    w2 = g_w * p2 / den
    first_lo = i1 < i2
    a = jnp.minimum(i1, i2)
    b = jnp.maximum(i1, i2)
    w_lo = jnp.where(first_lo, w1, w2)
    w_hi = jnp.where(first_lo, w2, w1)
    pair = jnp.where(a == 0, b - 1, jnp.where(a == 1, b + 1, N_PAIRS - 1))
    bucket = gidx * N_PAIRS + pair
    lane = lax.broadcasted_iota(I32, (tm, LANES), 1)
    onehot = lane == bucket
    prefix = jnp.dot(tri_ref[...], onehot.astype(F32).astype(BF16), preferred_element_type=F32)
    run = run_ref[...]
    rank = jnp.sum(jnp.where(onehot, prefix + run, 0.0), axis=1, keepdims=True) - 1.0
    run = run + prefix[tm - 1:tm, :]
    run_ref[...] = run
    cnt_ref[...] = run
    aux_ref[...] = (jnp.where(lane == AUX_W_LO, w_lo, 0.0) + jnp.where(lane == AUX_W_HI, w_hi, 0.0)
                    + jnp.where(lane == AUX_BUCKET, bucket.astype(F32), 0.0) + jnp.where(lane == AUX_RANK, rank, 0.0))


def _route(x, g, sc, sh, w_rg, b_rg, w_re, b_re, tm=512):
    t, d = x.shape
    w_all = jnp.concatenate([w_rg, jnp.moveaxis(w_re, 0, 1).reshape(d, N_EXPERTS)], axis=1)
    b_all = jnp.concatenate([b_rg, b_re.reshape(N_EXPERTS)])
    npad = LANES - w_all.shape[1]
    w_all = jnp.pad(w_all, ((0, 0), (0, npad)))
    b_all = jnp.pad(b_all, (0, npad)).reshape(1, LANES)
    vec = pl.BlockSpec((1, d), lambda i: (0, 0))
    r = lax.broadcasted_iota(I32, (tm, tm), 0)
    c = lax.broadcasted_iota(I32, (tm, tm), 1)
    tri = (c <= r).astype(BF16)
    return pl.pallas_call(
        _route_kernel,
        grid=(t // tm,),
        in_specs=[pl.BlockSpec((tm, d), lambda i: (i, 0)), vec, vec, vec,
                  pl.BlockSpec((d, LANES), lambda i: (0, 0)), pl.BlockSpec((1, LANES), lambda i: (0, 0)),
                  pl.BlockSpec((tm, tm), lambda i: (0, 0))],
        out_specs=[pl.BlockSpec((tm, d), lambda i: (i, 0)), pl.BlockSpec((tm, LANES), lambda i: (i, 0)),
                   pl.BlockSpec((1, LANES), lambda i: (0, 0))],
        out_shape=[jax.ShapeDtypeStruct((t, d), F32), jax.ShapeDtypeStruct((t, LANES), F32),
                   jax.ShapeDtypeStruct((1, LANES), F32)],
        scratch_shapes=[pltpu.VMEM((1, LANES), F32)],
        compiler_params=_cparams(("arbitrary",)),
        name="route",
    )(x, g.reshape(1, d), sc.reshape(1, d), sh.reshape(1, d), w_all, b_all, tri)


def _rope_block(blk, cos, sin, half):
    if half == HEAD_DIM // 2:
        partner = pltpu.roll(blk, HEAD_DIM // 2, 1)
    else:
        lane = lax.broadcasted_iota(I32, blk.shape, 1)
        first = (lane % IDX_DIM) < (IDX_DIM // 2)
        partner = jnp.where(first, pltpu.roll(blk, LANES - IDX_DIM // 2, 1), pltpu.roll(blk, IDX_DIM // 2, 1))
    return blk * cos + partner * sin


def _proj_rope_kernel(a_ref, w_ref, cos_ref, sin_ref, o_ref, *, half):
    acc = jnp.dot(a_ref[...], w_ref[...], preferred_element_type=F32)
    cos = cos_ref[...]
    sin = sin_ref[...]
    for h in range(acc.shape[1] // LANES):
        sl = slice(h * LANES, (h + 1) * LANES)
        o_ref[:, sl] = _rope_block(acc[:, sl], cos, sin, half).astype(o_ref.dtype)


def _proj_rope(a, w, col_block0, n_out, cos, sin, half, out_dtype, tm=512, tn=512):
    t, d = a.shape
    return pl.pallas_call(
        functools.partial(_proj_rope_kernel, half=half),
        grid=(t // tm, n_out // tn),
        in_specs=[pl.BlockSpec((tm, d), lambda i, j: (i, 0)),
                  pl.BlockSpec((d, tn), lambda i, j: (0, col_block0 + j)),
                  pl.BlockSpec((tm, LANES), lambda i, j: (i, 0)),
                  pl.BlockSpec((tm, LANES), lambda i, j: (i, 0))],
        out_specs=pl.BlockSpec((tm, tn), lambda i, j: (i, j)),
        out_shape=jax.ShapeDtypeStruct((t, n_out), out_dtype),
        compiler_params=_cparams(("arbitrary", "arbitrary")),
        name="proj_rope",
    )(a, w, cos, sin)


def _proj_kv_kernel(a_ref, wk_ref, wv_ref, cos_ref, sin_ref, o_ref):
    a = a_ref[...]
    k = jnp.dot(a, wk_ref[...], preferred_element_type=F32)
    v = jnp.dot(a, wv_ref[...], preferred_element_type=F32)
    cos = cos_ref[...]
    sin = sin_ref[...]
    for h in range(k.shape[1] // LANES):
        sl = slice(h * LANES, (h + 1) * LANES)
        kr = _rope_block(k[:, sl], cos, sin, HEAD_DIM // 2)
        kb = lax.bitcast_convert_type(kr.astype(BF16).astype(F32), I32)
        vb = lax.bitcast_convert_type(v[:, sl].astype(BF16).astype(F32), I32)
        o_ref[:, sl] = kb | lax.shift_right_logical(vb, 16)


def _proj_kv(a, w, kblock, vblock, cos, sin, tm=512):
    t, d = a.shape
    n = N_KV_GROUPS * HEAD_DIM
    return pl.pallas_call(
        _proj_kv_kernel,
        grid=(t // tm,),
        in_specs=[pl.BlockSpec((tm, d), lambda i: (i, 0)),
                  pl.BlockSpec((d, n), lambda i: (0, kblock)),
                  pl.BlockSpec((d, n), lambda i: (0, vblock)),
                  pl.BlockSpec((tm, LANES), lambda i: (i, 0)),
                  pl.BlockSpec((tm, LANES), lambda i: (i, 0))],
        out_specs=pl.BlockSpec((tm, n), lambda i: (i, 0)),
        out_shape=jax.ShapeDtypeStruct((t, n), I32),
        compiler_params=_cparams(("arbitrary",)),
        name="proj_kv",
    )(a, w, w, cos, sin)


QB = 256
TK = 512
RC = 64


def _indexer_kernel(qit_ref, ki_ref, wit_ref, tri_ref, idx_ref, nv_ref, d_ref, *, n_keep):
    b = pl.program_id(0)
    q0 = b * QB
    n_t = (q0 + QB + TK - 1) // TK
    n_rows = n_t * TK
    rowi = lax.broadcasted_iota(I32, (TK, QB), 0)
    qchunk = (q0 + lax.broadcasted_iota(I32, (TK, QB), 1)) // CHUNK

    def admissible(k0):
        return ((k0 + rowi) // CHUNK) <= qchunk

    def score_tile(kt, carry):
        k0 = pl.multiple_of(kt * TK, TK)
        ki_t = ki_ref[pl.ds(k0, TK), :]
        acc = jnp.zeros((TK, QB), F32)
        for h in range(IDX_HEADS):
            a = jnp.dot(ki_t, qit_ref[h * IDX_DIM:(h + 1) * IDX_DIM, :], preferred_element_type=F32)
            acc = acc + jnp.maximum(a, 0.0) * wit_ref[h:h + 1, :]
        s = jnp.where(admissible(k0), acc + 0.0, NEG_INF)
        u = lax.bitcast_convert_type(s, I32)
        d_ref[pl.ds(k0, TK), :] = jnp.where(u < 0, u ^ 0x7FFFFFFF, u)
        return carry

    lax.fori_loop(0, n_t, score_tile, 0)

    def count_where(pred):
        def body(kt, acc):
            k0 = pl.multiple_of(kt * TK, TK)
            hit = pred(d_ref[pl.ds(k0, TK), :]).astype(I32)
            return acc + jnp.sum(hit.reshape(TK // 8, 8, QB), axis=0)
        acc = lax.fori_loop(0, n_t, body, jnp.zeros((8, QB), I32))
        return jnp.sum(acc, axis=0, keepdims=True)

    def select_bit(it, carry):
        v_u, cnt_v = carry
        cand_u = v_u | jnp.left_shift(jnp.int32(1), 31 - it)
        cand = cand_u ^ INT_MIN
        cnt = count_where(lambda key: key >= cand)
        keep = cnt >= n_keep
        return jnp.where(keep, cand_u, v_u), jnp.where(keep, cnt, cnt_v)

    thr_u, cnt_thr = lax.fori_loop(0, 32, select_bit,
                                   (jnp.zeros((1, QB), I32), jnp.full((1, QB), n_keep + 1, I32)))
    thr = thr_u ^ INT_MIN

    tri = tri_ref[...]
    zero = jnp.zeros((1, QB), F32)

    def place(k0, sel, run_sel):
        pre_sel = jnp.dot(tri, sel.astype(F32).astype(BF16), preferred_element_type=F32)
        rank = (run_sel + pre_sel).astype(I32) - 1
        d_ref[pl.ds(k0, TK), :] = jnp.where(sel, (k0 + rowi) - rank, INVALID_SHIFT)
        return run_sel + pre_sel[TK - 1:TK, :]

    def rank_exact_count():
        def tile(kt, run_sel):
            k0 = pl.multiple_of(kt * TK, TK)
            return place(k0, admissible(k0) & (d_ref[pl.ds(k0, TK), :] >= thr), run_sel)
        return lax.fori_loop(0, n_t, tile, zero)

    def rank_with_ties():
        need = (n_keep - count_where(lambda key: key > thr)).astype(F32)

        def tile(kt, carry):
            run_eq, run_sel = carry
            k0 = pl.multiple_of(kt * TK, TK)
            key = d_ref[pl.ds(k0, TK), :]
            eq = key == thr
            eq_f = eq.astype(F32)
            pre_eq = jnp.dot(tri, eq_f.astype(BF16), preferred_element_type=F32)
            sel = admissible(k0) & ((key > thr) | (eq & ((run_eq + pre_eq - eq_f) < need)))
            return run_eq + pre_eq[TK - 1:TK, :], place(k0, sel, run_sel)
        return lax.fori_loop(0, n_t, tile, (zero, zero))[1]

    n_sel = lax.cond(jnp.max(jnp.abs(cnt_thr - n_keep)) == 0, rank_exact_count, rank_with_ties)

    d_ref[pl.ds(pl.multiple_of(n_rows, RC), RC), :] = jnp.full((RC, QB), INVALID_SHIFT, I32)
    n_chunks = n_rows // RC
    win_chunks = -(-n_keep // RC)
    window = win_chunks * RC
    n_stages = (d_ref.shape[0] - RC - 1).bit_length()
    for s in range(n_stages):
        sh = 1 << s
        test = sh | INVALID_SHIFT

        def merge(own, inc, sh=sh, test=test):
            take = (inc & test) == sh
            stay = (own & test) == 0
            return jnp.where(take, inc, jnp.where(stay, own, INVALID_SHIFT))

        def chunk_with_source(c, carry, sh=sh, merge=merge):
            r0 = pl.multiple_of(c * RC, RC)
            if sh < 8:
                x = d_ref[pl.ds(r0, RC + 8), :]
                own, inc = x[0:RC], x[sh:sh + RC]
            else:
                own = d_ref[pl.ds(r0, RC), :]
                inc = d_ref[pl.ds(pl.multiple_of(r0 + sh, 8), RC), :]
            d_ref[pl.ds(r0, RC), :] = merge(own, inc)
            return carry

        def chunk_without_source(c, carry, merge=merge):
            r0 = pl.multiple_of(c * RC, RC)
            own = d_ref[pl.ds(r0, RC), :]
            d_ref[pl.ds(r0, RC), :] = merge(own, jnp.full_like(own, INVALID_SHIFT))
            return carry

        def window_chunk(c, carry, sh=sh, merge=merge):
            r0 = pl.multiple_of((c // win_chunks) * sh + (c % win_chunks) * RC, RC)
            src = r0 + sh
            own = d_ref[pl.ds(r0, RC), :]
            inc = d_ref[pl.ds(pl.multiple_of(jnp.minimum(src, n_rows), RC), RC), :]
            d_ref[pl.ds(r0, RC), :] = merge(own, inc)
            return carry

        if sh < RC:
            lax.fori_loop(0, n_chunks, chunk_with_source, 0)
        elif sh < 2 * window:
            n_src = jnp.maximum(n_rows - sh, 0) // RC
            lax.fori_loop(0, n_src, chunk_with_source, 0)
            lax.fori_loop(n_src, n_chunks, chunk_without_source, 0)
        else:
            lax.fori_loop(0, ((n_rows + sh - 1) // sh) * win_chunks, window_chunk, 0)

    slot = lax.broadcasted_iota(I32, (n_keep, QB), 0)
    dd = d_ref[0:n_keep, :]
    idx_ref[...] = jnp.where(dd < INVALID_SHIFT, (slot + dd) * KV_ROWS, 0)
    nv_ref[...] = n_sel.astype(I32)


def _indexer(qit, ki, wit, n_keep):
    t = ki.shape[0]
    r = lax.broadcasted_iota(I32, (TK, TK), 0)
    c = lax.broadcasted_iota(I32, (TK, TK), 1)
    tri = (c <= r).astype(BF16)
    return pl.pallas_call(
        functools.partial(_indexer_kernel, n_keep=n_keep),
        grid=(t // QB,),
        in_specs=[pl.BlockSpec((IDX_HEADS * IDX_DIM, QB), lambda b: (0, b)),
                  pl.BlockSpec((t, IDX_DIM), lambda b: (0, 0)),
                  pl.BlockSpec((IDX_HEADS, QB), lambda b: (0, b)),
                  pl.BlockSpec((TK, TK), lambda b: (0, 0))],
        out_specs=[pl.BlockSpec((n_keep, QB), lambda b: (0, b)), pl.BlockSpec((1, QB), lambda b: (0, b))],
        out_shape=[jax.ShapeDtypeStruct((n_keep, t), I32), jax.ShapeDtypeStruct((1, t), I32)],
        scratch_shapes=[pltpu.VMEM((t + RC, QB), I32)],
        compiler_params=_cparams(("arbitrary",)),
        name="indexer",
    )(qit, ki, wit, tri)


GQ = 64
LIST_WIN = 8


def _gattn_kernel(nv_ref, lists_ref, q_ref, kv_hbm, o_ref, kv_vmem, stage_a, stage_b, stage_c, stage_d,
                  s_a, s_b, lists_a, lists_b, sem_kv, sem_lists, *, n_keep):
    i = pl.program_id(0)

    @pl.when(i == 0)
    def _():
        cp = pltpu.make_async_copy(kv_hbm, kv_vmem, sem_kv)
        cp.start()
        cp.wait()

    n_col = n_keep * KV_ROWS
    colg = lax.broadcasted_iota(I32, (N_HEADS, n_col), 1)
    rowh = lax.broadcasted_iota(I32, (N_HEADS, n_col), 0)
    own_group = (colg % KV_ROWS) == (rowh // HEADS_PER_GROUP)
    col_slot = colg // KV_ROWS
    scale = HEAD_DIM ** -0.5

    last = GQ - 1
    stages = (stage_a, stage_b, stage_c, stage_d)
    s_bufs = (s_a, s_b)

    list_sets = (lists_a, lists_b)

    def list_copy(window_row, p):
        src = lists_ref.at[pl.ds(pl.multiple_of(window_row, LIST_WIN), LIST_WIN), :]
        return pltpu.make_async_copy(src, list_sets[p], sem_lists.at[p])

    def gather(p, u, stage):
        for j in range(n_keep):
            row = pl.multiple_of(list_sets[p][u, j], KV_ROWS)
            stage[j * KV_ROWS:(j + 1) * KV_ROWS, :] = kv_vmem[pl.ds(row, KV_ROWS), :]

    def scores(r, stage, s_buf):
        k_all = lax.bitcast_convert_type(stage[...] & jnp.int32(-65536), F32).astype(BF16)
        qh = q_ref[pl.ds(pl.multiple_of(r * N_HEADS, N_HEADS), N_HEADS), :]
        s = lax.dot_general(qh, k_all, (((1,), (1,)), ((), ())), preferred_element_type=F32) * scale
        mask = own_group & (col_slot < nv_ref[i * GQ + r])
        s_buf[...] = jnp.where(mask, s, NEG_INF)

    def combine(r, stage, s_buf):
        s = s_buf[...]
        p = jnp.exp(s - jnp.max(s, axis=1, keepdims=True))
        l = jnp.sum(p, axis=1, keepdims=True)
        v_all = lax.bitcast_convert_type(jnp.left_shift(stage[...], 16), F32).astype(BF16)
        o = jnp.dot(p.astype(BF16), v_all, preferred_element_type=F32) / l
        o_ref[pl.ds(pl.multiple_of(r * N_HEADS, N_HEADS), N_HEADS), :] = o.astype(o_ref.dtype)

    list_copy(0, 1).start()
    list_copy(0, 1).wait()
    list_copy(LIST_WIN, 0).start()
    gather(1, 0, stages[0])
    gather(1, 1, stages[1])
    scores(0, stages[0], s_bufs[0])

    def query_group(it, carry):
        for p in range(2):
            r = LIST_WIN * (2 * it + p)
            list_copy(LIST_WIN + r, p).wait()
            list_copy(LIST_WIN + jnp.minimum(r + LIST_WIN, GQ - LIST_WIN), 1 - p).start()
            for u in range(LIST_WIN):
                gather(p, u, stages[(u + 2) % 4])
                scores(jnp.minimum(r + u + 1, last), stages[(u + 1) % 4], s_bufs[(u + 1) % 2])
                combine(r + u, stages[u % 4], s_bufs[u % 2])
        return carry

    lax.fori_loop(0, GQ // (2 * LIST_WIN), query_group, 0)
    list_copy(LIST_WIN, 0).wait()


def _gattn(nv, idx, q2, kv2, n_keep):
    t = nv.shape[0]
    blk = idx.reshape(t // GQ, GQ, n_keep)
    head = jnp.concatenate([blk[:, 0:2], jnp.zeros((t // GQ, LIST_WIN - 2, n_keep), I32)], axis=1)
    lists = jnp.concatenate([head, blk[:, 2:], blk[:, GQ - 2:]], axis=1).reshape(t // GQ * (LIST_WIN + GQ), n_keep)
    grid_spec = pltpu.PrefetchScalarGridSpec(
        num_scalar_prefetch=1,
        grid=(t // GQ,),
        in_specs=[pl.BlockSpec((LIST_WIN + GQ, n_keep), lambda i, nv: (i, 0)),
                  pl.BlockSpec((GQ * N_HEADS, HEAD_DIM), lambda i, nv: (i, 0)),
                  pl.BlockSpec(memory_space=pl.ANY)],
        out_specs=pl.BlockSpec((GQ * N_HEADS, HEAD_DIM), lambda i, nv: (i, 0)),
        scratch_shapes=[pltpu.VMEM(kv2.shape, I32),
                        pltpu.VMEM((n_keep * KV_ROWS, LANES), I32),
                        pltpu.VMEM((n_keep * KV_ROWS, LANES), I32),
                        pltpu.VMEM((n_keep * KV_ROWS, LANES), I32),
                        pltpu.VMEM((n_keep * KV_ROWS, LANES), I32),
                        pltpu.VMEM((N_HEADS, n_keep * KV_ROWS), F32),
                        pltpu.VMEM((N_HEADS, n_keep * KV_ROWS), F32),
                        pltpu.SMEM((LIST_WIN, n_keep), I32),
                        pltpu.SMEM((LIST_WIN, n_keep), I32),
                        pltpu.SemaphoreType.DMA,
                        pltpu.SemaphoreType.DMA((2,))],
    )
    return pl.pallas_call(
        functools.partial(_gattn_kernel, n_keep=n_keep),
        grid_spec=grid_spec,
        out_shape=jax.ShapeDtypeStruct((t * N_HEADS, HEAD_DIM), BF16),
        compiler_params=_cparams(("arbitrary",)),
        name="gattn",
    )(nv, lists, q2, kv2)


def _mm_res_kernel(a_ref, w_ref, b_ref, x_ref, g_ref, o_ref):
    y = jnp.dot(a_ref[...], w_ref[...], preferred_element_type=F32) + b_ref[...]
    o_ref[...] = x_ref[...] + g_ref[...] * y


def _mm_res(a, w, bias, x, gate, tm=512, tn=512):
    t, k = a.shape
    n = w.shape[1]
    return pl.pallas_call(
        _mm_res_kernel,
        grid=(t // tm, n // tn),
        in_specs=[pl.BlockSpec((tm, k), lambda i, j: (i, 0)),
                  pl.BlockSpec((k, tn), lambda i, j: (0, j)),
                  pl.BlockSpec((1, tn), lambda i, j: (0, j)),
                  pl.BlockSpec((tm, tn), lambda i, j: (i, j)),
                  pl.BlockSpec((1, tn), lambda i, j: (0, j))],
        out_specs=pl.BlockSpec((tm, tn), lambda i, j: (i, j)),
        out_shape=jax.ShapeDtypeStruct((t, n), F32),
        compiler_params=_cparams(("arbitrary", "arbitrary")),
        name="mm_res",
    )(a, w, bias.reshape(1, n), x, gate.reshape(1, n))


def _glu_kernel(a_ref, wa_ref, wg_ref, ba_ref, bg_ref, o_ref):
    a = a_ref[...]
    ya = jnp.dot(a, wa_ref[...], preferred_element_type=F32) + ba_ref[...]
    yg = jnp.dot(a, wg_ref[...], preferred_element_type=F32) + bg_ref[...]
    o_ref[...] = ya * jax.nn.sigmoid(yg)


def _glu(a, w, b, tm=512, tn=512):
    t, k = a.shape
    n = w.shape[1] // 2
    nb = n // tn
    b2 = b.reshape(1, 2 * n)
    return pl.pallas_call(
        _glu_kernel,
        grid=(t // tm, nb),
        in_specs=[pl.BlockSpec((tm, k), lambda i, j: (i, 0)),
                  pl.BlockSpec((k, tn), lambda i, j: (0, j)),
                  pl.BlockSpec((k, tn), lambda i, j: (0, j + nb)),
                  pl.BlockSpec((1, tn), lambda i, j: (0, j)),
                  pl.BlockSpec((1, tn), lambda i, j: (0, j + nb))],
        out_specs=pl.BlockSpec((tm, tn), lambda i, j: (i, j)),
        out_shape=jax.ShapeDtypeStruct((t, n), F32),
        compiler_params=_cparams(("arbitrary", "arbitrary")),
        name="glu",
    )(a, w, w, b2, b2)


HALO = 32


SUBLANES = 8
CONV_ROWS = 32
CONV_COLS = 512


def _dwconv_ln_kernel(u_ref, prev_ref, w_ref, b_ref, g_ref, beta_ref, o_ref, ext_ref, sh_ref, conv_ref):
    i = pl.program_id(0)
    tm, d = u_ref.shape
    ext_ref[0:HALO, :] = jnp.where(i == 0, 0.0, prev_ref[...])
    ext_ref[HALO:HALO + tm, :] = u_ref[...]
    for phase in range(SUBLANES):
        sh_ref[phase, 0:tm + HALO - phase, :] = ext_ref[phase:tm + HALO, :]
    first = HALO - (CONV_WIDTH - 1)

    def conv_rows(c, carry):
        r0 = pl.multiple_of(c * CONV_ROWS, CONV_ROWS)
        for cb in range(d // CONV_COLS):
            cols = slice(cb * CONV_COLS, (cb + 1) * CONV_COLS)
            acc = jnp.zeros((CONV_ROWS, CONV_COLS), F32) + b_ref[:, cols]
            for k in range(CONV_WIDTH):
                phase = (first + k) % SUBLANES
                start = pl.multiple_of(r0 + (first + k - phase), SUBLANES)
                acc = acc + sh_ref[phase, pl.ds(start, CONV_ROWS), cols] * w_ref[k:k + 1, cols]
            conv_ref[pl.ds(r0, CONV_ROWS), cols] = acc
        return carry

    lax.fori_loop(0, tm // CONV_ROWS, conv_rows, 0)
    acc = conv_ref[...]
    mu = jnp.mean(acc, axis=-1, keepdims=True)
    cen = acc - mu
    var = jnp.mean(cen * cen, axis=-1, keepdims=True)
    y = (cen * lax.rsqrt(var + EPS)) * g_ref[...] + beta_ref[...]
    o_ref[...] = (y * jax.nn.sigmoid(y)).astype(o_ref.dtype)


def _dwconv_ln(u, w_dw, b_dw, ln_g, ln_b, tm=256):
    t, d = u.shape
    ratio = tm // HALO
    vec = pl.BlockSpec((1, d), lambda i: (0, 0))
    return pl.pallas_call(
        _dwconv_ln_kernel,
        grid=(t // tm,),
        in_specs=[pl.BlockSpec((tm, d), lambda i: (i, 0)),
                  pl.BlockSpec((HALO, d), lambda i: (jnp.maximum(i * ratio - 1, 0), 0)),
                  pl.BlockSpec((HALO, d), lambda i: (0, 0)),
                  vec, vec, vec],
        out_specs=pl.BlockSpec((tm, d), lambda i: (i, 0)),
        out_shape=jax.ShapeDtypeStruct((t, d), BF16),
        scratch_shapes=[pltpu.VMEM((HALO + tm, d), F32), pltpu.VMEM((SUBLANES, HALO + tm, d), F32),
                        pltpu.VMEM((tm, d), F32)],
        compiler_params=_cparams(("arbitrary",)),
        name="dwconv_ln",
    )(u, u, jnp.pad(w_dw, ((0, HALO - CONV_WIDTH), (0, 0))), b_dw.reshape(1, d), ln_g.reshape(1, d), ln_b.reshape(1, d))


def _row_copy(src, src_row, dst, dst_row, sem):
    return pltpu.make_async_copy(src.at[pl.ds(src_row, 1), :], dst.at[pl.ds(dst_row, 1), :], sem)


ROW_UNROLL = 8


def _for_rows(n, fn):
    def group(gi, carry):
        for u in range(ROW_UNROLL):
            fn(gi * ROW_UNROLL + u, carry)
        return carry

    lax.fori_loop(0, n // ROW_UNROLL, group, 0)


def _dispatch_kernel(dest_ref, h_ref, aux_ref, xs_in, ws_in, xs_hbm, ws_hbm, sem_x, sem_w):
    del xs_in, ws_in
    tm = h_ref.shape[0]
    t0 = pl.program_id(0) * tm

    def issue(r, carry):
        slot = dest_ref[t0 + r]
        _row_copy(h_ref, r, xs_hbm, slot, sem_x).start()
        _row_copy(aux_ref, r, ws_hbm, slot, sem_w).start()
        return carry

    def drain(r, carry):
        _row_copy(h_ref, r, xs_hbm, 0, sem_x).wait()
        _row_copy(aux_ref, r, ws_hbm, 0, sem_w).wait()
        return carry

    _for_rows(tm, issue)
    _for_rows(tm, drain)


def _dispatch(dest, h, aux, n_rows, tm=256):
    t, d = h.shape
    grid_spec = pltpu.PrefetchScalarGridSpec(
        num_scalar_prefetch=1,
        grid=(t // tm,),
        in_specs=[pl.BlockSpec((tm, d), lambda i, dest: (i, 0)),
                  pl.BlockSpec((tm, LANES), lambda i, dest: (i, 0)),
                  pl.BlockSpec(memory_space=pl.ANY),
                  pl.BlockSpec(memory_space=pl.ANY)],
        out_specs=[pl.BlockSpec(memory_space=pl.ANY), pl.BlockSpec(memory_space=pl.ANY)],
        scratch_shapes=[pltpu.SemaphoreType.DMA, pltpu.SemaphoreType.DMA],
    )
    return pl.pallas_call(
        _dispatch_kernel,
        grid_spec=grid_spec,
        out_shape=[jax.ShapeDtypeStruct((n_rows, d), F32), jax.ShapeDtypeStruct((n_rows, LANES), F32)],
        input_output_aliases={3: 0, 4: 1},
        compiler_params=_cparams(("arbitrary",)),
        name="dispatch",
    )(dest, h, aux, jnp.zeros((n_rows, d), F32), jnp.zeros((n_rows, LANES), F32))


def _ffn_kernel(elo_ref, ehi_ref, live_ref, xs_ref, ws_ref, wg_lo, wu_lo, wd_lo, wg_hi, wu_hi, wd_hi, ys_ref):
    b = pl.program_id(0)

    @pl.when(live_ref[b] == 1)
    def _():
        x = xs_ref[...].astype(BF16)
        ws = ws_ref[...]

        def expert(wg, wu, wd):
            gate = jnp.dot(x, wg[0], preferred_element_type=F32)
            up = jnp.dot(x, wu[0], preferred_element_type=F32)
            hid = (gate * jax.nn.sigmoid(gate)) * up
            return jnp.dot(hid.astype(BF16), wd[0], preferred_element_type=F32)

        ys_ref[...] = (expert(wg_lo, wu_lo, wd_lo) * ws[:, AUX_W_LO:AUX_W_LO + 1]
                       + expert(wg_hi, wu_hi, wd_hi) * ws[:, AUX_W_HI:AUX_W_HI + 1])

    @pl.when(live_ref[b] == 0)
    def _():
        ys_ref[...] = jnp.zeros_like(ys_ref)


def _ffn(e_lo, e_hi, live, xs, ws, w_gate, w_up, w_down):
    n_rows, d = xs.shape
    _, _, de = w_gate.shape
    lo_in = pl.BlockSpec((1, d, de), lambda b, lo, hi, live: (lo[b], 0, 0))
    hi_in = pl.BlockSpec((1, d, de), lambda b, lo, hi, live: (hi[b], 0, 0))
    lo_out = pl.BlockSpec((1, de, d), lambda b, lo, hi, live: (lo[b], 0, 0))
    hi_out = pl.BlockSpec((1, de, d), lambda b, lo, hi, live: (hi[b], 0, 0))
    grid_spec = pltpu.PrefetchScalarGridSpec(
        num_scalar_prefetch=3,
        grid=(n_rows // MOE_TM,),
        in_specs=[pl.BlockSpec((MOE_TM, d), lambda b, lo, hi, live: (b, 0)),
                  pl.BlockSpec((MOE_TM, LANES), lambda b, lo, hi, live: (b, 0)),
                  lo_in, lo_in, lo_out, hi_in, hi_in, hi_out],
        out_specs=pl.BlockSpec((MOE_TM, d), lambda b, lo, hi, live: (b, 0)),
    )
    return pl.pallas_call(
        _ffn_kernel,
        grid_spec=grid_spec,
        out_shape=jax.ShapeDtypeStruct((n_rows, d), F32),
        compiler_params=_cparams(("arbitrary",)),
        name="ffn",
    )(e_lo, e_hi, live, xs, ws, w_gate, w_up, w_down, w_gate, w_up, w_down)


def _combine_kernel(dest_ref, x_ref, g_ref, fg_ref, ys_hbm, o_ref, buf, sem, *, final_norm):
    tm = x_ref.shape[0]
    t0 = pl.program_id(0) * tm

    def issue(r, carry):
        _row_copy(ys_hbm, dest_ref[t0 + r], buf, r, sem).start()
        return carry

    def drain(r, carry):
        _row_copy(ys_hbm, 0, buf, r, sem).wait()
        return carry

    _for_rows(tm, issue)
    _for_rows(tm, drain)
    y = x_ref[...] + g_ref[...] * buf[...]
    if final_norm:
        y = (y * lax.rsqrt(jnp.mean(y * y, axis=-1, keepdims=True) + EPS)) * fg_ref[...]
    o_ref[...] = y


def _combine(dest, x, gate, final_g, ys, final_norm, tm=256):
    t, d = x.shape
    vec = pl.BlockSpec((1, d), lambda i, dest: (0, 0))
    grid_spec = pltpu.PrefetchScalarGridSpec(
        num_scalar_prefetch=1,
        grid=(t // tm,),
        in_specs=[pl.BlockSpec((tm, d), lambda i, dest: (i, 0)), vec, vec, pl.BlockSpec(memory_space=pl.ANY)],
        out_specs=pl.BlockSpec((tm, d), lambda i, dest: (i, 0)),
        scratch_shapes=[pltpu.VMEM((tm, d), F32), pltpu.SemaphoreType.DMA],
    )
    return pl.pallas_call(
        functools.partial(_combine_kernel, final_norm=final_norm),
        grid_spec=grid_spec,
        out_shape=jax.ShapeDtypeStruct((t, d), F32),
        compiler_params=_cparams(("arbitrary",)),
        name="combine",
    )(dest, x, gate.reshape(1, d), final_g.reshape(1, d), ys)


def _rope_tables(t):
    pos = jnp.arange(t, dtype=I32).astype(F32)[:, None]
    lane = jnp.arange(LANES)

    def table(dim):
        half = dim // 2
        inv = ROPE_THETA ** (-jnp.arange(half, dtype=F32) / half)
        ang = pos * inv[None, :]
        j = lane % dim
        cos = jnp.cos(ang)[:, j % half]
        sin = jnp.sin(ang)[:, j % half] * jnp.where(j < half, -1.0, 1.0)[None, :]
        return cos, sin

    return table(HEAD_DIM), table(IDX_DIM)


def _attention_layer(x, mod, norm_g, w_in, w_out, tables):
    t, d = x.shape
    sh1, sc1, g1 = mod[0:d], mod[d:2 * d], mod[2 * d:3 * d]
    (cos_h, sin_h), (cos_i, sin_i) = tables
    hn = _normmod(x, norm_g, sc1, sh1)
    w_bf = w_in.astype(BF16)
    nq = N_HEADS * HEAD_DIM
    nkv = N_KV_GROUPS * HEAD_DIM
    nqi = IDX_HEADS * IDX_DIM
    tn = 512
    q = _proj_rope(hn, w_bf, 0, nq, cos_h, sin_h, HEAD_DIM // 2, BF16)
    kv = _proj_kv(hn, w_bf, nq // nkv, nq // nkv + 1, cos_h, sin_h)
    qi = _proj_rope(hn, w_bf, (nq + 2 * nkv) // tn, nqi, cos_i, sin_i, IDX_DIM // 2, BF16)
    tail0 = nq + 2 * nkv + nqi
    w_tail = jnp.pad(w_bf[:, tail0:], ((0, 0), (0, LANES - (IDX_DIM + IDX_HEADS))))
    lane = jnp.arange(LANES)
    wi_scale = (IDX_HEADS ** -0.5) * (IDX_DIM ** -0.5)
    cos_t = jnp.where(lane[None, :] < IDX_DIM, cos_i, wi_scale)
    sin_t = jnp.where(lane[None, :] < IDX_DIM, sin_i, 0.0)
    tail = _proj_rope(hn, w_tail, 0, LANES, cos_t, sin_t, IDX_DIM // 2, F32, tn=LANES)
    ki = tail[:, :IDX_DIM].astype(BF16)
    wit = tail[:, IDX_DIM:IDX_DIM + IDX_HEADS].T
    n_keep = min(TOPK_KEYS, t // 4)
    idx_t, nv = _indexer(qi.T, ki, wit, n_keep)
    att = _gattn(nv.reshape(t), idx_t.T, q.reshape(t * N_HEADS, HEAD_DIM), kv.reshape(t * KV_ROWS, LANES), n_keep)
    att = att.reshape(t, nq)
    return _mm_res(att, w_out.astype(BF16), jnp.zeros((d,), F32), x, g1)


def _conv_layer(x, mod, norm_g, w_pw1, b_pw1, w_dw, b_dw, ln_g, ln_b, w_pw2, b_pw2):
    t, d = x.shape
    sh1, sc1, g1 = mod[0:d], mod[d:2 * d], mod[2 * d:3 * d]
    hn = _normmod(x, norm_g, sc1, sh1)
    u = _glu(hn, w_pw1.astype(BF16), b_pw1)
    v = _dwconv_ln(u, w_dw, b_dw, ln_g, ln_b)
    return _mm_res(v, w_pw2.astype(BF16), b_pw2, x, g1)


def _moe_layer(x, mod, norm_g, w_rg, b_rg, w_re, b_re, w_gate, w_up, w_down, final_g, final_norm):
    t, d = x.shape
    sh2, sc2, g2 = mod[3 * d:4 * d], mod[4 * d:5 * d], mod[5 * d:6 * d]
    h, aux, counts = _route(x, norm_g, sc2, sh2, w_rg, b_rg, w_re, b_re)
    bucket = aux[:, AUX_BUCKET].astype(I32)
    rank = aux[:, AUX_RANK].astype(I32)
    sizes = counts[0, :N_BUCKETS].astype(I32)
    padded = (sizes + MOE_TM - 1) // MOE_TM * MOE_TM
    ends = jnp.cumsum(padded)
    dest = (ends - padded)[bucket] + rank
    n_blocks = t // MOE_TM + N_BUCKETS
    blk_start = jnp.arange(n_blocks, dtype=I32) * MOE_TM
    blk_bucket = jnp.minimum(jnp.searchsorted(ends, blk_start, side='right'), N_BUCKETS - 1).astype(I32)
    live = (blk_start < ends[-1]).astype(I32)
    group0 = (blk_bucket // N_PAIRS) * EXPERTS_PER_GROUP
    e_lo = group0 + jnp.asarray(PAIR_LO, I32)[blk_bucket % N_PAIRS]
    e_hi = group0 + jnp.asarray(PAIR_HI, I32)[blk_bucket % N_PAIRS]
    xs, ws = _dispatch(dest, h, aux, n_blocks * MOE_TM)
    ys = _ffn(e_lo, e_hi, live, xs, ws, w_gate.astype(BF16), w_up.astype(BF16), w_down.astype(BF16))
    return _combine(dest, x, g2, final_g, ys, final_norm)


def kernel(x, c, ada_w, ada_b, norm1_g, norm2_g, attn_w_in, attn_w_out, conv_w_pw1, conv_b_pw1, conv_w_dw, conv_b_dw, conv_ln_g, conv_ln_b, conv_w_pw2, conv_b_pw2, router_w_group, router_b_group, router_w_expert, router_b_expert, exp_w_gate, exp_w_up, exp_w_down, final_g):
    bsz, t, d = x.shape
    assert bsz == 1 and d == N_HEADS * HEAD_DIM
    depth = ada_w.shape[0]
    mods = _ada_mod(c, ada_w, ada_b)
    tables = _rope_tables(t)
    xf = x.reshape(t, d)
    for i in range(depth):
        j = i // 2
        if i % 2 == 0:
            xf = _attention_layer(xf, mods[i], norm1_g[i], attn_w_in[j], attn_w_out[j], tables)
        else:
            xf = _conv_layer(xf, mods[i], norm1_g[i], conv_w_pw1[j], conv_b_pw1[j], conv_w_dw[j], conv_b_dw[j],
                             conv_ln_g[j], conv_ln_b[j], conv_w_pw2[j], conv_b_pw2[j])
        xf = _moe_layer(xf, mods[i], norm2_g[i], router_w_group[i], router_b_group[i], router_w_expert[i],
                        router_b_expert[i], exp_w_gate[i], exp_w_up[i], exp_w_down[i], final_g, i == depth - 1)
    return xf.reshape(bsz, t, d)
```

```python
import functools

import jax
import jax.numpy as jnp
from jax import lax
from jax.experimental import pallas as pl
from jax.experimental.pallas import tpu as pltpu

EPS = 1e-6
CHUNK = 64
N_HEADS = 16
N_KV_GROUPS = 4
HEADS_PER_GROUP = N_HEADS // N_KV_GROUPS
HEAD_DIM = 128
IDX_HEADS = 16
IDX_DIM = 64
TOPK_KEYS = 256
ROPE_THETA = 10000.0
NEG_INF = -1e30
CONV_WIDTH = 31
N_GROUPS = 4
EXPERTS_PER_GROUP = 4
N_EXPERTS = N_GROUPS * EXPERTS_PER_GROUP

LANES = 128
KV_ROWS = N_KV_GROUPS
INT_MIN = -(2 ** 31)
INVALID_SHIFT = 1 << 30
VMEM_LIMIT = 56 * 1024 * 1024

F32 = jnp.float32
BF16 = jnp.bfloat16
I32 = jnp.int32
I16 = jnp.int16


def _cparams(sem):
    return pltpu.CompilerParams(dimension_semantics=sem, vmem_limit_bytes=VMEM_LIMIT)


def _ada_kernel(c_ref, w_ref, b_ref, o_ref):
    c = c_ref[...]
    ca = c * jax.nn.sigmoid(c)
    o_ref[0] = jnp.dot(ca, w_ref[0], preferred_element_type=F32, precision=lax.Precision.HIGHEST) + b_ref[0]


def _ada_mod(c, ada_w, ada_b):
    depth, d, n = ada_w.shape
    tn = 768
    c8 = jnp.zeros((8, d), F32).at[0].set(c[0])
    out = pl.pallas_call(
        _ada_kernel,
        grid=(depth, n // tn),
        in_specs=[pl.BlockSpec((8, d), lambda i, j: (0, 0)),
                  pl.BlockSpec((1, d, tn), lambda i, j: (i, 0, j)),
                  pl.BlockSpec((1, 1, tn), lambda i, j: (i, 0, j))],
        out_specs=pl.BlockSpec((1, 8, tn), lambda i, j: (i, 0, j)),
        out_shape=jax.ShapeDtypeStruct((depth, 8, n), F32),
        compiler_params=_cparams(("arbitrary", "arbitrary")),
        name="ada_mod",
    )(c8, ada_w, ada_b.reshape(depth, 1, n))
    return out[:, 0, :]


def _normmod_kernel(x_ref, g_ref, sc_ref, sh_ref, o_ref):
    x = x_ref[...]
    xn = x * lax.rsqrt(jnp.mean(x * x, axis=-1, keepdims=True) + EPS)
    o_ref[...] = ((xn * g_ref[...]) * (1.0 + sc_ref[...]) + sh_ref[...]).astype(o_ref.dtype)


def _normmod(x, g, sc, sh, tm=512):
    t, d = x.shape
    vec = pl.BlockSpec((1, d), lambda i: (0, 0))
    return pl.pallas_call(
        _normmod_kernel,
        grid=(t // tm,),
        in_specs=[pl.BlockSpec((tm, d), lambda i: (i, 0)), vec, vec, vec],
        out_specs=pl.BlockSpec((tm, d), lambda i: (i, 0)),
        out_shape=jax.ShapeDtypeStruct((t, d), BF16),
        compiler_params=_cparams(("arbitrary",)),
        name="normmod",
    )(x, g.reshape(1, d), sc.reshape(1, d), sh.reshape(1, d))


N_PAIRS = EXPERTS_PER_GROUP * (EXPERTS_PER_GROUP - 1) // 2
N_BUCKETS = N_GROUPS * N_PAIRS
PAIR_LO = (0, 0, 0, 1, 1, 2)
PAIR_HI = (1, 2, 3, 2, 3, 3)
MOE_TM = 256
AUX_W_LO, AUX_W_HI, AUX_BUCKET, AUX_RANK = 0, 1, 2, 3


def _route_kernel(x_ref, g_ref, sc_ref, sh_ref, wr_ref, br_ref, tri_ref, h_ref, aux_ref, cnt_ref, run_ref):
    @pl.when(pl.program_id(0) == 0)
    def _():
        run_ref[...] = jnp.zeros_like(run_ref)

    x = x_ref[...]
    xn = x * lax.rsqrt(jnp.mean(x * x, axis=-1, keepdims=True) + EPS)
    h = (xn * g_ref[...]) * (1.0 + sc_ref[...]) + sh_ref[...]
    h_ref[...] = h.astype(h_ref.dtype)
    h_hi = h.astype(BF16)
    h_lo = (h - h_hi.astype(F32)).astype(BF16)
    w = wr_ref[...]
    w_hi = w.astype(BF16)
    w_lo = (w - w_hi.astype(F32)).astype(BF16)
    logits = (jnp.dot(h_hi, w_hi, preferred_element_type=F32) + jnp.dot(h_lo, w_hi, preferred_element_type=F32)
              + jnp.dot(h_hi, w_lo, preferred_element_type=F32)) + br_ref[...]
    tm = x.shape[0]
    lane4 = lax.broadcasted_iota(I32, (tm, N_GROUPS), 1)

    def first_argmax(v):
        m = jnp.max(v, axis=-1, keepdims=True)
        return m, jnp.min(jnp.where(v == m, lane4, N_GROUPS), axis=-1, keepdims=True)

    gl = logits[:, 0:N_GROUPS]
    gmax, gidx = first_argmax(gl)
    g_w = 1.0 / jnp.sum(jnp.exp(gl - gmax), axis=-1, keepdims=True)
    el = jnp.zeros((tm, EXPERTS_PER_GROUP), F32)
    for g in range(N_GROUPS):
        lo = N_GROUPS + g * EXPERTS_PER_GROUP
        el = jnp.where(gidx == g, logits[:, lo:lo + EXPERTS_PER_GROUP], el)
    emax = jnp.max(el, axis=-1, keepdims=True)
    ee = jnp.exp(el - emax)
    ep = ee / jnp.sum(ee, axis=-1, keepdims=True)
    p1, i1 = first_argmax(ep)
    rest = jnp.where(lane4 == i1, -1.0, ep)
    p2, i2 = first_argmax(rest)
    den = p1 + p2
    w1 = g_w * p1 / den
    w2 = g_w * p2 / den
    first_lo = i1 < i2
    a = jnp.minimum(i1, i2)
    b = jnp.maximum(i1, i2)
    w_lo = jnp.where(first_lo, w1, w2)
    w_hi = jnp.where(first_lo, w2, w1)
    pair = jnp.where(a == 0, b - 1, jnp.where(a == 1, b + 1, N_PAIRS - 1))
    bucket = gidx * N_PAIRS + pair
    lane = lax.broadcasted_iota(I32, (tm, LANES), 1)
    onehot = lane == bucket
    prefix = jnp.dot(tri_ref[...], onehot.astype(F32).astype(BF16), preferred_element_type=F32)
    run = run_ref[...]
    rank = jnp.sum(jnp.where(onehot, prefix + run, 0.0), axis=1, keepdims=True) - 1.0
    run = run + prefix[tm - 1:tm, :]
    run_ref[...] = run
    cnt_ref[...] = run
    aux_ref[...] = (jnp.where(lane == AUX_W_LO, w_lo, 0.0) + jnp.where(lane == AUX_W_HI, w_hi, 0.0)
                    + jnp.where(lane == AUX_BUCKET, bucket.astype(F32), 0.0) + jnp.where(lane == AUX_RANK, rank, 0.0))


def _route(x, g, sc, sh, w_rg, b_rg, w_re, b_re, tm=512):
    t, d = x.shape
    w_all = jnp.concatenate([w_rg, jnp.moveaxis(w_re, 0, 1).reshape(d, N_EXPERTS)], axis=1)
    b_all = jnp.concatenate([b_rg, b_re.reshape(N_EXPERTS)])
    npad = LANES - w_all.shape[1]
    w_all = jnp.pad(w_all, ((0, 0), (0, npad)))
    b_all = jnp.pad(b_all, (0, npad)).reshape(1, LANES)
    vec = pl.BlockSpec((1, d), lambda i: (0, 0))
    r = lax.broadcasted_iota(I32, (tm, tm), 0)
    c = lax.broadcasted_iota(I32, (tm, tm), 1)
    tri = (c <= r).astype(BF16)
    return pl.pallas_call(
        _route_kernel,
        grid=(t // tm,),
        in_specs=[pl.BlockSpec((tm, d), lambda i: (i, 0)), vec, vec, vec,
                  pl.BlockSpec((d, LANES), lambda i: (0, 0)), pl.BlockSpec((1, LANES), lambda i: (0, 0)),
                  pl.BlockSpec((tm, tm), lambda i: (0, 0))],
        out_specs=[pl.BlockSpec((tm, d), lambda i: (i, 0)), pl.BlockSpec((tm, LANES), lambda i: (i, 0)),
                   pl.BlockSpec((1, LANES), lambda i: (0, 0))],
        out_shape=[jax.ShapeDtypeStruct((t, d), F32), jax.ShapeDtypeStruct((t, LANES), F32),
                   jax.ShapeDtypeStruct((1, LANES), F32)],
        scratch_shapes=[pltpu.VMEM((1, LANES), F32)],
        compiler_params=_cparams(("arbitrary",)),
        name="route",
    )(x, g.reshape(1, d), sc.reshape(1, d), sh.reshape(1, d), w_all, b_all, tri)


def _rope_block(blk, cos, sin, half):
    if half == HEAD_DIM // 2:
        partner = pltpu.roll(blk, HEAD_DIM // 2, 1)
    else:
        lane = lax.broadcasted_iota(I32, blk.shape, 1)
        first = (lane % IDX_DIM) < (IDX_DIM // 2)
        partner = jnp.where(first, pltpu.roll(blk, LANES - IDX_DIM // 2, 1), pltpu.roll(blk, IDX_DIM // 2, 1))
    return blk * cos + partner * sin


def _proj_rope_kernel(a_ref, w_ref, cos_ref, sin_ref, o_ref, *, half):
    acc = jnp.dot(a_ref[...], w_ref[...], preferred_element_type=F32)
    cos = cos_ref[...]
    sin = sin_ref[...]
    for h in range(acc.shape[1] // LANES):
        sl = slice(h * LANES, (h + 1) * LANES)
        o_ref[:, sl] = _rope_block(acc[:, sl], cos, sin, half).astype(o_ref.dtype)


def _proj_rope(a, w, col_block0, n_out, cos, sin, half, out_dtype, tm=512, tn=512):
    t, d = a.shape
    return pl.pallas_call(
        functools.partial(_proj_rope_kernel, half=half),
        grid=(t // tm, n_out // tn),
        in_specs=[pl.BlockSpec((tm, d), lambda i, j: (i, 0)),
                  pl.BlockSpec((d, tn), lambda i, j: (0, col_block0 + j)),
                  pl.BlockSpec((tm, LANES), lambda i, j: (i, 0)),
                  pl.BlockSpec((tm, LANES), lambda i, j: (i, 0))],
        out_specs=pl.BlockSpec((tm, tn), lambda i, j: (i, j)),
        out_shape=jax.ShapeDtypeStruct((t, n_out), out_dtype),
        compiler_params=_cparams(("arbitrary", "arbitrary")),
        name="proj_rope",
    )(a, w, cos, sin)


def _proj_kv_kernel(a_ref, wk_ref, wv_ref, cos_ref, sin_ref, o_ref):
    a = a_ref[...]
    k = jnp.dot(a, wk_ref[...], preferred_element_type=F32)
    v = jnp.dot(a, wv_ref[...], preferred_element_type=F32)
    cos = cos_ref[...]
    sin = sin_ref[...]
    for h in range(k.shape[1] // LANES):
        sl = slice(h * LANES, (h + 1) * LANES)
        kr = _rope_block(k[:, sl], cos, sin, HEAD_DIM // 2)
        kb = lax.bitcast_convert_type(kr.astype(BF16).astype(F32), I32)
        vb = lax.bitcast_convert_type(v[:, sl].astype(BF16).astype(F32), I32)
        o_ref[:, sl] = kb | lax.shift_right_logical(vb, 16)


def _proj_kv(a, w, kblock, vblock, cos, sin, tm=512):
    t, d = a.shape
    n = N_KV_GROUPS * HEAD_DIM
    return pl.pallas_call(
        _proj_kv_kernel,
        grid=(t // tm,),
        in_specs=[pl.BlockSpec((tm, d), lambda i: (i, 0)),
                  pl.BlockSpec((d, n), lambda i: (0, kblock)),
                  pl.BlockSpec((d, n), lambda i: (0, vblock)),
                  pl.BlockSpec((tm, LANES), lambda i: (i, 0)),
                  pl.BlockSpec((tm, LANES), lambda i: (i, 0))],
        out_specs=pl.BlockSpec((tm, n), lambda i: (i, 0)),
        out_shape=jax.ShapeDtypeStruct((t, n), I32),
        compiler_params=_cparams(("arbitrary",)),
        name="proj_kv",
    )(a, w, w, cos, sin)


QB = 256
TK = 512
RC = 64


def _indexer_kernel(qit_ref, ki_ref, wit_ref, tri_ref, idx_ref, nv_ref, d_ref, hi_ref, *, n_keep):
    b = pl.program_id(0)
    q0 = b * QB
    n_t = (q0 + QB + TK - 1) // TK
    n_rows = n_t * TK
    rowi = lax.broadcasted_iota(I32, (TK, QB), 0)
    qchunk = (q0 + lax.broadcasted_iota(I32, (TK, QB), 1)) // CHUNK

    def admissible(k0):
        return ((k0 + rowi) // CHUNK) <= qchunk

    def score_tile(kt, carry):
        k0 = pl.multiple_of(kt * TK, TK)
        ki_t = ki_ref[pl.ds(k0, TK), :]
        acc = jnp.zeros((TK, QB), F32)
        for h in range(IDX_HEADS):
            a = jnp.dot(ki_t, qit_ref[h * IDX_DIM:(h + 1) * IDX_DIM, :], preferred_element_type=F32)
            acc = acc + jnp.maximum(a, 0.0) * wit_ref[h:h + 1, :]
        s = jnp.where(admissible(k0), acc + 0.0, NEG_INF)
        u = lax.bitcast_convert_type(s, I32)
        key = jnp.where(u < 0, u ^ 0x7FFFFFFF, u)
        d_ref[pl.ds(k0, TK), :] = key
        hi_ref[pl.ds(k0, TK), :] = lax.shift_right_arithmetic(key, 16).astype(I16)
        return carry

    lax.fori_loop(0, n_t, score_tile, 0)

    def count_where(pred):
        def body(kt, acc):
            k0 = pl.multiple_of(kt * TK, TK)
            hit = pred(d_ref[pl.ds(k0, TK), :]).astype(I32)
            return acc + jnp.sum(hit.reshape(TK // 8, 8, QB), axis=0)
        acc = lax.fori_loop(0, n_t, body, jnp.zeros((8, QB), I32))
        return jnp.sum(acc, axis=0, keepdims=True)

    def signed16(v_u):
        return lax.shift_right_arithmetic(jnp.left_shift(v_u ^ 0x8000, 16), 16).astype(I16)

    def add_packed_rows(acc, hit):
        for c in range(TK // 16):
            acc = acc + hit[c * 16:(c + 1) * 16, :]
        return acc

    def count_half_where(pred):
        def body(kt, acc):
            k0 = pl.multiple_of(kt * TK, TK)
            return add_packed_rows(acc, pred(hi_ref[pl.ds(k0, TK), :]).astype(I16))
        acc = lax.fori_loop(0, n_t, body, jnp.zeros((16, QB), I16))
        return jnp.sum(acc.astype(I32), axis=0, keepdims=True)

    def select_half(base):
        def select_bit(it, carry):
            v_u, cnt_v = carry
            cand_u = v_u | jnp.left_shift(jnp.int32(1), 15 - it)
            cand16 = signed16(cand_u)
            cnt = base + count_half_where(lambda half: half >= cand16)
            keep = cnt >= n_keep
            return jnp.where(keep, cand_u, v_u), jnp.where(keep, cnt, cnt_v)
        return select_bit

    zero_i = jnp.zeros((1, QB), I32)
    high_u, cnt_thr = lax.fori_loop(0, 16, select_half(zero_i), (zero_i, jnp.full((1, QB), n_keep + 1, I32)))
    high16 = signed16(high_u)

    def to_low_halves(kt, acc):
        k0 = pl.multiple_of(kt * TK, TK)
        high = hi_ref[pl.ds(k0, TK), :]
        low = signed16(d_ref[pl.ds(k0, TK), :] & 0xFFFF)
        hi_ref[pl.ds(k0, TK), :] = jnp.where(high == high16, low, jnp.int16(-(2 ** 15)))
        return add_packed_rows(acc, (high > high16).astype(I16))

    above = lax.fori_loop(0, n_t, to_low_halves, jnp.zeros((16, QB), I16))
    above = jnp.sum(above.astype(I32), axis=0, keepdims=True)
    low_u, cnt_thr = lax.fori_loop(0, 16, select_half(above), (zero_i, cnt_thr))
    thr = (jnp.left_shift(high_u, 16) | low_u) ^ INT_MIN

    tri = tri_ref[...]
    zero = jnp.zeros((1, QB), F32)

    def place(k0, sel, run_sel):
        pre_sel = jnp.dot(tri, sel.astype(F32).astype(BF16), preferred_element_type=F32)
        rank = (run_sel + pre_sel).astype(I32) - 1
        d_ref[pl.ds(k0, TK), :] = jnp.where(sel, (k0 + rowi) - rank, INVALID_SHIFT)
        return run_sel + pre_sel[TK - 1:TK, :]

    def rank_exact_count():
        def tile(kt, run_sel):
            k0 = pl.multiple_of(kt * TK, TK)
            return place(k0, admissible(k0) & (d_ref[pl.ds(k0, TK), :] >= thr), run_sel)
        return lax.fori_loop(0, n_t, tile, zero)

    def rank_with_ties():
        need = (n_keep - count_where(lambda key: key > thr)).astype(F32)

        def tile(kt, carry):
            run_eq, run_sel = carry
            k0 = pl.multiple_of(kt * TK, TK)
            key = d_ref[pl.ds(k0, TK), :]
            eq = key == thr
            eq_f = eq.astype(F32)
            pre_eq = jnp.dot(tri, eq_f.astype(BF16), preferred_element_type=F32)
            sel = admissible(k0) & ((key > thr) | (eq & ((run_eq + pre_eq - eq_f) < need)))
            return run_eq + pre_eq[TK - 1:TK, :], place(k0, sel, run_sel)
        return lax.fori_loop(0, n_t, tile, (zero, zero))[1]

    n_sel = lax.cond(jnp.max(jnp.abs(cnt_thr - n_keep)) == 0, rank_exact_count, rank_with_ties)

    d_ref[pl.ds(pl.multiple_of(n_rows, RC), RC), :] = jnp.full((RC, QB), INVALID_SHIFT, I32)
    n_chunks = n_rows // RC
    win_chunks = -(-n_keep // RC)
    window = win_chunks * RC
    n_stages = (d_ref.shape[0] - RC - 1).bit_length()
    for s in range(n_stages):
        sh = 1 << s
        test = sh | INVALID_SHIFT

        def merge(own, inc, sh=sh, test=test):
            take = (inc & test) == sh
            stay = (own & test) == 0
            return jnp.where(take, inc, jnp.where(stay, own, INVALID_SHIFT))

        def chunk_with_source(c, carry, sh=sh, merge=merge):
            r0 = pl.multiple_of(c * RC, RC)
            if sh < 8:
                x = d_ref[pl.ds(r0, RC + 8), :]
                own, inc = x[0:RC], x[sh:sh + RC]
            else:
                own = d_ref[pl.ds(r0, RC), :]
                inc = d_ref[pl.ds(pl.multiple_of(r0 + sh, 8), RC), :]
            d_ref[pl.ds(r0, RC), :] = merge(own, inc)
            return carry

        def chunk_without_source(c, carry, merge=merge):
            r0 = pl.multiple_of(c * RC, RC)
            own = d_ref[pl.ds(r0, RC), :]
            d_ref[pl.ds(r0, RC), :] = merge(own, jnp.full_like(own, INVALID_SHIFT))
            return carry

        def window_chunk(c, carry, sh=sh, merge=merge):
            r0 = pl.multiple_of((c // win_chunks) * sh + (c % win_chunks) * RC, RC)
            src = r0 + sh
            own = d_ref[pl.ds(r0, RC), :]
            inc = d_ref[pl.ds(pl.multiple_of(jnp.minimum(src, n_rows), RC), RC), :]
            d_ref[pl.ds(r0, RC), :] = merge(own, inc)
            return carry

        if sh < RC:
            lax.fori_loop(0, n_chunks, chunk_with_source, 0)
        elif sh < 2 * window:
            n_src = jnp.maximum(n_rows - sh, 0) // RC
            lax.fori_loop(0, n_src, chunk_with_source, 0)
            lax.fori_loop(n_src, n_chunks, chunk_without_source, 0)
        else:
            lax.fori_loop(0, ((n_rows + sh - 1) // sh) * win_chunks, window_chunk, 0)

    slot = lax.broadcasted_iota(I32, (n_keep, QB), 0)
    dd = d_ref[0:n_keep, :]
    idx_ref[...] = jnp.where(dd < INVALID_SHIFT, (slot + dd) * KV_ROWS, 0)
    nv_ref[...] = n_sel.astype(I32)


def _indexer(qit, ki, wit, n_keep):
    t = ki.shape[0]
    r = lax.broadcasted_iota(I32, (TK, TK), 0)
    c = lax.broadcasted_iota(I32, (TK, TK), 1)
    tri = (c <= r).astype(BF16)
    return pl.pallas_call(
        functools.partial(_indexer_kernel, n_keep=n_keep),
        grid=(t // QB,),
        in_specs=[pl.BlockSpec((IDX_HEADS * IDX_DIM, QB), lambda b: (0, b)),
                  pl.BlockSpec((t, IDX_DIM), lambda b: (0, 0)),
                  pl.BlockSpec((IDX_HEADS, QB), lambda b: (0, b)),
                  pl.BlockSpec((TK, TK), lambda b: (0, 0))],
        out_specs=[pl.BlockSpec((n_keep, QB), lambda b: (0, b)), pl.BlockSpec((1, QB), lambda b: (0, b))],
        out_shape=[jax.ShapeDtypeStruct((n_keep, t), I32), jax.ShapeDtypeStruct((1, t), I32)],
        scratch_shapes=[pltpu.VMEM((t + RC, QB), I32), pltpu.VMEM((t, QB), I16)],
        compiler_params=_cparams(("arbitrary",)),
        name="indexer",
    )(qit, ki, wit, tri)


GQ = 64
GATTN_UNROLL = 8


def _gattn_kernel(nv_ref, idx_ref, q_ref, kv_hbm, o_ref, kv_vmem, stage_a, stage_b, stage_c, stage_d,
                  s_a, s_b, row_lists, sem_kv, sem_rows, *, n_keep):
    i = pl.program_id(0)

    @pl.when(i == 0)
    def _():
        cp = pltpu.make_async_copy(kv_hbm, kv_vmem, sem_kv)
        cp.start()
        cp.wait()

    cp = pltpu.make_async_copy(idx_ref, row_lists, sem_rows)
    cp.start()
    cp.wait()

    n_col = n_keep * KV_ROWS
    colg = lax.broadcasted_iota(I32, (N_HEADS, n_col), 1)
    rowh = lax.broadcasted_iota(I32, (N_HEADS, n_col), 0)
    own_group = (colg % KV_ROWS) == (rowh // HEADS_PER_GROUP)
    col_slot = colg // KV_ROWS
    scale = HEAD_DIM ** -0.5

    last = GQ - 1
    stages = (stage_a, stage_b, stage_c, stage_d)
    s_bufs = (s_a, s_b)

    def gather(r, stage):
        rows = row_lists.at[pl.ds(jnp.minimum(r, last), 1)]
        for j in range(n_keep):
            row = pl.multiple_of(rows[0, j], KV_ROWS)
            stage[j * KV_ROWS:(j + 1) * KV_ROWS, :] = kv_vmem[pl.ds(row, KV_ROWS), :]

    def scores(r, stage, s_buf):
        k_all = lax.bitcast_convert_type(stage[...] & jnp.int32(-65536), F32).astype(BF16)
        qh = q_ref[pl.ds(pl.multiple_of(r * N_HEADS, N_HEADS), N_HEADS), :]
        s = lax.dot_general(qh, k_all, (((1,), (1,)), ((), ())), preferred_element_type=F32) * scale
        mask = own_group & (col_slot < nv_ref[i * GQ + r])
        s_buf[...] = jnp.where(mask, s, NEG_INF)

    def combine(r, stage, s_buf):
        s = s_buf[...]
        p = jnp.exp(s - jnp.max(s, axis=1, keepdims=True))
        l = jnp.sum(p, axis=1, keepdims=True)
        v_all = lax.bitcast_convert_type(jnp.left_shift(stage[...], 16), F32).astype(BF16)
        o = jnp.dot(p.astype(BF16), v_all, preferred_element_type=F32) / l
        o_ref[pl.ds(pl.multiple_of(r * N_HEADS, N_HEADS), N_HEADS), :] = o.astype(o_ref.dtype)

    gather(0, stages[0])
    gather(1, stages[1])
    scores(0, stages[0], s_bufs[0])

    def query_group(it, carry):
        r = GATTN_UNROLL * it
        for u in range(GATTN_UNROLL):
            gather(r + u + 2, stages[(u + 2) % 4])
            scores(jnp.minimum(r + u + 1, last), stages[(u + 1) % 4], s_bufs[(u + 1) % 2])
            combine(r + u, stages[u % 4], s_bufs[u % 2])
        return carry

    lax.fori_loop(0, GQ // GATTN_UNROLL, query_group, 0)


def _gattn(nv, idx, q2, kv2, n_keep):
    t = nv.shape[0]
    grid_spec = pltpu.PrefetchScalarGridSpec(
        num_scalar_prefetch=1,
        grid=(t // GQ,),
        in_specs=[pl.BlockSpec((GQ, n_keep), lambda i, nv: (i, 0)),
                  pl.BlockSpec((GQ * N_HEADS, HEAD_DIM), lambda i, nv: (i, 0)),
                  pl.BlockSpec(memory_space=pl.ANY)],
        out_specs=pl.BlockSpec((GQ * N_HEADS, HEAD_DIM), lambda i, nv: (i, 0)),
        scratch_shapes=[pltpu.VMEM(kv2.shape, I32),
                        pltpu.VMEM((n_keep * KV_ROWS, LANES), I32),
                        pltpu.VMEM((n_keep * KV_ROWS, LANES), I32),
                        pltpu.VMEM((n_keep * KV_ROWS, LANES), I32),
                        pltpu.VMEM((n_keep * KV_ROWS, LANES), I32),
                        pltpu.VMEM((N_HEADS, n_keep * KV_ROWS), F32),
                        pltpu.VMEM((N_HEADS, n_keep * KV_ROWS), F32),
                        pltpu.SMEM((GQ, n_keep), I32),
                        pltpu.SemaphoreType.DMA,
                        pltpu.SemaphoreType.DMA],
    )
    return pl.pallas_call(
        functools.partial(_gattn_kernel, n_keep=n_keep),
        grid_spec=grid_spec,
        out_shape=jax.ShapeDtypeStruct((t * N_HEADS, HEAD_DIM), BF16),
        compiler_params=_cparams(("arbitrary",)),
        name="gattn",
    )(nv, idx, q2, kv2)


def _mm_res_kernel(a_ref, w_ref, b_ref, x_ref, g_ref, o_ref):
    y = jnp.dot(a_ref[...], w_ref[...], preferred_element_type=F32) + b_ref[...]
    o_ref[...] = x_ref[...] + g_ref[...] * y


def _mm_res(a, w, bias, x, gate, tm=512, tn=512):
    t, k = a.shape
    n = w.shape[1]
    return pl.pallas_call(
        _mm_res_kernel,
        grid=(t // tm, n // tn),
        in_specs=[pl.BlockSpec((tm, k), lambda i, j: (i, 0)),
                  pl.BlockSpec((k, tn), lambda i, j: (0, j)),
                  pl.BlockSpec((1, tn), lambda i, j: (0, j)),
                  pl.BlockSpec((tm, tn), lambda i, j: (i, j)),
                  pl.BlockSpec((1, tn), lambda i, j: (0, j))],
        out_specs=pl.BlockSpec((tm, tn), lambda i, j: (i, j)),
        out_shape=jax.ShapeDtypeStruct((t, n), F32),
        compiler_params=_cparams(("arbitrary", "arbitrary")),
        name="mm_res",
    )(a, w, bias.reshape(1, n), x, gate.reshape(1, n))


def _glu_kernel(a_ref, wa_ref, wg_ref, ba_ref, bg_ref, o_ref):
    a = a_ref[...]
    ya = jnp.dot(a, wa_ref[...], preferred_element_type=F32) + ba_ref[...]
    yg = jnp.dot(a, wg_ref[...], preferred_element_type=F32) + bg_ref[...]
    o_ref[...] = ya * jax.nn.sigmoid(yg)


def _glu(a, w, b, tm=512, tn=512):
    t, k = a.shape
    n = w.shape[1] // 2
    nb = n // tn
    b2 = b.reshape(1, 2 * n)
    return pl.pallas_call(
        _glu_kernel,
        grid=(t // tm, nb),
        in_specs=[pl.BlockSpec((tm, k), lambda i, j: (i, 0)),
                  pl.BlockSpec((k, tn), lambda i, j: (0, j)),
                  pl.BlockSpec((k, tn), lambda i, j: (0, j + nb)),
                  pl.BlockSpec((1, tn), lambda i, j: (0, j)),
                  pl.BlockSpec((1, tn), lambda i, j: (0, j + nb))],
        out_specs=pl.BlockSpec((tm, tn), lambda i, j: (i, j)),
        out_shape=jax.ShapeDtypeStruct((t, n), F32),
        compiler_params=_cparams(("arbitrary", "arbitrary")),
        name="glu",
    )(a, w, w, b2, b2)


HALO = 32


SUBLANES = 8
CONV_ROWS = 32
CONV_COLS = 512


def _dwconv_ln_kernel(u_ref, prev_ref, w_ref, b_ref, g_ref, beta_ref, o_ref, ext_ref, sh_ref, conv_ref):
    i = pl.program_id(0)
    tm, d = u_ref.shape
    ext_ref[0:HALO, :] = jnp.where(i == 0, 0.0, prev_ref[...])
    ext_ref[HALO:HALO + tm, :] = u_ref[...]
    for phase in range(SUBLANES):
        sh_ref[phase, 0:tm + HALO - phase, :] = ext_ref[phase:tm + HALO, :]
    first = HALO - (CONV_WIDTH - 1)

    def conv_rows(c, carry):
        r0 = pl.multiple_of(c * CONV_ROWS, CONV_ROWS)
        for cb in range(d // CONV_COLS):
            cols = slice(cb * CONV_COLS, (cb + 1) * CONV_COLS)
            acc = jnp.zeros((CONV_ROWS, CONV_COLS), F32) + b_ref[:, cols]
            for k in range(CONV_WIDTH):
                phase = (first + k) % SUBLANES
                start = pl.multiple_of(r0 + (first + k - phase), SUBLANES)
                acc = acc + sh_ref[phase, pl.ds(start, CONV_ROWS), cols] * w_ref[k:k + 1, cols]
            conv_ref[pl.ds(r0, CONV_ROWS), cols] = acc
        return carry

    lax.fori_loop(0, tm // CONV_ROWS, conv_rows, 0)
    acc = conv_ref[...]
    mu = jnp.mean(acc, axis=-1, keepdims=True)
    cen = acc - mu
    var = jnp.mean(cen * cen, axis=-1, keepdims=True)
    y = (cen * lax.rsqrt(var + EPS)) * g_ref[...] + beta_ref[...]
    o_ref[...] = (y * jax.nn.sigmoid(y)).astype(o_ref.dtype)


def _dwconv_ln(u, w_dw, b_dw, ln_g, ln_b, tm=256):
    t, d = u.shape
    ratio = tm // HALO
    vec = pl.BlockSpec((1, d), lambda i: (0, 0))
    return pl.pallas_call(
        _dwconv_ln_kernel,
        grid=(t // tm,),
        in_specs=[pl.BlockSpec((tm, d), lambda i: (i, 0)),
                  pl.BlockSpec((HALO, d), lambda i: (jnp.maximum(i * ratio - 1, 0), 0)),
                  pl.BlockSpec((HALO, d), lambda i: (0, 0)),
                  vec, vec, vec],
        out_specs=pl.BlockSpec((tm, d), lambda i: (i, 0)),
        out_shape=jax.ShapeDtypeStruct((t, d), BF16),
        scratch_shapes=[pltpu.VMEM((HALO + tm, d), F32), pltpu.VMEM((SUBLANES, HALO + tm, d), F32),
                        pltpu.VMEM((tm, d), F32)],
        compiler_params=_cparams(("arbitrary",)),
        name="dwconv_ln",
    )(u, u, jnp.pad(w_dw, ((0, HALO - CONV_WIDTH), (0, 0))), b_dw.reshape(1, d), ln_g.reshape(1, d), ln_b.reshape(1, d))


def _row_copy(src, src_row, dst, dst_row, sem):
    return pltpu.make_async_copy(src.at[pl.ds(src_row, 1), :], dst.at[pl.ds(dst_row, 1), :], sem)


ROW_UNROLL = 8


def _for_rows(n, fn):
    def group(gi, carry):
        for u in range(ROW_UNROLL):
            fn(gi * ROW_UNROLL + u, carry)
        return carry

    lax.fori_loop(0, n // ROW_UNROLL, group, 0)


def _dispatch_kernel(dest_ref, h_ref, aux_ref, xs_in, ws_in, xs_hbm, ws_hbm, sem_x, sem_w):
    del xs_in, ws_in
    tm = h_ref.shape[0]
    t0 = pl.program_id(0) * tm

    def issue(r, carry):
        slot = dest_ref[t0 + r]
        _row_copy(h_ref, r, xs_hbm, slot, sem_x).start()
        _row_copy(aux_ref, r, ws_hbm, slot, sem_w).start()
        return carry

    def drain(r, carry):
        _row_copy(h_ref, r, xs_hbm, 0, sem_x).wait()
        _row_copy(aux_ref, r, ws_hbm, 0, sem_w).wait()
        return carry

    _for_rows(tm, issue)
    _for_rows(tm, drain)


def _dispatch(dest, h, aux, n_rows, tm=256):
    t, d = h.shape
    grid_spec = pltpu.PrefetchScalarGridSpec(
        num_scalar_prefetch=1,
        grid=(t // tm,),
        in_specs=[pl.BlockSpec((tm, d), lambda i, dest: (i, 0)),
                  pl.BlockSpec((tm, LANES), lambda i, dest: (i, 0)),
                  pl.BlockSpec(memory_space=pl.ANY),
                  pl.BlockSpec(memory_space=pl.ANY)],
        out_specs=[pl.BlockSpec(memory_space=pl.ANY), pl.BlockSpec(memory_space=pl.ANY)],
        scratch_shapes=[pltpu.SemaphoreType.DMA, pltpu.SemaphoreType.DMA],
    )
    return pl.pallas_call(
        _dispatch_kernel,
        grid_spec=grid_spec,
        out_shape=[jax.ShapeDtypeStruct((n_rows, d), F32), jax.ShapeDtypeStruct((n_rows, LANES), F32)],
        input_output_aliases={3: 0, 4: 1},
        compiler_params=_cparams(("arbitrary",)),
        name="dispatch",
    )(dest, h, aux, jnp.zeros((n_rows, d), F32), jnp.zeros((n_rows, LANES), F32))


def _ffn_kernel(elo_ref, ehi_ref, live_ref, xs_ref, ws_ref, wg_lo, wu_lo, wd_lo, wg_hi, wu_hi, wd_hi, ys_ref):
    b = pl.program_id(0)

    @pl.when(live_ref[b] == 1)
    def _():
        x = xs_ref[...].astype(BF16)
        ws = ws_ref[...]

        def expert(wg, wu, wd):
            gate = jnp.dot(x, wg[0], preferred_element_type=F32)
            up = jnp.dot(x, wu[0], preferred_element_type=F32)
            hid = (gate * jax.nn.sigmoid(gate)) * up
            return jnp.dot(hid.astype(BF16), wd[0], preferred_element_type=F32)

        ys_ref[...] = (expert(wg_lo, wu_lo, wd_lo) * ws[:, AUX_W_LO:AUX_W_LO + 1]
                       + expert(wg_hi, wu_hi, wd_hi) * ws[:, AUX_W_HI:AUX_W_HI + 1])

    @pl.when(live_ref[b] == 0)
    def _():
        ys_ref[...] = jnp.zeros_like(ys_ref)


def _ffn(e_lo, e_hi, live, xs, ws, w_gate, w_up, w_down):
    n_rows, d = xs.shape
    _, _, de = w_gate.shape
    lo_in = pl.BlockSpec((1, d, de), lambda b, lo, hi, live: (lo[b], 0, 0))
    hi_in = pl.BlockSpec((1, d, de), lambda b, lo, hi, live: (hi[b], 0, 0))
    lo_out = pl.BlockSpec((1, de, d), lambda b, lo, hi, live: (lo[b], 0, 0))
    hi_out = pl.BlockSpec((1, de, d), lambda b, lo, hi, live: (hi[b], 0, 0))
    grid_spec = pltpu.PrefetchScalarGridSpec(
        num_scalar_prefetch=3,
        grid=(n_rows // MOE_TM,),
        in_specs=[pl.BlockSpec((MOE_TM, d), lambda b, lo, hi, live: (b, 0)),
                  pl.BlockSpec((MOE_TM, LANES), lambda b, lo, hi, live: (b, 0)),
                  lo_in, lo_in, lo_out, hi_in, hi_in, hi_out],
        out_specs=pl.BlockSpec((MOE_TM, d), lambda b, lo, hi, live: (b, 0)),
    )
    return pl.pallas_call(
        _ffn_kernel,
        grid_spec=grid_spec,
        out_shape=jax.ShapeDtypeStruct((n_rows, d), F32),
        compiler_params=_cparams(("arbitrary",)),
        name="ffn",
    )(e_lo, e_hi, live, xs, ws, w_gate, w_up, w_down, w_gate, w_up, w_down)


def _combine_kernel(dest_ref, x_ref, g_ref, fg_ref, ys_hbm, o_ref, buf, sem, *, final_norm):
    tm = x_ref.shape[0]
    t0 = pl.program_id(0) * tm

    def issue(r, carry):
        _row_copy(ys_hbm, dest_ref[t0 + r], buf, r, sem).start()
        return carry

    def drain(r, carry):
        _row_copy(ys_hbm, 0, buf, r, sem).wait()
        return carry

    _for_rows(tm, issue)
    _for_rows(tm, drain)
    y = x_ref[...] + g_ref[...] * buf[...]
    if final_norm:
        y = (y * lax.rsqrt(jnp.mean(y * y, axis=-1, keepdims=True) + EPS)) * fg_ref[...]
    o_ref[...] = y


def _combine(dest, x, gate, final_g, ys, final_norm, tm=256):
    t, d = x.shape
    vec = pl.BlockSpec((1, d), lambda i, dest: (0, 0))
    grid_spec = pltpu.PrefetchScalarGridSpec(
        num_scalar_prefetch=1,
        grid=(t // tm,),
        in_specs=[pl.BlockSpec((tm, d), lambda i, dest: (i, 0)), vec, vec, pl.BlockSpec(memory_space=pl.ANY)],
        out_specs=pl.BlockSpec((tm, d), lambda i, dest: (i, 0)),
        scratch_shapes=[pltpu.VMEM((tm, d), F32), pltpu.SemaphoreType.DMA],
    )
    return pl.pallas_call(
        functools.partial(_combine_kernel, final_norm=final_norm),
        grid_spec=grid_spec,
        out_shape=jax.ShapeDtypeStruct((t, d), F32),
        compiler_params=_cparams(("arbitrary",)),
        name="combine",
    )(dest, x, gate.reshape(1, d), final_g.reshape(1, d), ys)


def _rope_tables(t):
    pos = jnp.arange(t, dtype=I32).astype(F32)[:, None]
    lane = jnp.arange(LANES)

    def table(dim):
        half = dim // 2
        inv = ROPE_THETA ** (-jnp.arange(half, dtype=F32) / half)
        ang = pos * inv[None, :]
        j = lane % dim
        cos = jnp.cos(ang)[:, j % half]
        sin = jnp.sin(ang)[:, j % half] * jnp.where(j < half, -1.0, 1.0)[None, :]
        return cos, sin

    return table(HEAD_DIM), table(IDX_DIM)


def _attention_layer(x, mod, norm_g, w_in, w_out, tables):
    t, d = x.shape
    sh1, sc1, g1 = mod[0:d], mod[d:2 * d], mod[2 * d:3 * d]
    (cos_h, sin_h), (cos_i, sin_i) = tables
    hn = _normmod(x, norm_g, sc1, sh1)
    w_bf = w_in.astype(BF16)
    nq = N_HEADS * HEAD_DIM
    nkv = N_KV_GROUPS * HEAD_DIM
    nqi = IDX_HEADS * IDX_DIM
    tn = 512
    q = _proj_rope(hn, w_bf, 0, nq, cos_h, sin_h, HEAD_DIM // 2, BF16)
    kv = _proj_kv(hn, w_bf, nq // nkv, nq // nkv + 1, cos_h, sin_h)
    qi = _proj_rope(hn, w_bf, (nq + 2 * nkv) // tn, nqi, cos_i, sin_i, IDX_DIM // 2, BF16)
    tail0 = nq + 2 * nkv + nqi
    w_tail = jnp.pad(w_bf[:, tail0:], ((0, 0), (0, LANES - (IDX_DIM + IDX_HEADS))))
    lane = jnp.arange(LANES)
    wi_scale = (IDX_HEADS ** -0.5) * (IDX_DIM ** -0.5)
    cos_t = jnp.where(lane[None, :] < IDX_DIM, cos_i, wi_scale)
    sin_t = jnp.where(lane[None, :] < IDX_DIM, sin_i, 0.0)
    tail = _proj_rope(hn, w_tail, 0, LANES, cos_t, sin_t, IDX_DIM // 2, F32, tn=LANES)
    ki = tail[:, :IDX_DIM].astype(BF16)
    wit = tail[:, IDX_DIM:IDX_DIM + IDX_HEADS].T
    n_keep = min(TOPK_KEYS, t // 4)
    idx_t, nv = _indexer(qi.T, ki, wit, n_keep)
    att = _gattn(nv.reshape(t), idx_t.T, q.reshape(t * N_HEADS, HEAD_DIM), kv.reshape(t * KV_ROWS, LANES), n_keep)
    att = att.reshape(t, nq)
    return _mm_res(att, w_out.astype(BF16), jnp.zeros((d,), F32), x, g1)


def _conv_layer(x, mod, norm_g, w_pw1, b_pw1, w_dw, b_dw, ln_g, ln_b, w_pw2, b_pw2):
    t, d = x.shape
    sh1, sc1, g1 = mod[0:d], mod[d:2 * d], mod[2 * d:3 * d]
    hn = _normmod(x, norm_g, sc1, sh1)
    u = _glu(hn, w_pw1.astype(BF16), b_pw1)
    v = _dwconv_ln(u, w_dw, b_dw, ln_g, ln_b)
    return _mm_res(v, w_pw2.astype(BF16), b_pw2, x, g1)


def _moe_layer(x, mod, norm_g, w_rg, b_rg, w_re, b_re, w_gate, w_up, w_down, final_g, final_norm):
    t, d = x.shape
    sh2, sc2, g2 = mod[3 * d:4 * d], mod[4 * d:5 * d], mod[5 * d:6 * d]
    h, aux, counts = _route(x, norm_g, sc2, sh2, w_rg, b_rg, w_re, b_re)
    bucket = aux[:, AUX_BUCKET].astype(I32)
    rank = aux[:, AUX_RANK].astype(I32)
    sizes = counts[0, :N_BUCKETS].astype(I32)
    padded = (sizes + MOE_TM - 1) // MOE_TM * MOE_TM
    ends = jnp.cumsum(padded)
    dest = (ends - padded)[bucket] + rank
    n_blocks = t // MOE_TM + N_BUCKETS
    blk_start = jnp.arange(n_blocks, dtype=I32) * MOE_TM
    blk_bucket = jnp.minimum(jnp.searchsorted(ends, blk_start, side='right'), N_BUCKETS - 1).astype(I32)
    live = (blk_start < ends[-1]).astype(I32)
    group0 = (blk_bucket // N_PAIRS) * EXPERTS_PER_GROUP
    e_lo = group0 + jnp.asarray(PAIR_LO, I32)[blk_bucket % N_PAIRS]
    e_hi = group0 + jnp.asarray(PAIR_HI, I32)[blk_bucket % N_PAIRS]
    xs, ws = _dispatch(dest, h, aux, n_blocks * MOE_TM)
    ys = _ffn(e_lo, e_hi, live, xs, ws, w_gate.astype(BF16), w_up.astype(BF16), w_down.astype(BF16))
    return _combine(dest, x, g2, final_g, ys, final_norm)


def kernel(x, c, ada_w, ada_b, norm1_g, norm2_g, attn_w_in, attn_w_out, conv_w_pw1, conv_b_pw1, conv_w_dw, conv_b_dw, conv_ln_g, conv_ln_b, conv_w_pw2, conv_b_pw2, router_w_group, router_b_group, router_w_expert, router_b_expert, exp_w_gate, exp_w_up, exp_w_down, final_g):
    bsz, t, d = x.shape
    assert bsz == 1 and d == N_HEADS * HEAD_DIM
    depth = ada_w.shape[0]
    mods = _ada_mod(c, ada_w, ada_b)
    tables = _rope_tables(t)
    xf = x.reshape(t, d)
    for i in range(depth):
        j = i // 2
        if i % 2 == 0:
            xf = _attention_layer(xf, mods[i], norm1_g[i], attn_w_in[j], attn_w_out[j], tables)
        else:
            xf = _conv_layer(xf, mods[i], norm1_g[i], conv_w_pw1[j], conv_b_pw1[j], conv_w_dw[j], conv_b_dw[j],
                             conv_ln_g[j], conv_ln_b[j], conv_w_pw2[j], conv_b_pw2[j])
        xf = _moe_layer(xf, mods[i], norm2_g[i], router_w_group[i], router_b_group[i], router_w_expert[i],
                        router_b_expert[i], exp_w_gate[i], exp_w_up[i], exp_w_down[i], final_g, i == depth - 1)
    return xf.reshape(bsz, t, d)
```

```python
import functools

import jax
import jax.numpy as jnp
from jax import lax
from jax.experimental import pallas as pl
from jax.experimental.pallas import tpu as pltpu

EPS = 1e-6
CHUNK = 64
N_HEADS = 16
N_KV_GROUPS = 4
HEADS_PER_GROUP = N_HEADS // N_KV_GROUPS
HEAD_DIM = 128
IDX_HEADS = 16
IDX_DIM = 64
TOPK_KEYS = 256
ROPE_THETA = 10000.0
NEG_INF = -1e30
CONV_WIDTH = 31
N_GROUPS = 4
EXPERTS_PER_GROUP = 4
N_EXPERTS = N_GROUPS * EXPERTS_PER_GROUP

LANES = 128
KV_ROWS = N_KV_GROUPS
INT_MIN = -(2 ** 31)
INVALID_SHIFT = 1 << 30
INVALID_PACKED = 1 << 14
VMEM_LIMIT = 56 * 1024 * 1024

F32 = jnp.float32
BF16 = jnp.bfloat16
I32 = jnp.int32
I16 = jnp.int16


def _cparams(sem):
    return pltpu.CompilerParams(dimension_semantics=sem, vmem_limit_bytes=VMEM_LIMIT)


def _ada_kernel(c_ref, w_ref, b_ref, o_ref):
    c = c_ref[...]
    ca = c * jax.nn.sigmoid(c)
    o_ref[0] = jnp.dot(ca, w_ref[0], preferred_element_type=F32, precision=lax.Precision.HIGHEST) + b_ref[0]


def _ada_mod(c, ada_w, ada_b):
    depth, d, n = ada_w.shape
    tn = 768
    c8 = jnp.zeros((8, d), F32).at[0].set(c[0])
    out = pl.pallas_call(
        _ada_kernel,
        grid=(depth, n // tn),
        in_specs=[pl.BlockSpec((8, d), lambda i, j: (0, 0)),
                  pl.BlockSpec((1, d, tn), lambda i, j: (i, 0, j)),
                  pl.BlockSpec((1, 1, tn), lambda i, j: (i, 0, j))],
        out_specs=pl.BlockSpec((1, 8, tn), lambda i, j: (i, 0, j)),
        out_shape=jax.ShapeDtypeStruct((depth, 8, n), F32),
        compiler_params=_cparams(("arbitrary", "arbitrary")),
        name="ada_mod",
    )(c8, ada_w, ada_b.reshape(depth, 1, n))
    return out[:, 0, :]


def _normmod_kernel(x_ref, g_ref, sc_ref, sh_ref, o_ref):
    x = x_ref[...]
    xn = x * lax.rsqrt(jnp.mean(x * x, axis=-1, keepdims=True) + EPS)
    o_ref[...] = ((xn * g_ref[...]) * (1.0 + sc_ref[...]) + sh_ref[...]).astype(o_ref.dtype)


def _normmod(x, g, sc, sh, tm=512):
    t, d = x.shape
    vec = pl.BlockSpec((1, d), lambda i: (0, 0))
    return pl.pallas_call(
        _normmod_kernel,
        grid=(t // tm,),
        in_specs=[pl.BlockSpec((tm, d), lambda i: (i, 0)), vec, vec, vec],
        out_specs=pl.BlockSpec((tm, d), lambda i: (i, 0)),
        out_shape=jax.ShapeDtypeStruct((t, d), BF16),
        compiler_params=_cparams(("arbitrary",)),
        name="normmod",
    )(x, g.reshape(1, d), sc.reshape(1, d), sh.reshape(1, d))


N_PAIRS = EXPERTS_PER_GROUP * (EXPERTS_PER_GROUP - 1) // 2
N_BUCKETS = N_GROUPS * N_PAIRS
PAIR_LO = (0, 0, 0, 1, 1, 2)
PAIR_HI = (1, 2, 3, 2, 3, 3)
MOE_TM = 256
AUX_W_LO, AUX_W_HI, AUX_BUCKET, AUX_RANK = 0, 1, 2, 3


def _route_kernel(x_ref, g_ref, sc_ref, sh_ref, wr_ref, br_ref, tri_ref, h_ref, aux_ref, cnt_ref, run_ref):
    @pl.when(pl.program_id(0) == 0)
    def _():
        run_ref[...] = jnp.zeros_like(run_ref)

    x = x_ref[...]
    xn = x * lax.rsqrt(jnp.mean(x * x, axis=-1, keepdims=True) + EPS)
    h = (xn * g_ref[...]) * (1.0 + sc_ref[...]) + sh_ref[...]
    h_ref[...] = h.astype(h_ref.dtype)
    h_hi = h.astype(BF16)
    h_lo = (h - h_hi.astype(F32)).astype(BF16)
    w = wr_ref[...]
    w_hi = w.astype(BF16)
    w_lo = (w - w_hi.astype(F32)).astype(BF16)
    logits = (jnp.dot(h_hi, w_hi, preferred_element_type=F32) + jnp.dot(h_lo, w_hi, preferred_element_type=F32)
              + jnp.dot(h_hi, w_lo, preferred_element_type=F32)) + br_ref[...]
    tm = x.shape[0]
    lane4 = lax.broadcasted_iota(I32, (tm, N_GROUPS), 1)

    def first_argmax(v):
        m = jnp.max(v, axis=-1, keepdims=True)
        return m, jnp.min(jnp.where(v == m, lane4, N_GROUPS), axis=-1, keepdims=True)

    gl = logits[:, 0:N_GROUPS]
    gmax, gidx = first_argmax(gl)
    g_w = 1.0 / jnp.sum(jnp.exp(gl - gmax), axis=-1, keepdims=True)
    el = jnp.zeros((tm, EXPERTS_PER_GROUP), F32)
    for g in range(N_GROUPS):
        lo = N_GROUPS + g * EXPERTS_PER_GROUP
        el = jnp.where(gidx == g, logits[:, lo:lo + EXPERTS_PER_GROUP], el)
    emax = jnp.max(el, axis=-1, keepdims=True)
    ee = jnp.exp(el - emax)
    ep = ee / jnp.sum(ee, axis=-1, keepdims=True)
    p1, i1 = first_argmax(ep)
    rest = jnp.where(lane4 == i1, -1.0, ep)
    p2, i2 = first_argmax(rest)
    den = p1 + p2
    w1 = g_w * p1 / den
    w2 = g_w * p2 / den
    first_lo = i1 < i2
    a = jnp.minimum(i1, i2)
    b = jnp.maximum(i1, i2)
    w_lo = jnp.where(first_lo, w1, w2)
    w_hi = jnp.where(first_lo, w2, w1)
    pair = jnp.where(a == 0, b - 1, jnp.where(a == 1, b + 1, N_PAIRS - 1))
    bucket = gidx * N_PAIRS + pair
    lane = lax.broadcasted_iota(I32, (tm, LANES), 1)
    onehot = lane == bucket
    prefix = jnp.dot(tri_ref[...], onehot.astype(F32).astype(BF16), preferred_element_type=F32)
    run = run_ref[...]
    rank = jnp.sum(jnp.where(onehot, prefix + run, 0.0), axis=1, keepdims=True) - 1.0
    run = run + prefix[tm - 1:tm, :]
    run_ref[...] = run
    cnt_ref[...] = run
    aux_ref[...] = (jnp.where(lane == AUX_W_LO, w_lo, 0.0) + jnp.where(lane == AUX_W_HI, w_hi, 0.0)
                    + jnp.where(lane == AUX_BUCKET, bucket.astype(F32), 0.0) + jnp.where(lane == AUX_RANK, rank, 0.0))


def _route(x, g, sc, sh, w_rg, b_rg, w_re, b_re, tm=512):
    t, d = x.shape
    w_all = jnp.concatenate([w_rg, jnp.moveaxis(w_re, 0, 1).reshape(d, N_EXPERTS)], axis=1)
    b_all = jnp.concatenate([b_rg, b_re.reshape(N_EXPERTS)])
    npad = LANES - w_all.shape[1]
    w_all = jnp.pad(w_all, ((0, 0), (0, npad)))
    b_all = jnp.pad(b_all, (0, npad)).reshape(1, LANES)
    vec = pl.BlockSpec((1, d), lambda i: (0, 0))
    r = lax.broadcasted_iota(I32, (tm, tm), 0)
    c = lax.broadcasted_iota(I32, (tm, tm), 1)
    tri = (c <= r).astype(BF16)
    return pl.pallas_call(
        _route_kernel,
        grid=(t // tm,),
        in_specs=[pl.BlockSpec((tm, d), lambda i: (i, 0)), vec, vec, vec,
                  pl.BlockSpec((d, LANES), lambda i: (0, 0)), pl.BlockSpec((1, LANES), lambda i: (0, 0)),
                  pl.BlockSpec((tm, tm), lambda i: (0, 0))],
        out_specs=[pl.BlockSpec((tm, d), lambda i: (i, 0)), pl.BlockSpec((tm, LANES), lambda i: (i, 0)),
                   pl.BlockSpec((1, LANES), lambda i: (0, 0))],
        out_shape=[jax.ShapeDtypeStruct((t, d), F32), jax.ShapeDtypeStruct((t, LANES), F32),
                   jax.ShapeDtypeStruct((1, LANES), F32)],
        scratch_shapes=[pltpu.VMEM((1, LANES), F32)],
        compiler_params=_cparams(("arbitrary",)),
        name="route",
    )(x, g.reshape(1, d), sc.reshape(1, d), sh.reshape(1, d), w_all, b_all, tri)


def _rope_block(blk, cos, sin, half):
    if half == HEAD_DIM // 2:
        partner = pltpu.roll(blk, HEAD_DIM // 2, 1)
    else:
        lane = lax.broadcasted_iota(I32, blk.shape, 1)
        first = (lane % IDX_DIM) < (IDX_DIM // 2)
        partner = jnp.where(first, pltpu.roll(blk, LANES - IDX_DIM // 2, 1), pltpu.roll(blk, IDX_DIM // 2, 1))
    return blk * cos + partner * sin


def _proj_rope_kernel(a_ref, w_ref, cos_ref, sin_ref, o_ref, *, half):
    acc = jnp.dot(a_ref[...], w_ref[...], preferred_element_type=F32)
    cos = cos_ref[...]
    sin = sin_ref[...]
    for h in range(acc.shape[1] // LANES):
        sl = slice(h * LANES, (h + 1) * LANES)
        o_ref[:, sl] = _rope_block(acc[:, sl], cos, sin, half).astype(o_ref.dtype)


def _proj_rope(a, w, col_block0, n_out, cos, sin, half, out_dtype, tm=512, tn=512):
    t, d = a.shape
    return pl.pallas_call(
        functools.partial(_proj_rope_kernel, half=half),
        grid=(t // tm, n_out // tn),
        in_specs=[pl.BlockSpec((tm, d), lambda i, j: (i, 0)),
                  pl.BlockSpec((d, tn), lambda i, j: (0, col_block0 + j)),
                  pl.BlockSpec((tm, LANES), lambda i, j: (i, 0)),
                  pl.BlockSpec((tm, LANES), lambda i, j: (i, 0))],
        out_specs=pl.BlockSpec((tm, tn), lambda i, j: (i, j)),
        out_shape=jax.ShapeDtypeStruct((t, n_out), out_dtype),
        compiler_params=_cparams(("arbitrary", "arbitrary")),
        name="proj_rope",
    )(a, w, cos, sin)


def _proj_kv_kernel(a_ref, wk_ref, wv_ref, cos_ref, sin_ref, o_ref):
    a = a_ref[...]
    k = jnp.dot(a, wk_ref[...], preferred_element_type=F32)
    v = jnp.dot(a, wv_ref[...], preferred_element_type=F32)
    cos = cos_ref[...]
    sin = sin_ref[...]
    for h in range(k.shape[1] // LANES):
        sl = slice(h * LANES, (h + 1) * LANES)
        kr = _rope_block(k[:, sl], cos, sin, HEAD_DIM // 2)
        kb = lax.bitcast_convert_type(kr.astype(BF16).astype(F32), I32)
        vb = lax.bitcast_convert_type(v[:, sl].astype(BF16).astype(F32), I32)
        o_ref[:, sl] = kb | lax.shift_right_logical(vb, 16)


def _proj_kv(a, w, kblock, vblock, cos, sin, tm=512):
    t, d = a.shape
    n = N_KV_GROUPS * HEAD_DIM
    return pl.pallas_call(
        _proj_kv_kernel,
        grid=(t // tm,),
        in_specs=[pl.BlockSpec((tm, d), lambda i: (i, 0)),
                  pl.BlockSpec((d, n), lambda i: (0, kblock)),
                  pl.BlockSpec((d, n), lambda i: (0, vblock)),
                  pl.BlockSpec((tm, LANES), lambda i: (i, 0)),
                  pl.BlockSpec((tm, LANES), lambda i: (i, 0))],
        out_specs=pl.BlockSpec((tm, n), lambda i: (i, 0)),
        out_shape=jax.ShapeDtypeStruct((t, n), I32),
        compiler_params=_cparams(("arbitrary",)),
        name="proj_kv",
    )(a, w, w, cos, sin)


QB = 256
TK = 512
RC = 64


def _indexer_kernel(qit_ref, ki_ref, wit_ref, tri_ref, idx_ref, nv_ref, d_ref, hi_ref, *, n_keep):
    b = pl.program_id(0)
    q0 = b * QB
    n_t = (q0 + QB + TK - 1) // TK
    n_rows = n_t * TK
    rowi = lax.broadcasted_iota(I32, (TK, QB), 0)
    qchunk = (q0 + lax.broadcasted_iota(I32, (TK, QB), 1)) // CHUNK

    def admissible(k0):
        return ((k0 + rowi) // CHUNK) <= qchunk

    def score_tile(kt, carry):
        k0 = pl.multiple_of(kt * TK, TK)
        ki_t = ki_ref[pl.ds(k0, TK), :]
        acc = jnp.zeros((TK, QB), F32)
        for h in range(IDX_HEADS):
            a = jnp.dot(ki_t, qit_ref[h * IDX_DIM:(h + 1) * IDX_DIM, :], preferred_element_type=F32)
            acc = acc + jnp.maximum(a, 0.0) * wit_ref[h:h + 1, :]
        s = jnp.where(admissible(k0), acc + 0.0, NEG_INF)
        u = lax.bitcast_convert_type(s, I32)
        key = jnp.where(u < 0, u ^ 0x7FFFFFFF, u)
        d_ref[pl.ds(k0, TK), :] = key
        hi_ref[pl.ds(k0, TK), :] = lax.shift_right_arithmetic(key, 16).astype(I16)
        return carry

    lax.fori_loop(0, n_t, score_tile, 0)

    def count_where(pred):
        def body(kt, acc):
            k0 = pl.multiple_of(kt * TK, TK)
            hit = pred(d_ref[pl.ds(k0, TK), :]).astype(I32)
            return acc + jnp.sum(hit.reshape(TK // 8, 8, QB), axis=0)
        acc = lax.fori_loop(0, n_t, body, jnp.zeros((8, QB), I32))
        return jnp.sum(acc, axis=0, keepdims=True)

    def signed16(v_u):
        return lax.shift_right_arithmetic(jnp.left_shift(v_u ^ 0x8000, 16), 16).astype(I16)

    def add_packed_rows(acc, hit):
        for c in range(TK // 16):
            acc = acc + hit[c * 16:(c + 1) * 16, :]
        return acc

    def count_half_where(pred):
        def body(kt, acc):
            k0 = pl.multiple_of(kt * TK, TK)
            return add_packed_rows(acc, pred(hi_ref[pl.ds(k0, TK), :]).astype(I16))
        acc = lax.fori_loop(0, n_t, body, jnp.zeros((16, QB), I16))
        return jnp.sum(acc.astype(I32), axis=0, keepdims=True)

    def select_half(base):
        def select_bit(it, carry):
            v_u, cnt_v = carry
            cand_u = v_u | jnp.left_shift(jnp.int32(1), 15 - it)
            cand16 = signed16(cand_u)
            cnt = base + count_half_where(lambda half: half >= cand16)
            keep = cnt >= n_keep
            return jnp.where(keep, cand_u, v_u), jnp.where(keep, cnt, cnt_v)
        return select_bit

    zero_i = jnp.zeros((1, QB), I32)
    high_u, cnt_thr = lax.fori_loop(0, 16, select_half(zero_i), (zero_i, jnp.full((1, QB), n_keep + 1, I32)))
    high16 = signed16(high_u)

    def to_low_halves(kt, acc):
        k0 = pl.multiple_of(kt * TK, TK)
        high = hi_ref[pl.ds(k0, TK), :]
        low = signed16(d_ref[pl.ds(k0, TK), :] & 0xFFFF)
        hi_ref[pl.ds(k0, TK), :] = jnp.where(high == high16, low, jnp.int16(-(2 ** 15)))
        return add_packed_rows(acc, (high > high16).astype(I16))

    above = lax.fori_loop(0, n_t, to_low_halves, jnp.zeros((16, QB), I16))
    above = jnp.sum(above.astype(I32), axis=0, keepdims=True)
    low_u, cnt_thr = lax.fori_loop(0, 16, select_half(above), (zero_i, cnt_thr))
    thr = (jnp.left_shift(high_u, 16) | low_u) ^ INT_MIN

    tri = tri_ref[...]
    zero = jnp.zeros((1, QB), F32)

    def place(k0, sel, run_sel):
        pre_sel = jnp.dot(tri, sel.astype(F32).astype(BF16), preferred_element_type=F32)
        rank = (run_sel + pre_sel).astype(I32) - 1
        d_ref[pl.ds(k0, TK), :] = jnp.where(sel, (k0 + rowi) - rank, INVALID_SHIFT)
        return run_sel + pre_sel[TK - 1:TK, :]

    def rank_exact_count():
        def tile(kt, run_sel):
            k0 = pl.multiple_of(kt * TK, TK)
            return place(k0, admissible(k0) & (d_ref[pl.ds(k0, TK), :] >= thr), run_sel)
        return lax.fori_loop(0, n_t, tile, zero)

    def rank_with_ties():
        need = (n_keep - count_where(lambda key: key > thr)).astype(F32)

        def tile(kt, carry):
            run_eq, run_sel = carry
            k0 = pl.multiple_of(kt * TK, TK)
            key = d_ref[pl.ds(k0, TK), :]
            eq = key == thr
            eq_f = eq.astype(F32)
            pre_eq = jnp.dot(tri, eq_f.astype(BF16), preferred_element_type=F32)
            sel = admissible(k0) & ((key > thr) | (eq & ((run_eq + pre_eq - eq_f) < need)))
            return run_eq + pre_eq[TK - 1:TK, :], place(k0, sel, run_sel)
        return lax.fori_loop(0, n_t, tile, (zero, zero))[1]

    n_sel = lax.cond(jnp.max(jnp.abs(cnt_thr - n_keep)) == 0, rank_exact_count, rank_with_ties)

    p_ref = hi_ref
    packed_rows = 16
    d_ref[pl.ds(pl.multiple_of(n_rows, RC), RC), :] = jnp.full((RC, QB), INVALID_SHIFT, I32)
    p_ref[pl.ds(pl.multiple_of(n_rows, RC), RC), :] = jnp.full((RC, QB), INVALID_PACKED, I16)
    n_chunks = n_rows // RC
    win_chunks = -(-n_keep // RC)
    window = win_chunks * RC

    def first_stage(c, carry):
        r0 = pl.multiple_of(c * RC, RC)
        x = d_ref[pl.ds(r0, RC + 8), :]
        own, inc = x[0:RC], x[1:1 + RC]
        take = (inc & (1 | INVALID_SHIFT)) == 1
        stay = (own & (1 | INVALID_SHIFT)) == 0
        moved = jnp.where(take, inc, jnp.where(stay, own, INVALID_SHIFT))
        p_ref[pl.ds(r0, RC), :] = jnp.minimum(moved, INVALID_PACKED).astype(I16)
        return carry

    lax.fori_loop(0, n_chunks, first_stage, 0)
    n_stages = (d_ref.shape[0] - RC - 1).bit_length()
    for s in range(1, n_stages):
        sh = 1 << s
        test = sh | INVALID_PACKED

        def merge(own, inc, sh=sh, test=test):
            take = (inc & test) == sh
            stay = (own & test) == 0
            return jnp.where(take, inc, jnp.where(stay, own, INVALID_PACKED))

        def chunk_with_source(c, carry, sh=sh, merge=merge):
            r0 = pl.multiple_of(c * RC, RC)
            if sh < packed_rows:
                x = p_ref[pl.ds(r0, RC + packed_rows), :]
                own, inc = x[0:RC], x[sh:sh + RC]
            else:
                own = p_ref[pl.ds(r0, RC), :]
                inc = p_ref[pl.ds(pl.multiple_of(r0 + sh, packed_rows), RC), :]
            p_ref[pl.ds(r0, RC), :] = merge(own, inc)
            return carry

        def chunk_without_source(c, carry, merge=merge):
            r0 = pl.multiple_of(c * RC, RC)
            own = p_ref[pl.ds(r0, RC), :]
            p_ref[pl.ds(r0, RC), :] = merge(own, jnp.full_like(own, INVALID_PACKED))
            return carry

        def window_chunk(c, carry, sh=sh, merge=merge):
            r0 = pl.multiple_of((c // win_chunks) * sh + (c % win_chunks) * RC, RC)
            src = r0 + sh
            own = p_ref[pl.ds(r0, RC), :]
            inc = p_ref[pl.ds(pl.multiple_of(jnp.minimum(src, n_rows), RC), RC), :]
            p_ref[pl.ds(r0, RC), :] = merge(own, inc)
            return carry

        if sh < RC:
            lax.fori_loop(0, n_chunks, chunk_with_source, 0)
        elif sh < 2 * window:
            n_src = jnp.maximum(n_rows - sh, 0) // RC
            lax.fori_loop(0, n_src, chunk_with_source, 0)
            lax.fori_loop(n_src, n_chunks, chunk_without_source, 0)
        else:
            lax.fori_loop(0, ((n_rows + sh - 1) // sh) * win_chunks, window_chunk, 0)

    slot = lax.broadcasted_iota(I32, (n_keep, QB), 0)
    dd = p_ref[0:n_keep, :].astype(I32)
    idx_ref[...] = jnp.where(dd < INVALID_PACKED, (slot + dd) * KV_ROWS, 0)
    nv_ref[...] = n_sel.astype(I32)


def _indexer(qit, ki, wit, n_keep):
    t = ki.shape[0]
    assert t <= INVALID_PACKED and t % TK == 0
    r = lax.broadcasted_iota(I32, (TK, TK), 0)
    c = lax.broadcasted_iota(I32, (TK, TK), 1)
    tri = (c <= r).astype(BF16)
    return pl.pallas_call(
        functools.partial(_indexer_kernel, n_keep=n_keep),
        grid=(t // QB,),
        in_specs=[pl.BlockSpec((IDX_HEADS * IDX_DIM, QB), lambda b: (0, b)),
                  pl.BlockSpec((t, IDX_DIM), lambda b: (0, 0)),
                  pl.BlockSpec((IDX_HEADS, QB), lambda b: (0, b)),
                  pl.BlockSpec((TK, TK), lambda b: (0, 0))],
        out_specs=[pl.BlockSpec((n_keep, QB), lambda b: (0, b)), pl.BlockSpec((1, QB), lambda b: (0, b))],
        out_shape=[jax.ShapeDtypeStruct((n_keep, t), I32), jax.ShapeDtypeStruct((1, t), I32)],
        scratch_shapes=[pltpu.VMEM((t + RC, QB), I32), pltpu.VMEM((t + RC, QB), I16)],
        compiler_params=_cparams(("arbitrary",)),
        name="indexer",
    )(qit, ki, wit, tri)


GQ = 64
GATTN_UNROLL = 8


def _gattn_kernel(nv_ref, idx_ref, q_ref, kv_hbm, o_ref, kv_vmem, stage_a, stage_b, stage_c, stage_d,
                  s_a, s_b, row_lists, sem_kv, sem_rows, *, n_keep):
    i = pl.program_id(0)

    @pl.when(i == 0)
    def _():
        cp = pltpu.make_async_copy(kv_hbm, kv_vmem, sem_kv)
        cp.start()
        cp.wait()

    cp = pltpu.make_async_copy(idx_ref, row_lists, sem_rows)
    cp.start()
    cp.wait()

    n_col = n_keep * KV_ROWS
    colg = lax.broadcasted_iota(I32, (N_HEADS, n_col), 1)
    rowh = lax.broadcasted_iota(I32, (N_HEADS, n_col), 0)
    own_group = (colg % KV_ROWS) == (rowh // HEADS_PER_GROUP)
    col_slot = colg // KV_ROWS
    scale = HEAD_DIM ** -0.5

    last = GQ - 1
    stages = (stage_a, stage_b, stage_c, stage_d)
    s_bufs = (s_a, s_b)

    def gather(r, stage):
        rows = row_lists.at[pl.ds(jnp.minimum(r, last), 1)]
        for j in range(n_keep):
            row = pl.multiple_of(rows[0, j], KV_ROWS)
            stage[j * KV_ROWS:(j + 1) * KV_ROWS, :] = kv_vmem[pl.ds(row, KV_ROWS), :]

    def scores(r, stage, s_buf):
        k_all = lax.bitcast_convert_type(stage[...] & jnp.int32(-65536), F32).astype(BF16)
        qh = q_ref[pl.ds(pl.multiple_of(r * N_HEADS, N_HEADS), N_HEADS), :]
        s = lax.dot_general(qh, k_all, (((1,), (1,)), ((), ())), preferred_element_type=F32) * scale
        mask = own_group & (col_slot < nv_ref[i * GQ + r])
        s_buf[...] = jnp.where(mask, s, NEG_INF)

    def combine(r, stage, s_buf):
        s = s_buf[...]
        p = jnp.exp(s - jnp.max(s, axis=1, keepdims=True))
        l = jnp.sum(p, axis=1, keepdims=True)
        v_all = lax.bitcast_convert_type(jnp.left_shift(stage[...], 16), F32).astype(BF16)
        o = jnp.dot(p.astype(BF16), v_all, preferred_element_type=F32) / l
        o_ref[pl.ds(pl.multiple_of(r * N_HEADS, N_HEADS), N_HEADS), :] = o.astype(o_ref.dtype)

    gather(0, stages[0])
    gather(1, stages[1])
    scores(0, stages[0], s_bufs[0])

    def query_group(it, carry):
        r = GATTN_UNROLL * it
        for u in range(GATTN_UNROLL):
            gather(r + u + 2, stages[(u + 2) % 4])
            scores(jnp.minimum(r + u + 1, last), stages[(u + 1) % 4], s_bufs[(u + 1) % 2])
            combine(r + u, stages[u % 4], s_bufs[u % 2])
        return carry

    lax.fori_loop(0, GQ // GATTN_UNROLL, query_group, 0)


def _gattn(nv, idx, q2, kv2, n_keep):
    t = nv.shape[0]
    grid_spec = pltpu.PrefetchScalarGridSpec(
        num_scalar_prefetch=1,
        grid=(t // GQ,),
        in_specs=[pl.BlockSpec((GQ, n_keep), lambda i, nv: (i, 0)),
                  pl.BlockSpec((GQ * N_HEADS, HEAD_DIM), lambda i, nv: (i, 0)),
                  pl.BlockSpec(memory_space=pl.ANY)],
        out_specs=pl.BlockSpec((GQ * N_HEADS, HEAD_DIM), lambda i, nv: (i, 0)),
        scratch_shapes=[pltpu.VMEM(kv2.shape, I32),
                        pltpu.VMEM((n_keep * KV_ROWS, LANES), I32),
                        pltpu.VMEM((n_keep * KV_ROWS, LANES), I32),
                        pltpu.VMEM((n_keep * KV_ROWS, LANES), I32),
                        pltpu.VMEM((n_keep * KV_ROWS, LANES), I32),
                        pltpu.VMEM((N_HEADS, n_keep * KV_ROWS), F32),
                        pltpu.VMEM((N_HEADS, n_keep * KV_ROWS), F32),
                        pltpu.SMEM((GQ, n_keep), I32),
                        pltpu.SemaphoreType.DMA,
                        pltpu.SemaphoreType.DMA],
    )
    return pl.pallas_call(
        functools.partial(_gattn_kernel, n_keep=n_keep),
        grid_spec=grid_spec,
        out_shape=jax.ShapeDtypeStruct((t * N_HEADS, HEAD_DIM), BF16),
        compiler_params=_cparams(("arbitrary",)),
        name="gattn",
    )(nv, idx, q2, kv2)


def _mm_res_kernel(a_ref, w_ref, b_ref, x_ref, g_ref, o_ref):
    y = jnp.dot(a_ref[...], w_ref[...], preferred_element_type=F32) + b_ref[...]
    o_ref[...] = x_ref[...] + g_ref[...] * y


def _mm_res(a, w, bias, x, gate, tm=512, tn=512):
    t, k = a.shape
    n = w.shape[1]
    return pl.pallas_call(
        _mm_res_kernel,
        grid=(t // tm, n // tn),
        in_specs=[pl.BlockSpec((tm, k), lambda i, j: (i, 0)),
                  pl.BlockSpec((k, tn), lambda i, j: (0, j)),
                  pl.BlockSpec((1, tn), lambda i, j: (0, j)),
                  pl.BlockSpec((tm, tn), lambda i, j: (i, j)),
                  pl.BlockSpec((1, tn), lambda i, j: (0, j))],
        out_specs=pl.BlockSpec((tm, tn), lambda i, j: (i, j)),
        out_shape=jax.ShapeDtypeStruct((t, n), F32),
        compiler_params=_cparams(("arbitrary", "arbitrary")),
        name="mm_res",
    )(a, w, bias.reshape(1, n), x, gate.reshape(1, n))


def _glu_kernel(a_ref, wa_ref, wg_ref, ba_ref, bg_ref, o_ref):
    a = a_ref[...]
    ya = jnp.dot(a, wa_ref[...], preferred_element_type=F32) + ba_ref[...]
    yg = jnp.dot(a, wg_ref[...], preferred_element_type=F32) + bg_ref[...]
    o_ref[...] = ya * jax.nn.sigmoid(yg)


def _glu(a, w, b, tm=512, tn=512):
    t, k = a.shape
    n = w.shape[1] // 2
    nb = n // tn
    b2 = b.reshape(1, 2 * n)
    return pl.pallas_call(
        _glu_kernel,
        grid=(t // tm, nb),
        in_specs=[pl.BlockSpec((tm, k), lambda i, j: (i, 0)),
                  pl.BlockSpec((k, tn), lambda i, j: (0, j)),
                  pl.BlockSpec((k, tn), lambda i, j: (0, j + nb)),
                  pl.BlockSpec((1, tn), lambda i, j: (0, j)),
                  pl.BlockSpec((1, tn), lambda i, j: (0, j + nb))],
        out_specs=pl.BlockSpec((tm, tn), lambda i, j: (i, j)),
        out_shape=jax.ShapeDtypeStruct((t, n), F32),
        compiler_params=_cparams(("arbitrary", "arbitrary")),
        name="glu",
    )(a, w, w, b2, b2)


HALO = 32


SUBLANES = 8
CONV_ROWS = 32
CONV_COLS = 512


def _dwconv_ln_kernel(u_ref, prev_ref, w_ref, b_ref, g_ref, beta_ref, o_ref, ext_ref, sh_ref, conv_ref):
    i = pl.program_id(0)
    tm, d = u_ref.shape
    ext_ref[0:HALO, :] = jnp.where(i == 0, 0.0, prev_ref[...])
    ext_ref[HALO:HALO + tm, :] = u_ref[...]
    for phase in range(SUBLANES):
        sh_ref[phase, 0:tm + HALO - phase, :] = ext_ref[phase:tm + HALO, :]
    first = HALO - (CONV_WIDTH - 1)

    def conv_rows(c, carry):
        r0 = pl.multiple_of(c * CONV_ROWS, CONV_ROWS)
        for cb in range(d // CONV_COLS):
            cols = slice(cb * CONV_COLS, (cb + 1) * CONV_COLS)
            acc = jnp.zeros((CONV_ROWS, CONV_COLS), F32) + b_ref[:, cols]
            for k in range(CONV_WIDTH):
                phase = (first + k) % SUBLANES
                start = pl.multiple_of(r0 + (first + k - phase), SUBLANES)
                acc = acc + sh_ref[phase, pl.ds(start, CONV_ROWS), cols] * w_ref[k:k + 1, cols]
            conv_ref[pl.ds(r0, CONV_ROWS), cols] = acc
        return carry

    lax.fori_loop(0, tm // CONV_ROWS, conv_rows, 0)
    acc = conv_ref[...]
    mu = jnp.mean(acc, axis=-1, keepdims=True)
    cen = acc - mu
    var = jnp.mean(cen * cen, axis=-1, keepdims=True)
    y = (cen * lax.rsqrt(var + EPS)) * g_ref[...] + beta_ref[...]
    o_ref[...] = (y * jax.nn.sigmoid(y)).astype(o_ref.dtype)


def _dwconv_ln(u, w_dw, b_dw, ln_g, ln_b, tm=256):
    t, d = u.shape
    ratio = tm // HALO
    vec = pl.BlockSpec((1, d), lambda i: (0, 0))
    return pl.pallas_call(
        _dwconv_ln_kernel,
        grid=(t // tm,),
        in_specs=[pl.BlockSpec((tm, d), lambda i: (i, 0)),
                  pl.BlockSpec((HALO, d), lambda i: (jnp.maximum(i * ratio - 1, 0), 0)),
                  pl.BlockSpec((HALO, d), lambda i: (0, 0)),
                  vec, vec, vec],
        out_specs=pl.BlockSpec((tm, d), lambda i: (i, 0)),
        out_shape=jax.ShapeDtypeStruct((t, d), BF16),
        scratch_shapes=[pltpu.VMEM((HALO + tm, d), F32), pltpu.VMEM((SUBLANES, HALO + tm, d), F32),
                        pltpu.VMEM((tm, d), F32)],
        compiler_params=_cparams(("arbitrary",)),
        name="dwconv_ln",
    )(u, u, jnp.pad(w_dw, ((0, HALO - CONV_WIDTH), (0, 0))), b_dw.reshape(1, d), ln_g.reshape(1, d), ln_b.reshape(1, d))


def _row_copy(src, src_row, dst, dst_row, sem):
    return pltpu.make_async_copy(src.at[pl.ds(src_row, 1), :], dst.at[pl.ds(dst_row, 1), :], sem)


ROW_UNROLL = 8


def _for_rows(n, fn):
    def group(gi, carry):
        for u in range(ROW_UNROLL):
            fn(gi * ROW_UNROLL + u, carry)
        return carry

    lax.fori_loop(0, n // ROW_UNROLL, group, 0)


def _dispatch_kernel(dest_ref, h_ref, aux_ref, xs_in, ws_in, xs_hbm, ws_hbm, sem_x, sem_w):
    del xs_in, ws_in
    tm = h_ref.shape[0]
    t0 = pl.program_id(0) * tm

    def issue(r, carry):
        slot = dest_ref[t0 + r]
        _row_copy(h_ref, r, xs_hbm, slot, sem_x).start()
        _row_copy(aux_ref, r, ws_hbm, slot, sem_w).start()
        return carry

    def drain(r, carry):
        _row_copy(h_ref, r, xs_hbm, 0, sem_x).wait()
        _row_copy(aux_ref, r, ws_hbm, 0, sem_w).wait()
        return carry

    _for_rows(tm, issue)
    _for_rows(tm, drain)


def _dispatch(dest, h, aux, n_rows, tm=256):
    t, d = h.shape
    grid_spec = pltpu.PrefetchScalarGridSpec(
        num_scalar_prefetch=1,
        grid=(t // tm,),
        in_specs=[pl.BlockSpec((tm, d), lambda i, dest: (i, 0)),
                  pl.BlockSpec((tm, LANES), lambda i, dest: (i, 0)),
                  pl.BlockSpec(memory_space=pl.ANY),
                  pl.BlockSpec(memory_space=pl.ANY)],
        out_specs=[pl.BlockSpec(memory_space=pl.ANY), pl.BlockSpec(memory_space=pl.ANY)],
        scratch_shapes=[pltpu.SemaphoreType.DMA, pltpu.SemaphoreType.DMA],
    )
    return pl.pallas_call(
        _dispatch_kernel,
        grid_spec=grid_spec,
        out_shape=[jax.ShapeDtypeStruct((n_rows, d), F32), jax.ShapeDtypeStruct((n_rows, LANES), F32)],
        input_output_aliases={3: 0, 4: 1},
        compiler_params=_cparams(("arbitrary",)),
        name="dispatch",
    )(dest, h, aux, jnp.zeros((n_rows, d), F32), jnp.zeros((n_rows, LANES), F32))


def _ffn_kernel(elo_ref, ehi_ref, live_ref, xs_ref, ws_ref, wg_lo, wu_lo, wd_lo, wg_hi, wu_hi, wd_hi, ys_ref):
    b = pl.program_id(0)

    @pl.when(live_ref[b] == 1)
    def _():
        x = xs_ref[...].astype(BF16)
        ws = ws_ref[...]

        def expert(wg, wu, wd):
            gate = jnp.dot(x, wg[0], preferred_element_type=F32)
            up = jnp.dot(x, wu[0], preferred_element_type=F32)
            hid = (gate * jax.nn.sigmoid(gate)) * up
            return jnp.dot(hid.astype(BF16), wd[0], preferred_element_type=F32)

        ys_ref[...] = (expert(wg_lo, wu_lo, wd_lo) * ws[:, AUX_W_LO:AUX_W_LO + 1]
                       + expert(wg_hi, wu_hi, wd_hi) * ws[:, AUX_W_HI:AUX_W_HI + 1])

    @pl.when(live_ref[b] == 0)
    def _():
        ys_ref[...] = jnp.zeros_like(ys_ref)


def _ffn(e_lo, e_hi, live, xs, ws, w_gate, w_up, w_down):
    n_rows, d = xs.shape
    _, _, de = w_gate.shape
    lo_in = pl.BlockSpec((1, d, de), lambda b, lo, hi, live: (lo[b], 0, 0))
    hi_in = pl.BlockSpec((1, d, de), lambda b, lo, hi, live: (hi[b], 0, 0))
    lo_out = pl.BlockSpec((1, de, d), lambda b, lo, hi, live: (lo[b], 0, 0))
    hi_out = pl.BlockSpec((1, de, d), lambda b, lo, hi, live: (hi[b], 0, 0))
    grid_spec = pltpu.PrefetchScalarGridSpec(
        num_scalar_prefetch=3,
        grid=(n_rows // MOE_TM,),
        in_specs=[pl.BlockSpec((MOE_TM, d), lambda b, lo, hi, live: (b, 0)),
                  pl.BlockSpec((MOE_TM, LANES), lambda b, lo, hi, live: (b, 0)),
                  lo_in, lo_in, lo_out, hi_in, hi_in, hi_out],
        out_specs=pl.BlockSpec((MOE_TM, d), lambda b, lo, hi, live: (b, 0)),
    )
    return pl.pallas_call(
        _ffn_kernel,
        grid_spec=grid_spec,
        out_shape=jax.ShapeDtypeStruct((n_rows, d), F32),
        compiler_params=_cparams(("arbitrary",)),
        name="ffn",
    )(e_lo, e_hi, live, xs, ws, w_gate, w_up, w_down, w_gate, w_up, w_down)


def _combine_kernel(dest_ref, x_ref, g_ref, fg_ref, ys_hbm, o_ref, buf, sem, *, final_norm):
    tm = x_ref.shape[0]
    t0 = pl.program_id(0) * tm

    def issue(r, carry):
        _row_copy(ys_hbm, dest_ref[t0 + r], buf, r, sem).start()
        return carry

    def drain(r, carry):
        _row_copy(ys_hbm, 0, buf, r, sem).wait()
        return carry

    _for_rows(tm, issue)
    _for_rows(tm, drain)
    y = x_ref[...] + g_ref[...] * buf[...]
    if final_norm:
        y = (y * lax.rsqrt(jnp.mean(y * y, axis=-1, keepdims=True) + EPS)) * fg_ref[...]
    o_ref[...] = y


def _combine(dest, x, gate, final_g, ys, final_norm, tm=256):
    t, d = x.shape
    vec = pl.BlockSpec((1, d), lambda i, dest: (0, 0))
    grid_spec = pltpu.PrefetchScalarGridSpec(
        num_scalar_prefetch=1,
        grid=(t // tm,),
        in_specs=[pl.BlockSpec((tm, d), lambda i, dest: (i, 0)), vec, vec, pl.BlockSpec(memory_space=pl.ANY)],
        out_specs=pl.BlockSpec((tm, d), lambda i, dest: (i, 0)),
        scratch_shapes=[pltpu.VMEM((tm, d), F32), pltpu.SemaphoreType.DMA],
    )
    return pl.pallas_call(
        functools.partial(_combine_kernel, final_norm=final_norm),
        grid_spec=grid_spec,
        out_shape=jax.ShapeDtypeStruct((t, d), F32),
        compiler_params=_cparams(("arbitrary",)),
        name="combine",
    )(dest, x, gate.reshape(1, d), final_g.reshape(1, d), ys)


def _rope_tables(t):
    pos = jnp.arange(t, dtype=I32).astype(F32)[:, None]
    lane = jnp.arange(LANES)

    def table(dim):
        half = dim // 2
        inv = ROPE_THETA ** (-jnp.arange(half, dtype=F32) / half)
        ang = pos * inv[None, :]
        j = lane % dim
        cos = jnp.cos(ang)[:, j % half]
        sin = jnp.sin(ang)[:, j % half] * jnp.where(j < half, -1.0, 1.0)[None, :]
        return cos, sin

    return table(HEAD_DIM), table(IDX_DIM)


def _attention_layer(x, mod, norm_g, w_in, w_out, tables):
    t, d = x.shape
    sh1, sc1, g1 = mod[0:d], mod[d:2 * d], mod[2 * d:3 * d]
    (cos_h, sin_h), (cos_i, sin_i) = tables
    hn = _normmod(x, norm_g, sc1, sh1)
    w_bf = w_in.astype(BF16)
    nq = N_HEADS * HEAD_DIM
    nkv = N_KV_GROUPS * HEAD_DIM
    nqi = IDX_HEADS * IDX_DIM
    tn = 512
    q = _proj_rope(hn, w_bf, 0, nq, cos_h, sin_h, HEAD_DIM // 2, BF16)
    kv = _proj_kv(hn, w_bf, nq // nkv, nq // nkv + 1, cos_h, sin_h)
    qi = _proj_rope(hn, w_bf, (nq + 2 * nkv) // tn, nqi, cos_i, sin_i, IDX_DIM // 2, BF16)
    tail0 = nq + 2 * nkv + nqi
    w_tail = jnp.pad(w_bf[:, tail0:], ((0, 0), (0, LANES - (IDX_DIM + IDX_HEADS))))
    lane = jnp.arange(LANES)
    wi_scale = (IDX_HEADS ** -0.5) * (IDX_DIM ** -0.5)
    cos_t = jnp.where(lane[None, :] < IDX_DIM, cos_i, wi_scale)
    sin_t = jnp.where(lane[None, :] < IDX_DIM, sin_i, 0.0)
    tail = _proj_rope(hn, w_tail, 0, LANES, cos_t, sin_t, IDX_DIM // 2, F32, tn=LANES)
    ki = tail[:, :IDX_DIM].astype(BF16)
    wit = tail[:, IDX_DIM:IDX_DIM + IDX_HEADS].T
    n_keep = min(TOPK_KEYS, t // 4)
    idx_t, nv = _indexer(qi.T, ki, wit, n_keep)
    att = _gattn(nv.reshape(t), idx_t.T, q.reshape(t * N_HEADS, HEAD_DIM), kv.reshape(t * KV_ROWS, LANES), n_keep)
    att = att.reshape(t, nq)
    return _mm_res(att, w_out.astype(BF16), jnp.zeros((d,), F32), x, g1)


def _conv_layer(x, mod, norm_g, w_pw1, b_pw1, w_dw, b_dw, ln_g, ln_b, w_pw2, b_pw2):
    t, d = x.shape
    sh1, sc1, g1 = mod[0:d], mod[d:2 * d], mod[2 * d:3 * d]
    hn = _normmod(x, norm_g, sc1, sh1)
    u = _glu(hn, w_pw1.astype(BF16), b_pw1)
    v = _dwconv_ln(u, w_dw, b_dw, ln_g, ln_b)
    return _mm_res(v, w_pw2.astype(BF16), b_pw2, x, g1)


def _moe_layer(x, mod, norm_g, w_rg, b_rg, w_re, b_re, w_gate, w_up, w_down, final_g, final_norm):
    t, d = x.shape
    sh2, sc2, g2 = mod[3 * d:4 * d], mod[4 * d:5 * d], mod[5 * d:6 * d]
    h, aux, counts = _route(x, norm_g, sc2, sh2, w_rg, b_rg, w_re, b_re)
    bucket = aux[:, AUX_BUCKET].astype(I32)
    rank = aux[:, AUX_RANK].astype(I32)
    sizes = counts[0, :N_BUCKETS].astype(I32)
    padded = (sizes + MOE_TM - 1) // MOE_TM * MOE_TM
    ends = jnp.cumsum(padded)
    dest = (ends - padded)[bucket] + rank
    n_blocks = t // MOE_TM + N_BUCKETS
    blk_start = jnp.arange(n_blocks, dtype=I32) * MOE_TM
    blk_bucket = jnp.minimum(jnp.searchsorted(ends, blk_start, side='right'), N_BUCKETS - 1).astype(I32)
    live = (blk_start < ends[-1]).astype(I32)
    group0 = (blk_bucket // N_PAIRS) * EXPERTS_PER_GROUP
    e_lo = group0 + jnp.asarray(PAIR_LO, I32)[blk_bucket % N_PAIRS]
    e_hi = group0 + jnp.asarray(PAIR_HI, I32)[blk_bucket % N_PAIRS]
    xs, ws = _dispatch(dest, h, aux, n_blocks * MOE_TM)
    ys = _ffn(e_lo, e_hi, live, xs, ws, w_gate.astype(BF16), w_up.astype(BF16), w_down.astype(BF16))
    return _combine(dest, x, g2, final_g, ys, final_norm)


def kernel(x, c, ada_w, ada_b, norm1_g, norm2_g, attn_w_in, attn_w_out, conv_w_pw1, conv_b_pw1, conv_w_dw, conv_b_dw, conv_ln_g, conv_ln_b, conv_w_pw2, conv_b_pw2, router_w_group, router_b_group, router_w_expert, router_b_expert, exp_w_gate, exp_w_up, exp_w_down, final_g):
    bsz, t, d = x.shape
    assert bsz == 1 and d == N_HEADS * HEAD_DIM
    depth = ada_w.shape[0]
    mods = _ada_mod(c, ada_w, ada_b)
    tables = _rope_tables(t)
    xf = x.reshape(t, d)
    for i in range(depth):
        j = i // 2
        if i % 2 == 0:
            xf = _attention_layer(xf, mods[i], norm1_g[i], attn_w_in[j], attn_w_out[j], tables)
        else:
            xf = _conv_layer(xf, mods[i], norm1_g[i], conv_w_pw1[j], conv_b_pw1[j], conv_w_dw[j], conv_b_dw[j],
                             conv_ln_g[j], conv_ln_b[j], conv_w_pw2[j], conv_b_pw2[j])
        xf = _moe_layer(xf, mods[i], norm2_g[i], router_w_group[i], router_b_group[i], router_w_expert[i],
                        router_b_expert[i], exp_w_gate[i], exp_w_up[i], exp_w_down[i], final_g, i == depth - 1)
    return xf.reshape(bsz, t, d)
```

```python
import functools

import jax
import jax.numpy as jnp
from jax import lax
from jax.experimental import pallas as pl
from jax.experimental.pallas import tpu as pltpu

EPS = 1e-6
CHUNK = 64
N_HEADS = 16
N_KV_GROUPS = 4
HEADS_PER_GROUP = N_HEADS // N_KV_GROUPS
HEAD_DIM = 128
IDX_HEADS = 16
IDX_DIM = 64
TOPK_KEYS = 256
ROPE_THETA = 10000.0
NEG_INF = -1e30
CONV_WIDTH = 31
N_GROUPS = 4
EXPERTS_PER_GROUP = 4
N_EXPERTS = N_GROUPS * EXPERTS_PER_GROUP

LANES = 128
KV_ROWS = N_KV_GROUPS
INT_MIN = -(2 ** 31)
INVALID_SHIFT = 1 << 30
INVALID_PACKED = 1 << 14
VMEM_LIMIT = 56 * 1024 * 1024

F32 = jnp.float32
BF16 = jnp.bfloat16
I32 = jnp.int32
I16 = jnp.int16


def _cparams(sem):
    return pltpu.CompilerParams(dimension_semantics=sem, vmem_limit_bytes=VMEM_LIMIT)


def _ada_kernel(c_ref, w_ref, b_ref, o_ref):
    c = c_ref[...]
    ca = c * jax.nn.sigmoid(c)
    o_ref[0] = jnp.dot(ca, w_ref[0], preferred_element_type=F32, precision=lax.Precision.HIGHEST) + b_ref[0]


def _ada_mod(c, ada_w, ada_b):
    depth, d, n = ada_w.shape
    tn = 768
    c8 = jnp.zeros((8, d), F32).at[0].set(c[0])
    out = pl.pallas_call(
        _ada_kernel,
        grid=(depth, n // tn),
        in_specs=[pl.BlockSpec((8, d), lambda i, j: (0, 0)),
                  pl.BlockSpec((1, d, tn), lambda i, j: (i, 0, j)),
                  pl.BlockSpec((1, 1, tn), lambda i, j: (i, 0, j))],
        out_specs=pl.BlockSpec((1, 8, tn), lambda i, j: (i, 0, j)),
        out_shape=jax.ShapeDtypeStruct((depth, 8, n), F32),
        compiler_params=_cparams(("arbitrary", "arbitrary")),
        name="ada_mod",
    )(c8, ada_w, ada_b.reshape(depth, 1, n))
    return out[:, 0, :]


def _normmod_kernel(x_ref, g_ref, sc_ref, sh_ref, o_ref):
    x = x_ref[...]
    xn = x * lax.rsqrt(jnp.mean(x * x, axis=-1, keepdims=True) + EPS)
    o_ref[...] = ((xn * g_ref[...]) * (1.0 + sc_ref[...]) + sh_ref[...]).astype(o_ref.dtype)


def _normmod(x, g, sc, sh, tm=512):
    t, d = x.shape
    vec = pl.BlockSpec((1, d), lambda i: (0, 0))
    return pl.pallas_call(
        _normmod_kernel,
        grid=(t // tm,),
        in_specs=[pl.BlockSpec((tm, d), lambda i: (i, 0)), vec, vec, vec],
        out_specs=pl.BlockSpec((tm, d), lambda i: (i, 0)),
        out_shape=jax.ShapeDtypeStruct((t, d), BF16),
        compiler_params=_cparams(("arbitrary",)),
        name="normmod",
    )(x, g.reshape(1, d), sc.reshape(1, d), sh.reshape(1, d))


N_PAIRS = EXPERTS_PER_GROUP * (EXPERTS_PER_GROUP - 1) // 2
N_BUCKETS = N_GROUPS * N_PAIRS
PAIR_LO = (0, 0, 0, 1, 1, 2)
PAIR_HI = (1, 2, 3, 2, 3, 3)
MOE_TM = 256
AUX_W_LO, AUX_W_HI, AUX_BUCKET, AUX_RANK = 0, 1, 2, 3


def _route_kernel(x_ref, g_ref, sc_ref, sh_ref, wr_ref, br_ref, tri_ref, h_ref, aux_ref, cnt_ref, run_ref):
    @pl.when(pl.program_id(0) == 0)
    def _():
        run_ref[...] = jnp.zeros_like(run_ref)

    x = x_ref[...]
    xn = x * lax.rsqrt(jnp.mean(x * x, axis=-1, keepdims=True) + EPS)
    h = (xn * g_ref[...]) * (1.0 + sc_ref[...]) + sh_ref[...]
    h_ref[...] = h.astype(h_ref.dtype)
    h_hi = h.astype(BF16)
    h_lo = (h - h_hi.astype(F32)).astype(BF16)
    w = wr_ref[...]
    w_hi = w.astype(BF16)
    w_lo = (w - w_hi.astype(F32)).astype(BF16)
    logits = (jnp.dot(h_hi, w_hi, preferred_element_type=F32) + jnp.dot(h_lo, w_hi, preferred_element_type=F32)
              + jnp.dot(h_hi, w_lo, preferred_element_type=F32)) + br_ref[...]
    tm = x.shape[0]
    lane4 = lax.broadcasted_iota(I32, (tm, N_GROUPS), 1)

    def first_argmax(v):
        m = jnp.max(v, axis=-1, keepdims=True)
        return m, jnp.min(jnp.where(v == m, lane4, N_GROUPS), axis=-1, keepdims=True)

    gl = logits[:, 0:N_GROUPS]
    gmax, gidx = first_argmax(gl)
    g_w = 1.0 / jnp.sum(jnp.exp(gl - gmax), axis=-1, keepdims=True)
    el = jnp.zeros((tm, EXPERTS_PER_GROUP), F32)
    for g in range(N_GROUPS):
        lo = N_GROUPS + g * EXPERTS_PER_GROUP
        el = jnp.where(gidx == g, logits[:, lo:lo + EXPERTS_PER_GROUP], el)
    emax = jnp.max(el, axis=-1, keepdims=True)
    ee = jnp.exp(el - emax)
    ep = ee / jnp.sum(ee, axis=-1, keepdims=True)
    p1, i1 = first_argmax(ep)
    rest = jnp.where(lane4 == i1, -1.0, ep)
    p2, i2 = first_argmax(rest)
    den = p1 + p2
    w1 = g_w * p1 / den
    w2 = g_w * p2 / den
    first_lo = i1 < i2
    a = jnp.minimum(i1, i2)
    b = jnp.maximum(i1, i2)
    w_lo = jnp.where(first_lo, w1, w2)
    w_hi = jnp.where(first_lo, w2, w1)
    pair = jnp.where(a == 0, b - 1, jnp.where(a == 1, b + 1, N_PAIRS - 1))
    bucket = gidx * N_PAIRS + pair
    lane = lax.broadcasted_iota(I32, (tm, LANES), 1)
    onehot = lane == bucket
    prefix = jnp.dot(tri_ref[...], onehot.astype(F32).astype(BF16), preferred_element_type=F32)
    run = run_ref[...]
    rank = jnp.sum(jnp.where(onehot, prefix + run, 0.0), axis=1, keepdims=True) - 1.0
    run = run + prefix[tm - 1:tm, :]
    run_ref[...] = run
    cnt_ref[...] = run
    aux_ref[...] = (jnp.where(lane == AUX_W_LO, w_lo, 0.0) + jnp.where(lane == AUX_W_HI, w_hi, 0.0)
                    + jnp.where(lane == AUX_BUCKET, bucket.astype(F32), 0.0) + jnp.where(lane == AUX_RANK, rank, 0.0))


def _route(x, g, sc, sh, w_rg, b_rg, w_re, b_re, tm=512):
    t, d = x.shape
    w_all = jnp.concatenate([w_rg, jnp.moveaxis(w_re, 0, 1).reshape(d, N_EXPERTS)], axis=1)
    b_all = jnp.concatenate([b_rg, b_re.reshape(N_EXPERTS)])
    npad = LANES - w_all.shape[1]
    w_all = jnp.pad(w_all, ((0, 0), (0, npad)))
    b_all = jnp.pad(b_all, (0, npad)).reshape(1, LANES)
    vec = pl.BlockSpec((1, d), lambda i: (0, 0))
    r = lax.broadcasted_iota(I32, (tm, tm), 0)
    c = lax.broadcasted_iota(I32, (tm, tm), 1)
    tri = (c <= r).astype(BF16)
    return pl.pallas_call(
        _route_kernel,
        grid=(t // tm,),
        in_specs=[pl.BlockSpec((tm, d), lambda i: (i, 0)), vec, vec, vec,
                  pl.BlockSpec((d, LANES), lambda i: (0, 0)), pl.BlockSpec((1, LANES), lambda i: (0, 0)),
                  pl.BlockSpec((tm, tm), lambda i: (0, 0))],
        out_specs=[pl.BlockSpec((tm, d), lambda i: (i, 0)), pl.BlockSpec((tm, LANES), lambda i: (i, 0)),
                   pl.BlockSpec((1, LANES), lambda i: (0, 0))],
        out_shape=[jax.ShapeDtypeStruct((t, d), F32), jax.ShapeDtypeStruct((t, LANES), F32),
                   jax.ShapeDtypeStruct((1, LANES), F32)],
        scratch_shapes=[pltpu.VMEM((1, LANES), F32)],
        compiler_params=_cparams(("arbitrary",)),
        name="route",
    )(x, g.reshape(1, d), sc.reshape(1, d), sh.reshape(1, d), w_all, b_all, tri)


def _rope_block(blk, cos, sin, half):
    if half == HEAD_DIM // 2:
        partner = pltpu.roll(blk, HEAD_DIM // 2, 1)
    else:
        lane = lax.broadcasted_iota(I32, blk.shape, 1)
        first = (lane % IDX_DIM) < (IDX_DIM // 2)
        partner = jnp.where(first, pltpu.roll(blk, LANES - IDX_DIM // 2, 1), pltpu.roll(blk, IDX_DIM // 2, 1))
    return blk * cos + partner * sin


def _proj_rope_kernel(a_ref, w_ref, cos_ref, sin_ref, o_ref, *, half):
    acc = jnp.dot(a_ref[...], w_ref[...], preferred_element_type=F32)
    cos = cos_ref[...]
    sin = sin_ref[...]
    for h in range(acc.shape[1] // LANES):
        sl = slice(h * LANES, (h + 1) * LANES)
        o_ref[:, sl] = _rope_block(acc[:, sl], cos, sin, half).astype(o_ref.dtype)


def _proj_rope(a, w, col_block0, n_out, cos, sin, half, out_dtype, tm=512, tn=512):
    t, d = a.shape
    return pl.pallas_call(
        functools.partial(_proj_rope_kernel, half=half),
        grid=(t // tm, n_out // tn),
        in_specs=[pl.BlockSpec((tm, d), lambda i, j: (i, 0)),
                  pl.BlockSpec((d, tn), lambda i, j: (0, col_block0 + j)),
                  pl.BlockSpec((tm, LANES), lambda i, j: (i, 0)),
                  pl.BlockSpec((tm, LANES), lambda i, j: (i, 0))],
        out_specs=pl.BlockSpec((tm, tn), lambda i, j: (i, j)),
        out_shape=jax.ShapeDtypeStruct((t, n_out), out_dtype),
        compiler_params=_cparams(("arbitrary", "arbitrary")),
        name="proj_rope",
    )(a, w, cos, sin)


def _proj_kv_kernel(a_ref, wk_ref, wv_ref, cos_ref, sin_ref, o_ref):
    a = a_ref[...]
    k = jnp.dot(a, wk_ref[...], preferred_element_type=F32)
    v = jnp.dot(a, wv_ref[...], preferred_element_type=F32)
    cos = cos_ref[...]
    sin = sin_ref[...]
    for h in range(k.shape[1] // LANES):
        sl = slice(h * LANES, (h + 1) * LANES)
        kr = _rope_block(k[:, sl], cos, sin, HEAD_DIM // 2)
        kb = lax.bitcast_convert_type(kr.astype(BF16).astype(F32), I32)
        vb = lax.bitcast_convert_type(v[:, sl].astype(BF16).astype(F32), I32)
        o_ref[:, sl] = kb | lax.shift_right_logical(vb, 16)


def _proj_kv(a, w, kblock, vblock, cos, sin, tm=512):
    t, d = a.shape
    n = N_KV_GROUPS * HEAD_DIM
    return pl.pallas_call(
        _proj_kv_kernel,
        grid=(t // tm,),
        in_specs=[pl.BlockSpec((tm, d), lambda i: (i, 0)),
                  pl.BlockSpec((d, n), lambda i: (0, kblock)),
                  pl.BlockSpec((d, n), lambda i: (0, vblock)),
                  pl.BlockSpec((tm, LANES), lambda i: (i, 0)),
                  pl.BlockSpec((tm, LANES), lambda i: (i, 0))],
        out_specs=pl.BlockSpec((tm, n), lambda i: (i, 0)),
        out_shape=jax.ShapeDtypeStruct((t, n), I32),
        compiler_params=_cparams(("arbitrary",)),
        name="proj_kv",
    )(a, w, w, cos, sin)


QB = 256
TK = 512
RC = 128


def _indexer_kernel(qit_ref, ki_ref, wit_ref, tri_ref, idx_ref, nv_ref, d_ref, hi_ref, *, n_keep):
    b = pl.program_id(0)
    q0 = b * QB
    n_t = (q0 + QB + TK - 1) // TK
    n_rows = n_t * TK
    rowi = lax.broadcasted_iota(I32, (TK, QB), 0)
    qchunk = (q0 + lax.broadcasted_iota(I32, (TK, QB), 1)) // CHUNK

    def admissible(k0):
        return ((k0 + rowi) // CHUNK) <= qchunk

    def score_tile(kt, carry):
        k0 = pl.multiple_of(kt * TK, TK)
        ki_t = ki_ref[pl.ds(k0, TK), :]
        acc = jnp.zeros((TK, QB), F32)
        for h in range(IDX_HEADS):
            a = jnp.dot(ki_t, qit_ref[h * IDX_DIM:(h + 1) * IDX_DIM, :], preferred_element_type=F32)
            acc = acc + jnp.maximum(a, 0.0) * wit_ref[h:h + 1, :]
        s = jnp.where(admissible(k0), acc + 0.0, NEG_INF)
        u = lax.bitcast_convert_type(s, I32)
        key = jnp.where(u < 0, u ^ 0x7FFFFFFF, u)
        d_ref[pl.ds(k0, TK), :] = key
        hi_ref[pl.ds(k0, TK), :] = lax.shift_right_arithmetic(key, 16).astype(I16)
        return carry

    lax.fori_loop(0, n_t, score_tile, 0)

    def count_where(pred):
        def body(kt, acc):
            k0 = pl.multiple_of(kt * TK, TK)
            hit = pred(d_ref[pl.ds(k0, TK), :]).astype(I32)
            return acc + jnp.sum(hit.reshape(TK // 8, 8, QB), axis=0)
        acc = lax.fori_loop(0, n_t, body, jnp.zeros((8, QB), I32))
        return jnp.sum(acc, axis=0, keepdims=True)

    def signed16(v_u):
        return lax.shift_right_arithmetic(jnp.left_shift(v_u ^ 0x8000, 16), 16).astype(I16)

    def add_packed_rows(acc, hit):
        for c in range(TK // 16):
            acc = acc + hit[c * 16:(c + 1) * 16, :]
        return acc

    def count_half_where(pred):
        def body(kt, acc):
            k0 = pl.multiple_of(kt * TK, TK)
            return add_packed_rows(acc, pred(hi_ref[pl.ds(k0, TK), :]).astype(I16))
        acc = lax.fori_loop(0, n_t, body, jnp.zeros((16, QB), I16))
        return jnp.sum(acc.astype(I32), axis=0, keepdims=True)

    def select_half(base):
        def select_bit(it, carry):
            v_u, cnt_v = carry
            cand_u = v_u | jnp.left_shift(jnp.int32(1), 15 - it)
            cand16 = signed16(cand_u)
            cnt = base + count_half_where(lambda half: half >= cand16)
            keep = cnt >= n_keep
            return jnp.where(keep, cand_u, v_u), jnp.where(keep, cnt, cnt_v)
        return select_bit

    zero_i = jnp.zeros((1, QB), I32)
    high_u, cnt_thr = lax.fori_loop(0, 16, select_half(zero_i), (zero_i, jnp.full((1, QB), n_keep + 1, I32)))
    high16 = signed16(high_u)

    def to_low_halves(kt, acc):
        k0 = pl.multiple_of(kt * TK, TK)
        high = hi_ref[pl.ds(k0, TK), :]
        low = signed16(d_ref[pl.ds(k0, TK), :] & 0xFFFF)
        hi_ref[pl.ds(k0, TK), :] = jnp.where(high == high16, low, jnp.int16(-(2 ** 15)))
        return add_packed_rows(acc, (high > high16).astype(I16))

    above = lax.fori_loop(0, n_t, to_low_halves, jnp.zeros((16, QB), I16))
    above = jnp.sum(above.astype(I32), axis=0, keepdims=True)
    low_u, cnt_thr = lax.fori_loop(0, 16, select_half(above), (zero_i, cnt_thr))
    thr = (jnp.left_shift(high_u, 16) | low_u) ^ INT_MIN

    tri = tri_ref[...]
    zero = jnp.zeros((1, QB), F32)

    def place(k0, sel, run_sel):
        pre_sel = jnp.dot(tri, sel.astype(F32).astype(BF16), preferred_element_type=F32)
        rank = (run_sel + pre_sel).astype(I32) - 1
        d_ref[pl.ds(k0, TK), :] = jnp.where(sel, (k0 + rowi) - rank, INVALID_SHIFT)
        return run_sel + pre_sel[TK - 1:TK, :]

    def rank_exact_count():
        def tile(kt, run_sel):
            k0 = pl.multiple_of(kt * TK, TK)
            return place(k0, admissible(k0) & (d_ref[pl.ds(k0, TK), :] >= thr), run_sel)
        return lax.fori_loop(0, n_t, tile, zero)

    def rank_with_ties():
        need = (n_keep - count_where(lambda key: key > thr)).astype(F32)

        def tile(kt, carry):
            run_eq, run_sel = carry
            k0 = pl.multiple_of(kt * TK, TK)
            key = d_ref[pl.ds(k0, TK), :]
            eq = key == thr
            eq_f = eq.astype(F32)
            pre_eq = jnp.dot(tri, eq_f.astype(BF16), preferred_element_type=F32)
            sel = admissible(k0) & ((key > thr) | (eq & ((run_eq + pre_eq - eq_f) < need)))
            return run_eq + pre_eq[TK - 1:TK, :], place(k0, sel, run_sel)
        return lax.fori_loop(0, n_t, tile, (zero, zero))[1]

    n_sel = lax.cond(jnp.max(jnp.abs(cnt_thr - n_keep)) == 0, rank_exact_count, rank_with_ties)

    p_ref = hi_ref
    packed_rows = 16
    d_ref[pl.ds(pl.multiple_of(n_rows, RC), RC), :] = jnp.full((RC, QB), INVALID_SHIFT, I32)
    p_ref[pl.ds(pl.multiple_of(n_rows, RC), RC), :] = jnp.full((RC, QB), INVALID_PACKED, I16)
    n_chunks = n_rows // RC
    win_chunks = -(-n_keep // RC)
    window = win_chunks * RC

    def first_stage(c, carry):
        r0 = pl.multiple_of(c * RC, RC)
        x = d_ref[pl.ds(r0, RC + 8), :]
        own, inc = x[0:RC], x[1:1 + RC]
        take = (inc & (1 | INVALID_SHIFT)) == 1
        stay = (own & (1 | INVALID_SHIFT)) == 0
        moved = jnp.where(take, inc, jnp.where(stay, own, INVALID_SHIFT))
        p_ref[pl.ds(r0, RC), :] = jnp.minimum(moved, INVALID_PACKED).astype(I16)
        return carry

    lax.fori_loop(0, n_chunks, first_stage, 0)
    n_stages = (d_ref.shape[0] - RC - 1).bit_length()
    for s in range(1, n_stages):
        sh = 1 << s
        test = sh | INVALID_PACKED

        def merge(own, inc, sh=sh, test=test):
            take = (inc & test) == sh
            stay = (own & test) == 0
            return jnp.where(take, inc, jnp.where(stay, own, INVALID_PACKED))

        def chunk_with_source(c, carry, sh=sh, merge=merge):
            r0 = pl.multiple_of(c * RC, RC)
            if sh < packed_rows:
                x = p_ref[pl.ds(r0, RC + packed_rows), :]
                own, inc = x[0:RC], x[sh:sh + RC]
            else:
                own = p_ref[pl.ds(r0, RC), :]
                inc = p_ref[pl.ds(pl.multiple_of(r0 + sh, packed_rows), RC), :]
            p_ref[pl.ds(r0, RC), :] = merge(own, inc)
            return carry

        def chunk_without_source(c, carry, merge=merge):
            r0 = pl.multiple_of(c * RC, RC)
            own = p_ref[pl.ds(r0, RC), :]
            p_ref[pl.ds(r0, RC), :] = merge(own, jnp.full_like(own, INVALID_PACKED))
            return carry

        def window_chunk(c, carry, sh=sh, merge=merge):
            r0 = pl.multiple_of((c // win_chunks) * sh + (c % win_chunks) * RC, RC)
            src = r0 + sh
            own = p_ref[pl.ds(r0, RC), :]
            inc = p_ref[pl.ds(pl.multiple_of(jnp.minimum(src, n_rows), RC), RC), :]
            p_ref[pl.ds(r0, RC), :] = merge(own, inc)
            return carry

        if sh < RC:
            lax.fori_loop(0, n_chunks, chunk_with_source, 0)
        elif sh < 2 * window:
            n_src = jnp.maximum(n_rows - sh, 0) // RC
            lax.fori_loop(0, n_src, chunk_with_source, 0)
            lax.fori_loop(n_src, n_chunks, chunk_without_source, 0)
        else:
            lax.fori_loop(0, ((n_rows + sh - 1) // sh) * win_chunks, window_chunk, 0)

    slot = lax.broadcasted_iota(I32, (n_keep, QB), 0)
    dd = p_ref[0:n_keep, :].astype(I32)
    idx_ref[...] = jnp.where(dd < INVALID_PACKED, (slot + dd) * KV_ROWS, 0)
    nv_ref[...] = n_sel.astype(I32)


def _indexer(qit, ki, wit, n_keep):
    t = ki.shape[0]
    assert t <= INVALID_PACKED and t % TK == 0
    r = lax.broadcasted_iota(I32, (TK, TK), 0)
    c = lax.broadcasted_iota(I32, (TK, TK), 1)
    tri = (c <= r).astype(BF16)
    return pl.pallas_call(
        functools.partial(_indexer_kernel, n_keep=n_keep),
        grid=(t // QB,),
        in_specs=[pl.BlockSpec((IDX_HEADS * IDX_DIM, QB), lambda b: (0, b)),
                  pl.BlockSpec((t, IDX_DIM), lambda b: (0, 0)),
                  pl.BlockSpec((IDX_HEADS, QB), lambda b: (0, b)),
                  pl.BlockSpec((TK, TK), lambda b: (0, 0))],
        out_specs=[pl.BlockSpec((n_keep, QB), lambda b: (0, b)), pl.BlockSpec((1, QB), lambda b: (0, b))],
        out_shape=[jax.ShapeDtypeStruct((n_keep, t), I32), jax.ShapeDtypeStruct((1, t), I32)],
        scratch_shapes=[pltpu.VMEM((t + RC, QB), I32), pltpu.VMEM((t + RC, QB), I16)],
        compiler_params=_cparams(("arbitrary",)),
        name="indexer",
    )(qit, ki, wit, tri)


GQ = 128
GATTN_UNROLL = 8


def _gattn_kernel(nv_ref, idx_ref, q_ref, kv_hbm, o_ref, kv_vmem, stage_a, stage_b, stage_c, stage_d,
                  s_a, s_b, row_lists, sem_kv, sem_rows, *, n_keep):
    i = pl.program_id(0)

    @pl.when(i == 0)
    def _():
        cp = pltpu.make_async_copy(kv_hbm, kv_vmem, sem_kv)
        cp.start()
        cp.wait()

    cp = pltpu.make_async_copy(idx_ref, row_lists, sem_rows)
    cp.start()
    cp.wait()

    n_col = n_keep * KV_ROWS
    colg = lax.broadcasted_iota(I32, (N_HEADS, n_col), 1)
    rowh = lax.broadcasted_iota(I32, (N_HEADS, n_col), 0)
    own_group = (colg % KV_ROWS) == (rowh // HEADS_PER_GROUP)
    col_slot = colg // KV_ROWS
    scale = HEAD_DIM ** -0.5

    last = GQ - 1
    stages = (stage_a, stage_b, stage_c, stage_d)
    s_bufs = (s_a, s_b)

    def gather(r, stage):
        rows = row_lists.at[pl.ds(jnp.minimum(r, last), 1)]
        for j in range(n_keep):
            row = pl.multiple_of(rows[0, j], KV_ROWS)
            stage[j * KV_ROWS:(j + 1) * KV_ROWS, :] = kv_vmem[pl.ds(row, KV_ROWS), :]

    def scores(r, stage, s_buf):
        k_all = lax.bitcast_convert_type(stage[...] & jnp.int32(-65536), F32).astype(BF16)
        qh = q_ref[pl.ds(pl.multiple_of(r * N_HEADS, N_HEADS), N_HEADS), :]
        s = lax.dot_general(qh, k_all, (((1,), (1,)), ((), ())), preferred_element_type=F32) * scale
        mask = own_group & (col_slot < nv_ref[i * GQ + r])
        s_buf[...] = jnp.where(mask, s, NEG_INF)

    def combine(r, stage, s_buf):
        s = s_buf[...]
        p = jnp.exp(s - jnp.max(s, axis=1, keepdims=True))
        l = jnp.sum(p, axis=1, keepdims=True)
        v_all = lax.bitcast_convert_type(jnp.left_shift(stage[...], 16), F32).astype(BF16)
        o = jnp.dot(p.astype(BF16), v_all, preferred_element_type=F32) / l
        o_ref[pl.ds(pl.multiple_of(r * N_HEADS, N_HEADS), N_HEADS), :] = o.astype(o_ref.dtype)

    gather(0, stages[0])
    gather(1, stages[1])
    scores(0, stages[0], s_bufs[0])

    def query_group(it, carry):
        r = GATTN_UNROLL * it
        for u in range(GATTN_UNROLL):
            gather(r + u + 2, stages[(u + 2) % 4])
            scores(jnp.minimum(r + u + 1, last), stages[(u + 1) % 4], s_bufs[(u + 1) % 2])
            combine(r + u, stages[u % 4], s_bufs[u % 2])
        return carry

    lax.fori_loop(0, GQ // GATTN_UNROLL, query_group, 0)


def _gattn(nv, idx, q2, kv2, n_keep):
    t = nv.shape[0]
    grid_spec = pltpu.PrefetchScalarGridSpec(
        num_scalar_prefetch=1,
        grid=(t // GQ,),
        in_specs=[pl.BlockSpec((GQ, n_keep), lambda i, nv: (i, 0)),
                  pl.BlockSpec((GQ * N_HEADS, HEAD_DIM), lambda i, nv: (i, 0)),
                  pl.BlockSpec(memory_space=pl.ANY)],
        out_specs=pl.BlockSpec((GQ * N_HEADS, HEAD_DIM), lambda i, nv: (i, 0)),
        scratch_shapes=[pltpu.VMEM(kv2.shape, I32),
                        pltpu.VMEM((n_keep * KV_ROWS, LANES), I32),
                        pltpu.VMEM((n_keep * KV_ROWS, LANES), I32),
                        pltpu.VMEM((n_keep * KV_ROWS, LANES), I32),
                        pltpu.VMEM((n_keep * KV_ROWS, LANES), I32),
                        pltpu.VMEM((N_HEADS, n_keep * KV_ROWS), F32),
                        pltpu.VMEM((N_HEADS, n_keep * KV_ROWS), F32),
                        pltpu.SMEM((GQ, n_keep), I32),
                        pltpu.SemaphoreType.DMA,
                        pltpu.SemaphoreType.DMA],
    )
    return pl.pallas_call(
        functools.partial(_gattn_kernel, n_keep=n_keep),
        grid_spec=grid_spec,
        out_shape=jax.ShapeDtypeStruct((t * N_HEADS, HEAD_DIM), BF16),
        compiler_params=_cparams(("arbitrary",)),
        name="gattn",
    )(nv, idx, q2, kv2)


def _mm_res_kernel(a_ref, w_ref, b_ref, x_ref, g_ref, o_ref):
    y = jnp.dot(a_ref[...], w_ref[...], preferred_element_type=F32) + b_ref[...]
    o_ref[...] = x_ref[...] + g_ref[...] * y


def _mm_res(a, w, bias, x, gate, tm=512, tn=512):
    t, k = a.shape
    n = w.shape[1]
    return pl.pallas_call(
        _mm_res_kernel,
        grid=(t // tm, n // tn),
        in_specs=[pl.BlockSpec((tm, k), lambda i, j: (i, 0)),
                  pl.BlockSpec((k, tn), lambda i, j: (0, j)),
                  pl.BlockSpec((1, tn), lambda i, j: (0, j)),
                  pl.BlockSpec((tm, tn), lambda i, j: (i, j)),
                  pl.BlockSpec((1, tn), lambda i, j: (0, j))],
        out_specs=pl.BlockSpec((tm, tn), lambda i, j: (i, j)),
        out_shape=jax.ShapeDtypeStruct((t, n), F32),
        compiler_params=_cparams(("arbitrary", "arbitrary")),
        name="mm_res",
    )(a, w, bias.reshape(1, n), x, gate.reshape(1, n))


def _glu_kernel(a_ref, wa_ref, wg_ref, ba_ref, bg_ref, o_ref):
    a = a_ref[...]
    ya = jnp.dot(a, wa_ref[...], preferred_element_type=F32) + ba_ref[...]
    yg = jnp.dot(a, wg_ref[...], preferred_element_type=F32) + bg_ref[...]
    o_ref[...] = ya * jax.nn.sigmoid(yg)


def _glu(a, w, b, tm=512, tn=512):
    t, k = a.shape
    n = w.shape[1] // 2
    nb = n // tn
    b2 = b.reshape(1, 2 * n)
    return pl.pallas_call(
        _glu_kernel,
        grid=(t // tm, nb),
        in_specs=[pl.BlockSpec((tm, k), lambda i, j: (i, 0)),
                  pl.BlockSpec((k, tn), lambda i, j: (0, j)),
                  pl.BlockSpec((k, tn), lambda i, j: (0, j + nb)),
                  pl.BlockSpec((1, tn), lambda i, j: (0, j)),
                  pl.BlockSpec((1, tn), lambda i, j: (0, j + nb))],
        out_specs=pl.BlockSpec((tm, tn), lambda i, j: (i, j)),
        out_shape=jax.ShapeDtypeStruct((t, n), F32),
        compiler_params=_cparams(("arbitrary", "arbitrary")),
        name="glu",
    )(a, w, w, b2, b2)


HALO = 32


SUBLANES = 8
CONV_ROWS = 32
CONV_COLS = 512


def _dwconv_ln_kernel(u_ref, prev_ref, w_ref, b_ref, g_ref, beta_ref, o_ref, ext_ref, sh_ref, conv_ref):
    i = pl.program_id(0)
    tm, d = u_ref.shape
    ext_ref[0:HALO, :] = jnp.where(i == 0, 0.0, prev_ref[...])
    ext_ref[HALO:HALO + tm, :] = u_ref[...]
    for phase in range(SUBLANES):
        sh_ref[phase, 0:tm + HALO - phase, :] = ext_ref[phase:tm + HALO, :]
    first = HALO - (CONV_WIDTH - 1)

    def conv_rows(c, carry):
        r0 = pl.multiple_of(c * CONV_ROWS, CONV_ROWS)
        for cb in range(d // CONV_COLS):
            cols = slice(cb * CONV_COLS, (cb + 1) * CONV_COLS)
            acc = jnp.zeros((CONV_ROWS, CONV_COLS), F32) + b_ref[:, cols]
            for k in range(CONV_WIDTH):
                phase = (first + k) % SUBLANES
                start = pl.multiple_of(r0 + (first + k - phase), SUBLANES)
                acc = acc + sh_ref[phase, pl.ds(start, CONV_ROWS), cols] * w_ref[k:k + 1, cols]
            conv_ref[pl.ds(r0, CONV_ROWS), cols] = acc
        return carry

    lax.fori_loop(0, tm // CONV_ROWS, conv_rows, 0)
    acc = conv_ref[...]
    mu = jnp.mean(acc, axis=-1, keepdims=True)
    cen = acc - mu
    var = jnp.mean(cen * cen, axis=-1, keepdims=True)
    y = (cen * lax.rsqrt(var + EPS)) * g_ref[...] + beta_ref[...]
    o_ref[...] = (y * jax.nn.sigmoid(y)).astype(o_ref.dtype)


def _dwconv_ln(u, w_dw, b_dw, ln_g, ln_b, tm=256):
    t, d = u.shape
    ratio = tm // HALO
    vec = pl.BlockSpec((1, d), lambda i: (0, 0))
    return pl.pallas_call(
        _dwconv_ln_kernel,
        grid=(t // tm,),
        in_specs=[pl.BlockSpec((tm, d), lambda i: (i, 0)),
                  pl.BlockSpec((HALO, d), lambda i: (jnp.maximum(i * ratio - 1, 0), 0)),
                  pl.BlockSpec((HALO, d), lambda i: (0, 0)),
                  vec, vec, vec],
        out_specs=pl.BlockSpec((tm, d), lambda i: (i, 0)),
        out_shape=jax.ShapeDtypeStruct((t, d), BF16),
        scratch_shapes=[pltpu.VMEM((HALO + tm, d), F32), pltpu.VMEM((SUBLANES, HALO + tm, d), F32),
                        pltpu.VMEM((tm, d), F32)],
        compiler_params=_cparams(("arbitrary",)),
        name="dwconv_ln",
    )(u, u, jnp.pad(w_dw, ((0, HALO - CONV_WIDTH), (0, 0))), b_dw.reshape(1, d), ln_g.reshape(1, d), ln_b.reshape(1, d))


def _row_copy(src, src_row, dst, dst_row, sem):
    return pltpu.make_async_copy(src.at[pl.ds(src_row, 1), :], dst.at[pl.ds(dst_row, 1), :], sem)


ROW_UNROLL = 8


def _for_rows(n, fn):
    def group(gi, carry):
        for u in range(ROW_UNROLL):
            fn(gi * ROW_UNROLL + u, carry)
        return carry

    lax.fori_loop(0, n // ROW_UNROLL, group, 0)


def _dispatch_kernel(dest_ref, h_ref, aux_ref, xs_in, ws_in, xs_hbm, ws_hbm, sem_x, sem_w):
    del xs_in, ws_in
    tm = h_ref.shape[0]
    t0 = pl.program_id(0) * tm

    def issue(r, carry):
        slot = dest_ref[t0 + r]
        _row_copy(h_ref, r, xs_hbm, slot, sem_x).start()
        _row_copy(aux_ref, r, ws_hbm, slot, sem_w).start()
        return carry

    def drain(r, carry):
        _row_copy(h_ref, r, xs_hbm, 0, sem_x).wait()
        _row_copy(aux_ref, r, ws_hbm, 0, sem_w).wait()
        return carry

    _for_rows(tm, issue)
    _for_rows(tm, drain)


def _dispatch(dest, h, aux, n_rows, tm=256):
    t, d = h.shape
    grid_spec = pltpu.PrefetchScalarGridSpec(
        num_scalar_prefetch=1,
        grid=(t // tm,),
        in_specs=[pl.BlockSpec((tm, d), lambda i, dest: (i, 0)),
                  pl.BlockSpec((tm, LANES), lambda i, dest: (i, 0)),
                  pl.BlockSpec(memory_space=pl.ANY),
                  pl.BlockSpec(memory_space=pl.ANY)],
        out_specs=[pl.BlockSpec(memory_space=pl.ANY), pl.BlockSpec(memory_space=pl.ANY)],
        scratch_shapes=[pltpu.SemaphoreType.DMA, pltpu.SemaphoreType.DMA],
    )
    return pl.pallas_call(
        _dispatch_kernel,
        grid_spec=grid_spec,
        out_shape=[jax.ShapeDtypeStruct((n_rows, d), F32), jax.ShapeDtypeStruct((n_rows, LANES), F32)],
        input_output_aliases={3: 0, 4: 1},
        compiler_params=_cparams(("arbitrary",)),
        name="dispatch",
    )(dest, h, aux, jnp.zeros((n_rows, d), F32), jnp.zeros((n_rows, LANES), F32))


def _ffn_kernel(elo_ref, ehi_ref, live_ref, xs_ref, ws_ref, wg_lo, wu_lo, wd_lo, wg_hi, wu_hi, wd_hi, ys_ref):
    b = pl.program_id(0)

    @pl.when(live_ref[b] == 1)
    def _():
        x = xs_ref[...].astype(BF16)
        ws = ws_ref[...]

        def expert(wg, wu, wd):
            gate = jnp.dot(x, wg[0], preferred_element_type=F32)
            up = jnp.dot(x, wu[0], preferred_element_type=F32)
            hid = (gate * jax.nn.sigmoid(gate)) * up
            return jnp.dot(hid.astype(BF16), wd[0], preferred_element_type=F32)

        ys_ref[...] = (expert(wg_lo, wu_lo, wd_lo) * ws[:, AUX_W_LO:AUX_W_LO + 1]
                       + expert(wg_hi, wu_hi, wd_hi) * ws[:, AUX_W_HI:AUX_W_HI + 1])

    @pl.when(live_ref[b] == 0)
    def _():
        ys_ref[...] = jnp.zeros_like(ys_ref)


def _ffn(e_lo, e_hi, live, xs, ws, w_gate, w_up, w_down):
    n_rows, d = xs.shape
    _, _, de = w_gate.shape
    lo_in = pl.BlockSpec((1, d, de), lambda b, lo, hi, live: (lo[b], 0, 0))
    hi_in = pl.BlockSpec((1, d, de), lambda b, lo, hi, live: (hi[b], 0, 0))
    lo_out = pl.BlockSpec((1, de, d), lambda b, lo, hi, live: (lo[b], 0, 0))
    hi_out = pl.BlockSpec((1, de, d), lambda b, lo, hi, live: (hi[b], 0, 0))
    grid_spec = pltpu.PrefetchScalarGridSpec(
        num_scalar_prefetch=3,
        grid=(n_rows // MOE_TM,),
        in_specs=[pl.BlockSpec((MOE_TM, d), lambda b, lo, hi, live: (b, 0)),
                  pl.BlockSpec((MOE_TM, LANES), lambda b, lo, hi, live: (b, 0)),
                  lo_in, lo_in, lo_out, hi_in, hi_in, hi_out],
        out_specs=pl.BlockSpec((MOE_TM, d), lambda b, lo, hi, live: (b, 0)),
    )
    return pl.pallas_call(
        _ffn_kernel,
        grid_spec=grid_spec,
        out_shape=jax.ShapeDtypeStruct((n_rows, d), F32),
        compiler_params=_cparams(("arbitrary",)),
        name="ffn",
    )(e_lo, e_hi, live, xs, ws, w_gate, w_up, w_down, w_gate, w_up, w_down)


def _combine_kernel(dest_ref, x_ref, g_ref, fg_ref, ys_hbm, o_ref, buf, sem, *, final_norm):
    tm = x_ref.shape[0]
    t0 = pl.program_id(0) * tm

    def issue(r, carry):
        _row_copy(ys_hbm, dest_ref[t0 + r], buf, r, sem).start()
        return carry

    def drain(r, carry):
        _row_copy(ys_hbm, 0, buf, r, sem).wait()
        return carry

    _for_rows(tm, issue)
    _for_rows(tm, drain)
    y = x_ref[...] + g_ref[...] * buf[...]
    if final_norm:
        y = (y * lax.rsqrt(jnp.mean(y * y, axis=-1, keepdims=True) + EPS)) * fg_ref[...]
    o_ref[...] = y


def _combine(dest, x, gate, final_g, ys, final_norm, tm=256):
    t, d = x.shape
    vec = pl.BlockSpec((1, d), lambda i, dest: (0, 0))
    grid_spec = pltpu.PrefetchScalarGridSpec(
        num_scalar_prefetch=1,
        grid=(t // tm,),
        in_specs=[pl.BlockSpec((tm, d), lambda i, dest: (i, 0)), vec, vec, pl.BlockSpec(memory_space=pl.ANY)],
        out_specs=pl.BlockSpec((tm, d), lambda i, dest: (i, 0)),
        scratch_shapes=[pltpu.VMEM((tm, d), F32), pltpu.SemaphoreType.DMA],
    )
    return pl.pallas_call(
        functools.partial(_combine_kernel, final_norm=final_norm),
        grid_spec=grid_spec,
        out_shape=jax.ShapeDtypeStruct((t, d), F32),
        compiler_params=_cparams(("arbitrary",)),
        name="combine",
    )(dest, x, gate.reshape(1, d), final_g.reshape(1, d), ys)


def _rope_tables(t):
    pos = jnp.arange(t, dtype=I32).astype(F32)[:, None]
    lane = jnp.arange(LANES)

    def table(dim):
        half = dim // 2
        inv = ROPE_THETA ** (-jnp.arange(half, dtype=F32) / half)
        ang = pos * inv[None, :]
        j = lane % dim
        cos = jnp.cos(ang)[:, j % half]
        sin = jnp.sin(ang)[:, j % half] * jnp.where(j < half, -1.0, 1.0)[None, :]
        return cos, sin

    return table(HEAD_DIM), table(IDX_DIM)


def _attention_layer(x, mod, norm_g, w_in, w_out, tables):
    t, d = x.shape
    sh1, sc1, g1 = mod[0:d], mod[d:2 * d], mod[2 * d:3 * d]
    (cos_h, sin_h), (cos_i, sin_i) = tables
    hn = _normmod(x, norm_g, sc1, sh1)
    w_bf = w_in.astype(BF16)
    nq = N_HEADS * HEAD_DIM
    nkv = N_KV_GROUPS * HEAD_DIM
    nqi = IDX_HEADS * IDX_DIM
    tn = 512
    q = _proj_rope(hn, w_bf, 0, nq, cos_h, sin_h, HEAD_DIM // 2, BF16)
    kv = _proj_kv(hn, w_bf, nq // nkv, nq // nkv + 1, cos_h, sin_h)
    qi = _proj_rope(hn, w_bf, (nq + 2 * nkv) // tn, nqi, cos_i, sin_i, IDX_DIM // 2, BF16)
    tail0 = nq + 2 * nkv + nqi
    w_tail = jnp.pad(w_bf[:, tail0:], ((0, 0), (0, LANES - (IDX_DIM + IDX_HEADS))))
    lane = jnp.arange(LANES)
    wi_scale = (IDX_HEADS ** -0.5) * (IDX_DIM ** -0.5)
    cos_t = jnp.where(lane[None, :] < IDX_DIM, cos_i, wi_scale)
    sin_t = jnp.where(lane[None, :] < IDX_DIM, sin_i, 0.0)
    tail = _proj_rope(hn, w_tail, 0, LANES, cos_t, sin_t, IDX_DIM // 2, F32, tn=LANES)
    ki = tail[:, :IDX_DIM].astype(BF16)
    wit = tail[:, IDX_DIM:IDX_DIM + IDX_HEADS].T
    n_keep = min(TOPK_KEYS, t // 4)
    idx_t, nv = _indexer(qi.T, ki, wit, n_keep)
    att = _gattn(nv.reshape(t), idx_t.T, q.reshape(t * N_HEADS, HEAD_DIM), kv.reshape(t * KV_ROWS, LANES), n_keep)
    att = att.reshape(t, nq)
    return _mm_res(att, w_out.astype(BF16), jnp.zeros((d,), F32), x, g1)


def _conv_layer(x, mod, norm_g, w_pw1, b_pw1, w_dw, b_dw, ln_g, ln_b, w_pw2, b_pw2):
    t, d = x.shape
    sh1, sc1, g1 = mod[0:d], mod[d:2 * d], mod[2 * d:3 * d]
    hn = _normmod(x, norm_g, sc1, sh1)
    u = _glu(hn, w_pw1.astype(BF16), b_pw1)
    v = _dwconv_ln(u, w_dw, b_dw, ln_g, ln_b)
    return _mm_res(v, w_pw2.astype(BF16), b_pw2, x, g1)


def _moe_layer(x, mod, norm_g, w_rg, b_rg, w_re, b_re, w_gate, w_up, w_down, final_g, final_norm):
    t, d = x.shape
    sh2, sc2, g2 = mod[3 * d:4 * d], mod[4 * d:5 * d], mod[5 * d:6 * d]
    h, aux, counts = _route(x, norm_g, sc2, sh2, w_rg, b_rg, w_re, b_re)
    bucket = aux[:, AUX_BUCKET].astype(I32)
    rank = aux[:, AUX_RANK].astype(I32)
    sizes = counts[0, :N_BUCKETS].astype(I32)
    padded = (sizes + MOE_TM - 1) // MOE_TM * MOE_TM
    ends = jnp.cumsum(padded)
    dest = (ends - padded)[bucket] + rank
    n_blocks = t // MOE_TM + N_BUCKETS
    blk_start = jnp.arange(n_blocks, dtype=I32) * MOE_TM
    blk_bucket = jnp.minimum(jnp.searchsorted(ends, blk_start, side='right'), N_BUCKETS - 1).astype(I32)
    live = (blk_start < ends[-1]).astype(I32)
    group0 = (blk_bucket // N_PAIRS) * EXPERTS_PER_GROUP
    e_lo = group0 + jnp.asarray(PAIR_LO, I32)[blk_bucket % N_PAIRS]
    e_hi = group0 + jnp.asarray(PAIR_HI, I32)[blk_bucket % N_PAIRS]
    xs, ws = _dispatch(dest, h, aux, n_blocks * MOE_TM)
    ys = _ffn(e_lo, e_hi, live, xs, ws, w_gate.astype(BF16), w_up.astype(BF16), w_down.astype(BF16))
    return _combine(dest, x, g2, final_g, ys, final_norm)


def kernel(x, c, ada_w, ada_b, norm1_g, norm2_g, attn_w_in, attn_w_out, conv_w_pw1, conv_b_pw1, conv_w_dw, conv_b_dw, conv_ln_g, conv_ln_b, conv_w_pw2, conv_b_pw2, router_w_group, router_b_group, router_w_expert, router_b_expert, exp_w_gate, exp_w_up, exp_w_down, final_g):
    bsz, t, d = x.shape
    assert bsz == 1 and d == N_HEADS * HEAD_DIM
    depth = ada_w.shape[0]
    mods = _ada_mod(c, ada_w, ada_b)
    tables = _rope_tables(t)
    xf = x.reshape(t, d)
    for i in range(depth):
        j = i // 2
        if i % 2 == 0:
            xf = _attention_layer(xf, mods[i], norm1_g[i], attn_w_in[j], attn_w_out[j], tables)
        else:
            xf = _conv_layer(xf, mods[i], norm1_g[i], conv_w_pw1[j], conv_b_pw1[j], conv_w_dw[j], conv_b_dw[j],
                             conv_ln_g[j], conv_ln_b[j], conv_w_pw2[j], conv_b_pw2[j])
        xf = _moe_layer(xf, mods[i], norm2_g[i], router_w_group[i], router_b_group[i], router_w_expert[i],
                        router_b_expert[i], exp_w_gate[i], exp_w_up[i], exp_w_down[i], final_g, i == depth - 1)
    return xf.reshape(bsz, t, d)
```

```python
import functools

import jax
import jax.numpy as jnp
from jax import lax
from jax.experimental import pallas as pl
from jax.experimental.pallas import tpu as pltpu

EPS = 1e-6
CHUNK = 64
N_HEADS = 16
N_KV_GROUPS = 4
HEADS_PER_GROUP = N_HEADS // N_KV_GROUPS
HEAD_DIM = 128
IDX_HEADS = 16
IDX_DIM = 64
TOPK_KEYS = 256
ROPE_THETA = 10000.0
NEG_INF = -1e30
CONV_WIDTH = 31
N_GROUPS = 4
EXPERTS_PER_GROUP = 4
N_EXPERTS = N_GROUPS * EXPERTS_PER_GROUP

LANES = 128
KV_ROWS = N_KV_GROUPS
INT_MIN = -(2 ** 31)
INVALID_SHIFT = 1 << 30
INVALID_PACKED = 1 << 14
VMEM_LIMIT = 56 * 1024 * 1024

F32 = jnp.float32
BF16 = jnp.bfloat16
I32 = jnp.int32
I16 = jnp.int16


def _cparams(sem):
    return pltpu.CompilerParams(dimension_semantics=sem, vmem_limit_bytes=VMEM_LIMIT)


def _ada_kernel(c_ref, w_ref, b_ref, o_ref):
    c = c_ref[...]
    ca = c * jax.nn.sigmoid(c)
    o_ref[0] = jnp.dot(ca, w_ref[0], preferred_element_type=F32, precision=lax.Precision.HIGHEST) + b_ref[0]


def _ada_mod(c, ada_w, ada_b):
    depth, d, n = ada_w.shape
    tn = 768
    c8 = jnp.zeros((8, d), F32).at[0].set(c[0])
    out = pl.pallas_call(
        _ada_kernel,
        grid=(depth, n // tn),
        in_specs=[pl.BlockSpec((8, d), lambda i, j: (0, 0)),
                  pl.BlockSpec((1, d, tn), lambda i, j: (i, 0, j)),
                  pl.BlockSpec((1, 1, tn), lambda i, j: (i, 0, j))],
        out_specs=pl.BlockSpec((1, 8, tn), lambda i, j: (i, 0, j)),
        out_shape=jax.ShapeDtypeStruct((depth, 8, n), F32),
        compiler_params=_cparams(("arbitrary", "arbitrary")),
        name="ada_mod",
    )(c8, ada_w, ada_b.reshape(depth, 1, n))
    return out[:, 0, :]


def _normmod_kernel(x_ref, g_ref, sc_ref, sh_ref, o_ref):
    x = x_ref[...]
    xn = x * lax.rsqrt(jnp.mean(x * x, axis=-1, keepdims=True) + EPS)
    o_ref[...] = ((xn * g_ref[...]) * (1.0 + sc_ref[...]) + sh_ref[...]).astype(o_ref.dtype)


def _normmod(x, g, sc, sh, tm=512):
    t, d = x.shape
    vec = pl.BlockSpec((1, d), lambda i: (0, 0))
    return pl.pallas_call(
        _normmod_kernel,
        grid=(t // tm,),
        in_specs=[pl.BlockSpec((tm, d), lambda i: (i, 0)), vec, vec, vec],
        out_specs=pl.BlockSpec((tm, d), lambda i: (i, 0)),
        out_shape=jax.ShapeDtypeStruct((t, d), BF16),
        compiler_params=_cparams(("arbitrary",)),
        name="normmod",
    )(x, g.reshape(1, d), sc.reshape(1, d), sh.reshape(1, d))


N_PAIRS = EXPERTS_PER_GROUP * (EXPERTS_PER_GROUP - 1) // 2
N_BUCKETS = N_GROUPS * N_PAIRS
PAIR_LO = (0, 0, 0, 1, 1, 2)
PAIR_HI = (1, 2, 3, 2, 3, 3)
MOE_TM = 256
AUX_W_LO, AUX_W_HI, AUX_BUCKET, AUX_RANK = 0, 1, 2, 3


def _route_kernel(x_ref, g_ref, sc_ref, sh_ref, wr_ref, br_ref, tri_ref, h_ref, aux_ref, cnt_ref, run_ref):
    @pl.when(pl.program_id(0) == 0)
    def _():
        run_ref[...] = jnp.zeros_like(run_ref)

    x = x_ref[...]
    xn = x * lax.rsqrt(jnp.mean(x * x, axis=-1, keepdims=True) + EPS)
    h = (xn * g_ref[...]) * (1.0 + sc_ref[...]) + sh_ref[...]
    h_ref[...] = h.astype(h_ref.dtype)
    h_hi = h.astype(BF16)
    h_lo = (h - h_hi.astype(F32)).astype(BF16)
    w = wr_ref[...]
    w_hi = w.astype(BF16)
    w_lo = (w - w_hi.astype(F32)).astype(BF16)
    logits = (jnp.dot(h_hi, w_hi, preferred_element_type=F32) + jnp.dot(h_lo, w_hi, preferred_element_type=F32)
              + jnp.dot(h_hi, w_lo, preferred_element_type=F32)) + br_ref[...]
    tm = x.shape[0]
    lane4 = lax.broadcasted_iota(I32, (tm, N_GROUPS), 1)

    def first_argmax(v):
        m = jnp.max(v, axis=-1, keepdims=True)
        return m, jnp.min(jnp.where(v == m, lane4, N_GROUPS), axis=-1, keepdims=True)

    gl = logits[:, 0:N_GROUPS]
    gmax, gidx = first_argmax(gl)
    g_w = 1.0 / jnp.sum(jnp.exp(gl - gmax), axis=-1, keepdims=True)
    el = jnp.zeros((tm, EXPERTS_PER_GROUP), F32)
    for g in range(N_GROUPS):
        lo = N_GROUPS + g * EXPERTS_PER_GROUP
        el = jnp.where(gidx == g, logits[:, lo:lo + EXPERTS_PER_GROUP], el)
    emax = jnp.max(el, axis=-1, keepdims=True)
    ee = jnp.exp(el - emax)
    ep = ee / jnp.sum(ee, axis=-1, keepdims=True)
    p1, i1 = first_argmax(ep)
    rest = jnp.where(lane4 == i1, -1.0, ep)
    p2, i2 = first_argmax(rest)
    den = p1 + p2
    w1 = g_w * p1 / den
    w2 = g_w * p2 / den
    first_lo = i1 < i2
    a = jnp.minimum(i1, i2)
    b = jnp.maximum(i1, i2)
    w_lo = jnp.where(first_lo, w1, w2)
    w_hi = jnp.where(first_lo, w2, w1)
    pair = jnp.where(a == 0, b - 1, jnp.where(a == 1, b + 1, N_PAIRS - 1))
    bucket = gidx * N_PAIRS + pair
    lane = lax.broadcasted_iota(I32, (tm, LANES), 1)
    onehot = lane == bucket
    prefix = jnp.dot(tri_ref[...], onehot.astype(F32).astype(BF16), preferred_element_type=F32)
    run = run_ref[...]
    rank = jnp.sum(jnp.where(onehot, prefix + run, 0.0), axis=1, keepdims=True) - 1.0
    run = run + prefix[tm - 1:tm, :]
    run_ref[...] = run
    cnt_ref[...] = run
    aux_ref[...] = (jnp.where(lane == AUX_W_LO, w_lo, 0.0) + jnp.where(lane == AUX_W_HI, w_hi, 0.0)
                    + jnp.where(lane == AUX_BUCKET, bucket.astype(F32), 0.0) + jnp.where(lane == AUX_RANK, rank, 0.0))


def _route(x, g, sc, sh, w_rg, b_rg, w_re, b_re, tm=512):
    t, d = x.shape
    w_all = jnp.concatenate([w_rg, jnp.moveaxis(w_re, 0, 1).reshape(d, N_EXPERTS)], axis=1)
    b_all = jnp.concatenate([b_rg, b_re.reshape(N_EXPERTS)])
    npad = LANES - w_all.shape[1]
    w_all = jnp.pad(w_all, ((0, 0), (0, npad)))
    b_all = jnp.pad(b_all, (0, npad)).reshape(1, LANES)
    vec = pl.BlockSpec((1, d), lambda i: (0, 0))
    r = lax.broadcasted_iota(I32, (tm, tm), 0)
    c = lax.broadcasted_iota(I32, (tm, tm), 1)
    tri = (c <= r).astype(BF16)
    return pl.pallas_call(
        _route_kernel,
        grid=(t // tm,),
        in_specs=[pl.BlockSpec((tm, d), lambda i: (i, 0)), vec, vec, vec,
                  pl.BlockSpec((d, LANES), lambda i: (0, 0)), pl.BlockSpec((1, LANES), lambda i: (0, 0)),
                  pl.BlockSpec((tm, tm), lambda i: (0, 0))],
        out_specs=[pl.BlockSpec((tm, d), lambda i: (i, 0)), pl.BlockSpec((tm, LANES), lambda i: (i, 0)),
                   pl.BlockSpec((1, LANES), lambda i: (0, 0))],
        out_shape=[jax.ShapeDtypeStruct((t, d), F32), jax.ShapeDtypeStruct((t, LANES), F32),
                   jax.ShapeDtypeStruct((1, LANES), F32)],
        scratch_shapes=[pltpu.VMEM((1, LANES), F32)],
        compiler_params=_cparams(("arbitrary",)),
        name="route",
    )(x, g.reshape(1, d), sc.reshape(1, d), sh.reshape(1, d), w_all, b_all, tri)


def _rope_block(blk, cos, sin, half):
    if half == HEAD_DIM // 2:
        partner = pltpu.roll(blk, HEAD_DIM // 2, 1)
    else:
        lane = lax.broadcasted_iota(I32, blk.shape, 1)
        first = (lane % IDX_DIM) < (IDX_DIM // 2)
        partner = jnp.where(first, pltpu.roll(blk, LANES - IDX_DIM // 2, 1), pltpu.roll(blk, IDX_DIM // 2, 1))
    return blk * cos + partner * sin


def _proj_rope_kernel(a_ref, w_ref, cos_ref, sin_ref, o_ref, *, half):
    acc = jnp.dot(a_ref[...], w_ref[...], preferred_element_type=F32)
    cos = cos_ref[...]
    sin = sin_ref[...]
    for h in range(acc.shape[1] // LANES):
        sl = slice(h * LANES, (h + 1) * LANES)
        o_ref[:, sl] = _rope_block(acc[:, sl], cos, sin, half).astype(o_ref.dtype)


def _proj_rope(a, w, col_block0, n_out, cos, sin, half, out_dtype, tm=512, tn=512):
    t, d = a.shape
    return pl.pallas_call(
        functools.partial(_proj_rope_kernel, half=half),
        grid=(t // tm, n_out // tn),
        in_specs=[pl.BlockSpec((tm, d), lambda i, j: (i, 0)),
                  pl.BlockSpec((d, tn), lambda i, j: (0, col_block0 + j)),
                  pl.BlockSpec((tm, LANES), lambda i, j: (i, 0)),
                  pl.BlockSpec((tm, LANES), lambda i, j: (i, 0))],
        out_specs=pl.BlockSpec((tm, tn), lambda i, j: (i, j)),
        out_shape=jax.ShapeDtypeStruct((t, n_out), out_dtype),
        compiler_params=_cparams(("arbitrary", "arbitrary")),
        name="proj_rope",
    )(a, w, cos, sin)


def _proj_kv_kernel(a_ref, wk_ref, wv_ref, cos_ref, sin_ref, o_ref):
    a = a_ref[...]
    k = jnp.dot(a, wk_ref[...], preferred_element_type=F32)
    v = jnp.dot(a, wv_ref[...], preferred_element_type=F32)
    cos = cos_ref[...]
    sin = sin_ref[...]
    for h in range(k.shape[1] // LANES):
        sl = slice(h * LANES, (h + 1) * LANES)
        kr = _rope_block(k[:, sl], cos, sin, HEAD_DIM // 2)
        kb = lax.bitcast_convert_type(kr.astype(BF16).astype(F32), I32)
        vb = lax.bitcast_convert_type(v[:, sl].astype(BF16).astype(F32), I32)
        o_ref[:, sl] = kb | lax.shift_right_logical(vb, 16)


def _proj_kv(a, w, kblock, vblock, cos, sin, tm=512):
    t, d = a.shape
    n = N_KV_GROUPS * HEAD_DIM
    return pl.pallas_call(
        _proj_kv_kernel,
        grid=(t // tm,),
        in_specs=[pl.BlockSpec((tm, d), lambda i: (i, 0)),
                  pl.BlockSpec((d, n), lambda i: (0, kblock)),
                  pl.BlockSpec((d, n), lambda i: (0, vblock)),
                  pl.BlockSpec((tm, LANES), lambda i: (i, 0)),
                  pl.BlockSpec((tm, LANES), lambda i: (i, 0))],
        out_specs=pl.BlockSpec((tm, n), lambda i: (i, 0)),
        out_shape=jax.ShapeDtypeStruct((t, n), I32),
        compiler_params=_cparams(("arbitrary",)),
        name="proj_kv",
    )(a, w, w, cos, sin)


QB = 256
TK = 512
RC = 128


def _indexer_kernel(qit_ref, ki_ref, wit_ref, tri_ref, idx_ref, nv_ref, d_ref, hi_ref, *, n_keep):
    b = pl.program_id(0)
    q0 = b * QB
    n_t = (q0 + QB + TK - 1) // TK
    n_rows = n_t * TK
    rowi = lax.broadcasted_iota(I32, (TK, QB), 0)
    qchunk = (q0 + lax.broadcasted_iota(I32, (TK, QB), 1)) // CHUNK

    def admissible(k0):
        return ((k0 + rowi) // CHUNK) <= qchunk

    def score_tile(kt, carry):
        k0 = pl.multiple_of(kt * TK, TK)
        ki_t = ki_ref[pl.ds(k0, TK), :]
        acc = jnp.zeros((TK, QB), F32)
        for h in range(IDX_HEADS):
            a = jnp.dot(ki_t, qit_ref[h * IDX_DIM:(h + 1) * IDX_DIM, :], preferred_element_type=F32)
            acc = acc + jnp.maximum(a, 0.0) * wit_ref[h:h + 1, :]
        s = jnp.where(admissible(k0), acc + 0.0, NEG_INF)
        u = lax.bitcast_convert_type(s, I32)
        key = jnp.where(u < 0, u ^ 0x7FFFFFFF, u)
        d_ref[pl.ds(k0, TK), :] = key
        hi_ref[pl.ds(k0, TK), :] = lax.shift_right_arithmetic(key, 16).astype(I16)
        return carry

    def score_tile_pair(kp, carry):
        score_tile(2 * kp, carry)
        return score_tile(2 * kp + 1, carry)

    lax.fori_loop(0, n_t // 2, score_tile_pair, 0)

    @pl.when(n_t % 2 == 1)
    def _():
        score_tile(n_t - 1, 0)

    def count_where(pred):
        def body(kt, acc):
            k0 = pl.multiple_of(kt * TK, TK)
            hit = pred(d_ref[pl.ds(k0, TK), :]).astype(I32)
            return acc + jnp.sum(hit.reshape(TK // 8, 8, QB), axis=0)
        acc = lax.fori_loop(0, n_t, body, jnp.zeros((8, QB), I32))
        return jnp.sum(acc, axis=0, keepdims=True)

    def signed16(v_u):
        return lax.shift_right_arithmetic(jnp.left_shift(v_u ^ 0x8000, 16), 16).astype(I16)

    def add_packed_rows(acc, hit):
        for c in range(TK // 16):
            acc = acc + hit[c * 16:(c + 1) * 16, :]
        return acc

    def count_half_where(pred):
        def body(kt, acc):
            k0 = pl.multiple_of(kt * TK, TK)
            return add_packed_rows(acc, pred(hi_ref[pl.ds(k0, TK), :]).astype(I16))
        acc = lax.fori_loop(0, n_t, body, jnp.zeros((16, QB), I16))
        return jnp.sum(acc.astype(I32), axis=0, keepdims=True)

    def select_half(base):
        def select_bit(it, carry):
            v_u, cnt_v = carry
            cand_u = v_u | jnp.left_shift(jnp.int32(1), 15 - it)
            cand16 = signed16(cand_u)
            cnt = base + count_half_where(lambda half: half >= cand16)
            keep = cnt >= n_keep
            return jnp.where(keep, cand_u, v_u), jnp.where(keep, cnt, cnt_v)
        return select_bit

    zero_i = jnp.zeros((1, QB), I32)
    high_u, cnt_thr = lax.fori_loop(0, 16, select_half(zero_i), (zero_i, jnp.full((1, QB), n_keep + 1, I32)))
    high16 = signed16(high_u)

    def to_low_halves(kt, acc):
        k0 = pl.multiple_of(kt * TK, TK)
        high = hi_ref[pl.ds(k0, TK), :]
        low = signed16(d_ref[pl.ds(k0, TK), :] & 0xFFFF)
        hi_ref[pl.ds(k0, TK), :] = jnp.where(high == high16, low, jnp.int16(-(2 ** 15)))
        return add_packed_rows(acc, (high > high16).astype(I16))

    above = lax.fori_loop(0, n_t, to_low_halves, jnp.zeros((16, QB), I16))
    above = jnp.sum(above.astype(I32), axis=0, keepdims=True)
    low_u, cnt_thr = lax.fori_loop(0, 16, select_half(above), (zero_i, cnt_thr))
    thr = (jnp.left_shift(high_u, 16) | low_u) ^ INT_MIN

    tri = tri_ref[...]
    zero = jnp.zeros((1, QB), F32)

    def place(k0, sel, run_sel):
        pre_sel = jnp.dot(tri, sel.astype(F32).astype(BF16), preferred_element_type=F32)
        rank = (run_sel + pre_sel).astype(I32) - 1
        d_ref[pl.ds(k0, TK), :] = jnp.where(sel, (k0 + rowi) - rank, INVALID_SHIFT)
        return run_sel + pre_sel[TK - 1:TK, :]

    def rank_exact_count():
        def tile(kt, run_sel):
            k0 = pl.multiple_of(kt * TK, TK)
            return place(k0, admissible(k0) & (d_ref[pl.ds(k0, TK), :] >= thr), run_sel)
        return lax.fori_loop(0, n_t, tile, zero)

    def rank_with_ties():
        need = (n_keep - count_where(lambda key: key > thr)).astype(F32)

        def tile(kt, carry):
            run_eq, run_sel = carry
            k0 = pl.multiple_of(kt * TK, TK)
            key = d_ref[pl.ds(k0, TK), :]
            eq = key == thr
            eq_f = eq.astype(F32)
            pre_eq = jnp.dot(tri, eq_f.astype(BF16), preferred_element_type=F32)
            sel = admissible(k0) & ((key > thr) | (eq & ((run_eq + pre_eq - eq_f) < need)))
            return run_eq + pre_eq[TK - 1:TK, :], place(k0, sel, run_sel)
        return lax.fori_loop(0, n_t, tile, (zero, zero))[1]

    n_sel = lax.cond(jnp.max(jnp.abs(cnt_thr - n_keep)) == 0, rank_exact_count, rank_with_ties)

    p_ref = hi_ref
    packed_rows = 16
    d_ref[pl.ds(pl.multiple_of(n_rows, RC), RC), :] = jnp.full((RC, QB), INVALID_SHIFT, I32)
    p_ref[pl.ds(pl.multiple_of(n_rows, RC), RC), :] = jnp.full((RC, QB), INVALID_PACKED, I16)
    n_chunks = n_rows // RC
    win_chunks = -(-n_keep // RC)
    window = win_chunks * RC

    def first_stage(c, carry):
        r0 = pl.multiple_of(c * RC, RC)
        x = d_ref[pl.ds(r0, RC + 8), :]
        own, inc = x[0:RC], x[1:1 + RC]
        take = (inc & (1 | INVALID_SHIFT)) == 1
        stay = (own & (1 | INVALID_SHIFT)) == 0
        moved = jnp.where(take, inc, jnp.where(stay, own, INVALID_SHIFT))
        p_ref[pl.ds(r0, RC), :] = jnp.minimum(moved, INVALID_PACKED).astype(I16)
        return carry

    lax.fori_loop(0, n_chunks, first_stage, 0)
    n_stages = (d_ref.shape[0] - RC - 1).bit_length()
    for s in range(1, n_stages):
        sh = 1 << s
        test = sh | INVALID_PACKED

        def merge(own, inc, sh=sh, test=test):
            take = (inc & test) == sh
            stay = (own & test) == 0
            return jnp.where(take, inc, jnp.where(stay, own, INVALID_PACKED))

        def chunk_with_source(c, carry, sh=sh, merge=merge):
            r0 = pl.multiple_of(c * RC, RC)
            if sh < packed_rows:
                x = p_ref[pl.ds(r0, RC + packed_rows), :]
                own, inc = x[0:RC], x[sh:sh + RC]
            else:
                own = p_ref[pl.ds(r0, RC), :]
                inc = p_ref[pl.ds(pl.multiple_of(r0 + sh, packed_rows), RC), :]
            p_ref[pl.ds(r0, RC), :] = merge(own, inc)
            return carry

        def chunk_without_source(c, carry, merge=merge):
            r0 = pl.multiple_of(c * RC, RC)
            own = p_ref[pl.ds(r0, RC), :]
            p_ref[pl.ds(r0, RC), :] = merge(own, jnp.full_like(own, INVALID_PACKED))
            return carry

        def window_chunk(c, carry, sh=sh, merge=merge):
            r0 = pl.multiple_of((c // win_chunks) * sh + (c % win_chunks) * RC, RC)
            src = r0 + sh
            own = p_ref[pl.ds(r0, RC), :]
            inc = p_ref[pl.ds(pl.multiple_of(jnp.minimum(src, n_rows), RC), RC), :]
            p_ref[pl.ds(r0, RC), :] = merge(own, inc)
            return carry

        if sh < RC:
            lax.fori_loop(0, n_chunks, chunk_with_source, 0)
        elif sh < 2 * window:
            n_src = jnp.maximum(n_rows - sh, 0) // RC
            lax.fori_loop(0, n_src, chunk_with_source, 0)
            lax.fori_loop(n_src, n_chunks, chunk_without_source, 0)
        else:
            lax.fori_loop(0, ((n_rows + sh - 1) // sh) * win_chunks, window_chunk, 0)

    slot = lax.broadcasted_iota(I32, (n_keep, QB), 0)
    dd = p_ref[0:n_keep, :].astype(I32)
    idx_ref[...] = jnp.where(dd < INVALID_PACKED, (slot + dd) * KV_ROWS, 0)
    nv_ref[...] = n_sel.astype(I32)


def _indexer(qit, ki, wit, n_keep):
    t = ki.shape[0]
    assert t <= INVALID_PACKED and t % TK == 0
    r = lax.broadcasted_iota(I32, (TK, TK), 0)
    c = lax.broadcasted_iota(I32, (TK, TK), 1)
    tri = (c <= r).astype(BF16)
    return pl.pallas_call(
        functools.partial(_indexer_kernel, n_keep=n_keep),
        grid=(t // QB,),
        in_specs=[pl.BlockSpec((IDX_HEADS * IDX_DIM, QB), lambda b: (0, b)),
                  pl.BlockSpec((t, IDX_DIM), lambda b: (0, 0)),
                  pl.BlockSpec((IDX_HEADS, QB), lambda b: (0, b)),
                  pl.BlockSpec((TK, TK), lambda b: (0, 0))],
        out_specs=[pl.BlockSpec((n_keep, QB), lambda b: (0, b)), pl.BlockSpec((1, QB), lambda b: (0, b))],
        out_shape=[jax.ShapeDtypeStruct((n_keep, t), I32), jax.ShapeDtypeStruct((1, t), I32)],
        scratch_shapes=[pltpu.VMEM((t + RC, QB), I32), pltpu.VMEM((t + RC, QB), I16)],
        compiler_params=_cparams(("arbitrary",)),
        name="indexer",
    )(qit, ki, wit, tri)


GQ = 128
GATTN_UNROLL = 8


def _gattn_kernel(nv_ref, idx_ref, q_ref, kv_hbm, o_ref, kv_vmem, stage_a, stage_b, stage_c, stage_d,
                  s_a, s_b, row_lists, sem_kv, sem_rows, *, n_keep):
    i = pl.program_id(0)

    @pl.when(i == 0)
    def _():
        cp = pltpu.make_async_copy(kv_hbm, kv_vmem, sem_kv)
        cp.start()
        cp.wait()

    cp = pltpu.make_async_copy(idx_ref, row_lists, sem_rows)
    cp.start()
    cp.wait()

    n_col = n_keep * KV_ROWS
    colg = lax.broadcasted_iota(I32, (N_HEADS, n_col), 1)
    rowh = lax.broadcasted_iota(I32, (N_HEADS, n_col), 0)
    own_group = (colg % KV_ROWS) == (rowh // HEADS_PER_GROUP)
    col_slot = colg // KV_ROWS
    scale = HEAD_DIM ** -0.5

    last = GQ - 1
    stages = (stage_a, stage_b, stage_c, stage_d)
    s_bufs = (s_a, s_b)

    def gather(r, stage):
        rows = row_lists.at[pl.ds(jnp.minimum(r, last), 1)]
        for j in range(n_keep):
            row = pl.multiple_of(rows[0, j], KV_ROWS)
            stage[j * KV_ROWS:(j + 1) * KV_ROWS, :] = kv_vmem[pl.ds(row, KV_ROWS), :]

    def scores(r, stage, s_buf):
        k_all = lax.bitcast_convert_type(stage[...] & jnp.int32(-65536), F32).astype(BF16)
        qh = q_ref[pl.ds(pl.multiple_of(r * N_HEADS, N_HEADS), N_HEADS), :]
        s = lax.dot_general(qh, k_all, (((1,), (1,)), ((), ())), preferred_element_type=F32) * scale
        mask = own_group & (col_slot < nv_ref[i * GQ + r])
        s_buf[...] = jnp.where(mask, s, NEG_INF)

    def combine(r, stage, s_buf):
        s = s_buf[...]
        p = jnp.exp(s - jnp.max(s, axis=1, keepdims=True))
        l = jnp.sum(p, axis=1, keepdims=True)
        v_all = lax.bitcast_convert_type(jnp.left_shift(stage[...], 16), F32).astype(BF16)
        o = jnp.dot(p.astype(BF16), v_all, preferred_element_type=F32) / l
        o_ref[pl.ds(pl.multiple_of(r * N_HEADS, N_HEADS), N_HEADS), :] = o.astype(o_ref.dtype)

    gather(0, stages[0])
    gather(1, stages[1])
    scores(0, stages[0], s_bufs[0])

    def query_group(it, carry):
        r = GATTN_UNROLL * it
        for u in range(GATTN_UNROLL):
            gather(r + u + 2, stages[(u + 2) % 4])
            scores(jnp.minimum(r + u + 1, last), stages[(u + 1) % 4], s_bufs[(u + 1) % 2])
            combine(r + u, stages[u % 4], s_bufs[u % 2])
        return carry

    lax.fori_loop(0, GQ // GATTN_UNROLL, query_group, 0)


def _gattn(nv, idx, q2, kv2, n_keep):
    t = nv.shape[0]
    grid_spec = pltpu.PrefetchScalarGridSpec(
        num_scalar_prefetch=1,
        grid=(t // GQ,),
        in_specs=[pl.BlockSpec((GQ, n_keep), lambda i, nv: (i, 0)),
                  pl.BlockSpec((GQ * N_HEADS, HEAD_DIM), lambda i, nv: (i, 0)),
                  pl.BlockSpec(memory_space=pl.ANY)],
        out_specs=pl.BlockSpec((GQ * N_HEADS, HEAD_DIM), lambda i, nv: (i, 0)),
        scratch_shapes=[pltpu.VMEM(kv2.shape, I32),
                        pltpu.VMEM((n_keep * KV_ROWS, LANES), I32),
                        pltpu.VMEM((n_keep * KV_ROWS, LANES), I32),
                        pltpu.VMEM((n_keep * KV_ROWS, LANES), I32),
                        pltpu.VMEM((n_keep * KV_ROWS, LANES), I32),
                        pltpu.VMEM((N_HEADS, n_keep * KV_ROWS), F32),
                        pltpu.VMEM((N_HEADS, n_keep * KV_ROWS), F32),
                        pltpu.SMEM((GQ, n_keep), I32),
                        pltpu.SemaphoreType.DMA,
                        pltpu.SemaphoreType.DMA],
    )
    return pl.pallas_call(
        functools.partial(_gattn_kernel, n_keep=n_keep),
        grid_spec=grid_spec,
        out_shape=jax.ShapeDtypeStruct((t * N_HEADS, HEAD_DIM), BF16),
        compiler_params=_cparams(("arbitrary",)),
        name="gattn",
    )(nv, idx, q2, kv2)


def _mm_res_kernel(a_ref, w_ref, b_ref, x_ref, g_ref, o_ref):
    y = jnp.dot(a_ref[...], w_ref[...], preferred_element_type=F32) + b_ref[...]
    o_ref[...] = x_ref[...] + g_ref[...] * y


def _mm_res(a, w, bias, x, gate, tm=512, tn=512):
    t, k = a.shape
    n = w.shape[1]
    return pl.pallas_call(
        _mm_res_kernel,
        grid=(t // tm, n // tn),
        in_specs=[pl.BlockSpec((tm, k), lambda i, j: (i, 0)),
                  pl.BlockSpec((k, tn), lambda i, j: (0, j)),
                  pl.BlockSpec((1, tn), lambda i, j: (0, j)),
                  pl.BlockSpec((tm, tn), lambda i, j: (i, j)),
                  pl.BlockSpec((1, tn), lambda i, j: (0, j))],
        out_specs=pl.BlockSpec((tm, tn), lambda i, j: (i, j)),
        out_shape=jax.ShapeDtypeStruct((t, n), F32),
        compiler_params=_cparams(("arbitrary", "arbitrary")),
        name="mm_res",
    )(a, w, bias.reshape(1, n), x, gate.reshape(1, n))


def _glu_kernel(a_ref, wa_ref, wg_ref, ba_ref, bg_ref, o_ref):
    a = a_ref[...]
    ya = jnp.dot(a, wa_ref[...], preferred_element_type=F32) + ba_ref[...]
    yg = jnp.dot(a, wg_ref[...], preferred_element_type=F32) + bg_ref[...]
    o_ref[...] = ya * jax.nn.sigmoid(yg)


def _glu(a, w, b, tm=512, tn=512):
    t, k = a.shape
    n = w.shape[1] // 2
    nb = n // tn
    b2 = b.reshape(1, 2 * n)
    return pl.pallas_call(
        _glu_kernel,
        grid=(t // tm, nb),
        in_specs=[pl.BlockSpec((tm, k), lambda i, j: (i, 0)),
                  pl.BlockSpec((k, tn), lambda i, j: (0, j)),
                  pl.BlockSpec((k, tn), lambda i, j: (0, j + nb)),
                  pl.BlockSpec((1, tn), lambda i, j: (0, j)),
                  pl.BlockSpec((1, tn), lambda i, j: (0, j + nb))],
        out_specs=pl.BlockSpec((tm, tn), lambda i, j: (i, j)),
        out_shape=jax.ShapeDtypeStruct((t, n), F32),
        compiler_params=_cparams(("arbitrary", "arbitrary")),
        name="glu",
    )(a, w, w, b2, b2)


HALO = 32


SUBLANES = 8
CONV_ROWS = 32
CONV_COLS = 512


def _dwconv_ln_kernel(u_ref, prev_ref, w_ref, b_ref, g_ref, beta_ref, o_ref, ext_ref, sh_ref, conv_ref):
    i = pl.program_id(0)
    tm, d = u_ref.shape
    ext_ref[0:HALO, :] = jnp.where(i == 0, 0.0, prev_ref[...])
    ext_ref[HALO:HALO + tm, :] = u_ref[...]
    for phase in range(SUBLANES):
        sh_ref[phase, 0:tm + HALO - phase, :] = ext_ref[phase:tm + HALO, :]
    first = HALO - (CONV_WIDTH - 1)

    def conv_rows(c, carry):
        r0 = pl.multiple_of(c * CONV_ROWS, CONV_ROWS)
        for cb in range(d // CONV_COLS):
            cols = slice(cb * CONV_COLS, (cb + 1) * CONV_COLS)
            acc = jnp.zeros((CONV_ROWS, CONV_COLS), F32) + b_ref[:, cols]
            for k in range(CONV_WIDTH):
                phase = (first + k) % SUBLANES
                start = pl.multiple_of(r0 + (first + k - phase), SUBLANES)
                acc = acc + sh_ref[phase, pl.ds(start, CONV_ROWS), cols] * w_ref[k:k + 1, cols]
            conv_ref[pl.ds(r0, CONV_ROWS), cols] = acc
        return carry

    lax.fori_loop(0, tm // CONV_ROWS, conv_rows, 0)
    acc = conv_ref[...]
    mu = jnp.mean(acc, axis=-1, keepdims=True)
    cen = acc - mu
    var = jnp.mean(cen * cen, axis=-1, keepdims=True)
    y = (cen * lax.rsqrt(var + EPS)) * g_ref[...] + beta_ref[...]
    o_ref[...] = (y * jax.nn.sigmoid(y)).astype(o_ref.dtype)


def _dwconv_ln(u, w_dw, b_dw, ln_g, ln_b, tm=256):
    t, d = u.shape
    ratio = tm // HALO
    vec = pl.BlockSpec((1, d), lambda i: (0, 0))
    return pl.pallas_call(
        _dwconv_ln_kernel,
        grid=(t // tm,),
        in_specs=[pl.BlockSpec((tm, d), lambda i: (i, 0)),
                  pl.BlockSpec((HALO, d), lambda i: (jnp.maximum(i * ratio - 1, 0), 0)),
                  pl.BlockSpec((HALO, d), lambda i: (0, 0)),
                  vec, vec, vec],
        out_specs=pl.BlockSpec((tm, d), lambda i: (i, 0)),
        out_shape=jax.ShapeDtypeStruct((t, d), BF16),
        scratch_shapes=[pltpu.VMEM((HALO + tm, d), F32), pltpu.VMEM((SUBLANES, HALO + tm, d), F32),
                        pltpu.VMEM((tm, d), F32)],
        compiler_params=_cparams(("arbitrary",)),
        name="dwconv_ln",
    )(u, u, jnp.pad(w_dw, ((0, HALO - CONV_WIDTH), (0, 0))), b_dw.reshape(1, d), ln_g.reshape(1, d), ln_b.reshape(1, d))


def _row_copy(src, src_row, dst, dst_row, sem):
    return pltpu.make_async_copy(src.at[pl.ds(src_row, 1), :], dst.at[pl.ds(dst_row, 1), :], sem)


ROW_UNROLL = 8


def _for_rows(n, fn):
    def group(gi, carry):
        for u in range(ROW_UNROLL):
            fn(gi * ROW_UNROLL + u, carry)
        return carry

    lax.fori_loop(0, n // ROW_UNROLL, group, 0)


def _dispatch_kernel(dest_ref, h_ref, aux_ref, xs_in, ws_in, xs_hbm, ws_hbm, sem_x, sem_w):
    del xs_in, ws_in
    tm = h_ref.shape[0]
    t0 = pl.program_id(0) * tm

    def issue(r, carry):
        slot = dest_ref[t0 + r]
        _row_copy(h_ref, r, xs_hbm, slot, sem_x).start()
        _row_copy(aux_ref, r, ws_hbm, slot, sem_w).start()
        return carry

    def drain(r, carry):
        _row_copy(h_ref, r, xs_hbm, 0, sem_x).wait()
        _row_copy(aux_ref, r, ws_hbm, 0, sem_w).wait()
        return carry

    _for_rows(tm, issue)
    _for_rows(tm, drain)


def _dispatch(dest, h, aux, n_rows, tm=256):
    t, d = h.shape
    grid_spec = pltpu.PrefetchScalarGridSpec(
        num_scalar_prefetch=1,
        grid=(t // tm,),
        in_specs=[pl.BlockSpec((tm, d), lambda i, dest: (i, 0)),
                  pl.BlockSpec((tm, LANES), lambda i, dest: (i, 0)),
                  pl.BlockSpec(memory_space=pl.ANY),
                  pl.BlockSpec(memory_space=pl.ANY)],
        out_specs=[pl.BlockSpec(memory_space=pl.ANY), pl.BlockSpec(memory_space=pl.ANY)],
        scratch_shapes=[pltpu.SemaphoreType.DMA, pltpu.SemaphoreType.DMA],
    )
    return pl.pallas_call(
        _dispatch_kernel,
        grid_spec=grid_spec,
        out_shape=[jax.ShapeDtypeStruct((n_rows, d), F32), jax.ShapeDtypeStruct((n_rows, LANES), F32)],
        input_output_aliases={3: 0, 4: 1},
        compiler_params=_cparams(("arbitrary",)),
        name="dispatch",
    )(dest, h, aux, jnp.zeros((n_rows, d), F32), jnp.zeros((n_rows, LANES), F32))


def _ffn_kernel(elo_ref, ehi_ref, live_ref, xs_ref, ws_ref, wg_lo, wu_lo, wd_lo, wg_hi, wu_hi, wd_hi, ys_ref):
    b = pl.program_id(0)

    @pl.when(live_ref[b] == 1)
    def _():
        x = xs_ref[...].astype(BF16)
        ws = ws_ref[...]

        def expert(wg, wu, wd):
            gate = jnp.dot(x, wg[0], preferred_element_type=F32)
            up = jnp.dot(x, wu[0], preferred_element_type=F32)
            hid = (gate * jax.nn.sigmoid(gate)) * up
            return jnp.dot(hid.astype(BF16), wd[0], preferred_element_type=F32)

        ys_ref[...] = (expert(wg_lo, wu_lo, wd_lo) * ws[:, AUX_W_LO:AUX_W_LO + 1]
                       + expert(wg_hi, wu_hi, wd_hi) * ws[:, AUX_W_HI:AUX_W_HI + 1])

    @pl.when(live_ref[b] == 0)
    def _():
        ys_ref[...] = jnp.zeros_like(ys_ref)


def _ffn(e_lo, e_hi, live, xs, ws, w_gate, w_up, w_down):
    n_rows, d = xs.shape
    _, _, de = w_gate.shape
    lo_in = pl.BlockSpec((1, d, de), lambda b, lo, hi, live: (lo[b], 0, 0))
    hi_in = pl.BlockSpec((1, d, de), lambda b, lo, hi, live: (hi[b], 0, 0))
    lo_out = pl.BlockSpec((1, de, d), lambda b, lo, hi, live: (lo[b], 0, 0))
    hi_out = pl.BlockSpec((1, de, d), lambda b, lo, hi, live: (hi[b], 0, 0))
    grid_spec = pltpu.PrefetchScalarGridSpec(
        num_scalar_prefetch=3,
        grid=(n_rows // MOE_TM,),
        in_specs=[pl.BlockSpec((MOE_TM, d), lambda b, lo, hi, live: (b, 0)),
                  pl.BlockSpec((MOE_TM, LANES), lambda b, lo, hi, live: (b, 0)),
                  lo_in, lo_in, lo_out, hi_in, hi_in, hi_out],
        out_specs=pl.BlockSpec((MOE_TM, d), lambda b, lo, hi, live: (b, 0)),
    )
    return pl.pallas_call(
        _ffn_kernel,
        grid_spec=grid_spec,
        out_shape=jax.ShapeDtypeStruct((n_rows, d), F32),
        compiler_params=_cparams(("arbitrary",)),
        name="ffn",
    )(e_lo, e_hi, live, xs, ws, w_gate, w_up, w_down, w_gate, w_up, w_down)


def _combine_kernel(dest_ref, x_ref, g_ref, fg_ref, ys_hbm, o_ref, buf, sem, *, final_norm):
    tm = x_ref.shape[0]
    t0 = pl.program_id(0) * tm

    def issue(r, carry):
        _row_copy(ys_hbm, dest_ref[t0 + r], buf, r, sem).start()
        return carry

    def drain(r, carry):
        _row_copy(ys_hbm, 0, buf, r, sem).wait()
        return carry

    _for_rows(tm, issue)
    _for_rows(tm, drain)
    y = x_ref[...] + g_ref[...] * buf[...]
    if final_norm:
        y = (y * lax.rsqrt(jnp.mean(y * y, axis=-1, keepdims=True) + EPS)) * fg_ref[...]
    o_ref[...] = y


def _combine(dest, x, gate, final_g, ys, final_norm, tm=256):
    t, d = x.shape
    vec = pl.BlockSpec((1, d), lambda i, dest: (0, 0))
    grid_spec = pltpu.PrefetchScalarGridSpec(
        num_scalar_prefetch=1,
        grid=(t // tm,),
        in_specs=[pl.BlockSpec((tm, d), lambda i, dest: (i, 0)), vec, vec, pl.BlockSpec(memory_space=pl.ANY)],
        out_specs=pl.BlockSpec((tm, d), lambda i, dest: (i, 0)),
        scratch_shapes=[pltpu.VMEM((tm, d), F32), pltpu.SemaphoreType.DMA],
    )
    return pl.pallas_call(
        functools.partial(_combine_kernel, final_norm=final_norm),
        grid_spec=grid_spec,
        out_shape=jax.ShapeDtypeStruct((t, d), F32),
        compiler_params=_cparams(("arbitrary",)),
        name="combine",
    )(dest, x, gate.reshape(1, d), final_g.reshape(1, d), ys)


def _rope_tables(t):
    pos = jnp.arange(t, dtype=I32).astype(F32)[:, None]
    lane = jnp.arange(LANES)

    def table(dim):
        half = dim // 2
        inv = ROPE_THETA ** (-jnp.arange(half, dtype=F32) / half)
        ang = pos * inv[None, :]
        j = lane % dim
        cos = jnp.cos(ang)[:, j % half]
        sin = jnp.sin(ang)[:, j % half] * jnp.where(j < half, -1.0, 1.0)[None, :]
        return cos, sin

    return table(HEAD_DIM), table(IDX_DIM)


def _attention_layer(x, mod, norm_g, w_in, w_out, tables):
    t, d = x.shape
    sh1, sc1, g1 = mod[0:d], mod[d:2 * d], mod[2 * d:3 * d]
    (cos_h, sin_h), (cos_i, sin_i) = tables
    hn = _normmod(x, norm_g, sc1, sh1)
    w_bf = w_in.astype(BF16)
    nq = N_HEADS * HEAD_DIM
    nkv = N_KV_GROUPS * HEAD_DIM
    nqi = IDX_HEADS * IDX_DIM
    tn = 512
    q = _proj_rope(hn, w_bf, 0, nq, cos_h, sin_h, HEAD_DIM // 2, BF16)
    kv = _proj_kv(hn, w_bf, nq // nkv, nq // nkv + 1, cos_h, sin_h)
    qi = _proj_rope(hn, w_bf, (nq + 2 * nkv) // tn, nqi, cos_i, sin_i, IDX_DIM // 2, BF16)
    tail0 = nq + 2 * nkv + nqi
    w_tail = jnp.pad(w_bf[:, tail0:], ((0, 0), (0, LANES - (IDX_DIM + IDX_HEADS))))
    lane = jnp.arange(LANES)
    wi_scale = (IDX_HEADS ** -0.5) * (IDX_DIM ** -0.5)
    cos_t = jnp.where(lane[None, :] < IDX_DIM, cos_i, wi_scale)
    sin_t = jnp.where(lane[None, :] < IDX_DIM, sin_i, 0.0)
    tail = _proj_rope(hn, w_tail, 0, LANES, cos_t, sin_t, IDX_DIM // 2, F32, tn=LANES)
    ki = tail[:, :IDX_DIM].astype(BF16)
    wit = tail[:, IDX_DIM:IDX_DIM + IDX_HEADS].T
    n_keep = min(TOPK_KEYS, t // 4)
    idx_t, nv = _indexer(qi.T, ki, wit, n_keep)
    att = _gattn(nv.reshape(t), idx_t.T, q.reshape(t * N_HEADS, HEAD_DIM), kv.reshape(t * KV_ROWS, LANES), n_keep)
    att = att.reshape(t, nq)
    return _mm_res(att, w_out.astype(BF16), jnp.zeros((d,), F32), x, g1)


def _conv_layer(x, mod, norm_g, w_pw1, b_pw1, w_dw, b_dw, ln_g, ln_b, w_pw2, b_pw2):
    t, d = x.shape
    sh1, sc1, g1 = mod[0:d], mod[d:2 * d], mod[2 * d:3 * d]
    hn = _normmod(x, norm_g, sc1, sh1)
    u = _glu(hn, w_pw1.astype(BF16), b_pw1)
    v = _dwconv_ln(u, w_dw, b_dw, ln_g, ln_b)
    return _mm_res(v, w_pw2.astype(BF16), b_pw2, x, g1)


def _moe_layer(x, mod, norm_g, w_rg, b_rg, w_re, b_re, w_gate, w_up, w_down, final_g, final_norm):
    t, d = x.shape
    sh2, sc2, g2 = mod[3 * d:4 * d], mod[4 * d:5 * d], mod[5 * d:6 * d]
    h, aux, counts = _route(x, norm_g, sc2, sh2, w_rg, b_rg, w_re, b_re)
    bucket = aux[:, AUX_BUCKET].astype(I32)
    rank = aux[:, AUX_RANK].astype(I32)
    sizes = counts[0, :N_BUCKETS].astype(I32)
    padded = (sizes + MOE_TM - 1) // MOE_TM * MOE_TM
    ends = jnp.cumsum(padded)
    dest = (ends - padded)[bucket] + rank
    n_blocks = t // MOE_TM + N_BUCKETS
    blk_start = jnp.arange(n_blocks, dtype=I32) * MOE_TM
    blk_bucket = jnp.minimum(jnp.searchsorted(ends, blk_start, side='right'), N_BUCKETS - 1).astype(I32)
    live = (blk_start < ends[-1]).astype(I32)
    group0 = (blk_bucket // N_PAIRS) * EXPERTS_PER_GROUP
    e_lo = group0 + jnp.asarray(PAIR_LO, I32)[blk_bucket % N_PAIRS]
    e_hi = group0 + jnp.asarray(PAIR_HI, I32)[blk_bucket % N_PAIRS]
    xs, ws = _dispatch(dest, h, aux, n_blocks * MOE_TM)
    ys = _ffn(e_lo, e_hi, live, xs, ws, w_gate.astype(BF16), w_up.astype(BF16), w_down.astype(BF16))
    return _combine(dest, x, g2, final_g, ys, final_norm)


def kernel(x, c, ada_w, ada_b, norm1_g, norm2_g, attn_w_in, attn_w_out, conv_w_pw1, conv_b_pw1, conv_w_dw, conv_b_dw, conv_ln_g, conv_ln_b, conv_w_pw2, conv_b_pw2, router_w_group, router_b_group, router_w_expert, router_b_expert, exp_w_gate, exp_w_up, exp_w_down, final_g):
    bsz, t, d = x.shape
    assert bsz == 1 and d == N_HEADS * HEAD_DIM
    depth = ada_w.shape[0]
    mods = _ada_mod(c, ada_w, ada_b)
    tables = _rope_tables(t)
    xf = x.reshape(t, d)
    for i in range(depth):
        j = i // 2
        if i % 2 == 0:
            xf = _attention_layer(xf, mods[i], norm1_g[i], attn_w_in[j], attn_w_out[j], tables)
        else:
            xf = _conv_layer(xf, mods[i], norm1_g[i], conv_w_pw1[j], conv_b_pw1[j], conv_w_dw[j], conv_b_dw[j],
                             conv_ln_g[j], conv_ln_b[j], conv_w_pw2[j], conv_b_pw2[j])
        xf = _moe_layer(xf, mods[i], norm2_g[i], router_w_group[i], router_b_group[i], router_w_expert[i],
                        router_b_expert[i], exp_w_gate[i], exp_w_up[i], exp_w_down[i], final_g, i == depth - 1)
    return xf.reshape(bsz, t, d)
```

```python
import functools

import jax
import jax.numpy as jnp
from jax import lax
from jax.experimental import pallas as pl
from jax.experimental.pallas import tpu as pltpu

EPS = 1e-6
CHUNK = 64
N_HEADS = 16
N_KV_GROUPS = 4
HEADS_PER_GROUP = N_HEADS // N_KV_GROUPS
HEAD_DIM = 128
IDX_HEADS = 16
IDX_DIM = 64
TOPK_KEYS = 256
ROPE_THETA = 10000.0
NEG_INF = -1e30
CONV_WIDTH = 31
N_GROUPS = 4
EXPERTS_PER_GROUP = 4
N_EXPERTS = N_GROUPS * EXPERTS_PER_GROUP

LANES = 128
KV_ROWS = N_KV_GROUPS
INT_MIN = -(2 ** 31)
INVALID_SHIFT = 1 << 30
INVALID_PACKED = 1 << 14
VMEM_LIMIT = 56 * 1024 * 1024

F32 = jnp.float32
BF16 = jnp.bfloat16
I32 = jnp.int32
I16 = jnp.int16


def _cparams(sem):
    return pltpu.CompilerParams(dimension_semantics=sem, vmem_limit_bytes=VMEM_LIMIT)


def _ada_kernel(c_ref, w_ref, b_ref, o_ref):
    c = c_ref[...]
    ca = c * jax.nn.sigmoid(c)
    o_ref[0] = jnp.dot(ca, w_ref[0], preferred_element_type=F32, precision=lax.Precision.HIGHEST) + b_ref[0]


def _ada_mod(c, ada_w, ada_b):
    depth, d, n = ada_w.shape
    tn = 768
    c8 = jnp.zeros((8, d), F32).at[0].set(c[0])
    out = pl.pallas_call(
        _ada_kernel,
        grid=(depth, n // tn),
        in_specs=[pl.BlockSpec((8, d), lambda i, j: (0, 0)),
                  pl.BlockSpec((1, d, tn), lambda i, j: (i, 0, j)),
                  pl.BlockSpec((1, 1, tn), lambda i, j: (i, 0, j))],
        out_specs=pl.BlockSpec((1, 8, tn), lambda i, j: (i, 0, j)),
        out_shape=jax.ShapeDtypeStruct((depth, 8, n), F32),
        compiler_params=_cparams(("arbitrary", "arbitrary")),
        name="ada_mod",
    )(c8, ada_w, ada_b.reshape(depth, 1, n))
    return out[:, 0, :]


def _normmod_kernel(x_ref, g_ref, sc_ref, sh_ref, o_ref):
    x = x_ref[...]
    xn = x * lax.rsqrt(jnp.mean(x * x, axis=-1, keepdims=True) + EPS)
    o_ref[...] = ((xn * g_ref[...]) * (1.0 + sc_ref[...]) + sh_ref[...]).astype(o_ref.dtype)


def _normmod(x, g, sc, sh, tm=512):
    t, d = x.shape
    vec = pl.BlockSpec((1, d), lambda i: (0, 0))
    return pl.pallas_call(
        _normmod_kernel,
        grid=(t // tm,),
        in_specs=[pl.BlockSpec((tm, d), lambda i: (i, 0)), vec, vec, vec],
        out_specs=pl.BlockSpec((tm, d), lambda i: (i, 0)),
        out_shape=jax.ShapeDtypeStruct((t, d), BF16),
        compiler_params=_cparams(("arbitrary",)),
        name="normmod",
    )(x, g.reshape(1, d), sc.reshape(1, d), sh.reshape(1, d))


N_PAIRS = EXPERTS_PER_GROUP * (EXPERTS_PER_GROUP - 1) // 2
N_BUCKETS = N_GROUPS * N_PAIRS
PAIR_LO = (0, 0, 0, 1, 1, 2)
PAIR_HI = (1, 2, 3, 2, 3, 3)
MOE_TM = 256
AUX_W_LO, AUX_W_HI, AUX_BUCKET, AUX_RANK = 0, 1, 2, 3


def _route_kernel(x_ref, g_ref, sc_ref, sh_ref, wr_ref, br_ref, tri_ref, h_ref, aux_ref, cnt_ref, run_ref):
    @pl.when(pl.program_id(0) == 0)
    def _():
        run_ref[...] = jnp.zeros_like(run_ref)

    x = x_ref[...]
    xn = x * lax.rsqrt(jnp.mean(x * x, axis=-1, keepdims=True) + EPS)
    h = (xn * g_ref[...]) * (1.0 + sc_ref[...]) + sh_ref[...]
    h_ref[...] = h.astype(h_ref.dtype)
    h_hi = h.astype(BF16)
    h_lo = (h - h_hi.astype(F32)).astype(BF16)
    w = wr_ref[...]
    w_hi = w.astype(BF16)
    w_lo = (w - w_hi.astype(F32)).astype(BF16)
    logits = (jnp.dot(h_hi, w_hi, preferred_element_type=F32) + jnp.dot(h_lo, w_hi, preferred_element_type=F32)
              + jnp.dot(h_hi, w_lo, preferred_element_type=F32)) + br_ref[...]
    tm = x.shape[0]
    lane4 = lax.broadcasted_iota(I32, (tm, N_GROUPS), 1)

    def first_argmax(v):
        m = jnp.max(v, axis=-1, keepdims=True)
        return m, jnp.min(jnp.where(v == m, lane4, N_GROUPS), axis=-1, keepdims=True)

    gl = logits[:, 0:N_GROUPS]
    gmax, gidx = first_argmax(gl)
    g_w = 1.0 / jnp.sum(jnp.exp(gl - gmax), axis=-1, keepdims=True)
    el = jnp.zeros((tm, EXPERTS_PER_GROUP), F32)
    for g in range(N_GROUPS):
        lo = N_GROUPS + g * EXPERTS_PER_GROUP
        el = jnp.where(gidx == g, logits[:, lo:lo + EXPERTS_PER_GROUP], el)
    emax = jnp.max(el, axis=-1, keepdims=True)
    ee = jnp.exp(el - emax)
    ep = ee / jnp.sum(ee, axis=-1, keepdims=True)
    p1, i1 = first_argmax(ep)
    rest = jnp.where(lane4 == i1, -1.0, ep)
    p2, i2 = first_argmax(rest)
    den = p1 + p2
    w1 = g_w * p1 / den
    w2 = g_w * p2 / den
    first_lo = i1 < i2
    a = jnp.minimum(i1, i2)
    b = jnp.maximum(i1, i2)
    w_lo = jnp.where(first_lo, w1, w2)
    w_hi = jnp.where(first_lo, w2, w1)
    pair = jnp.where(a == 0, b - 1, jnp.where(a == 1, b + 1, N_PAIRS - 1))
    bucket = gidx * N_PAIRS + pair
    lane = lax.broadcasted_iota(I32, (tm, LANES), 1)
    onehot = lane == bucket
    prefix = jnp.dot(tri_ref[...], onehot.astype(F32).astype(BF16), preferred_element_type=F32)
    run = run_ref[...]
    rank = jnp.sum(jnp.where(onehot, prefix + run, 0.0), axis=1, keepdims=True) - 1.0
    run = run + prefix[tm - 1:tm, :]
    run_ref[...] = run
    cnt_ref[...] = run
    aux_ref[...] = (jnp.where(lane == AUX_W_LO, w_lo, 0.0) + jnp.where(lane == AUX_W_HI, w_hi, 0.0)
                    + jnp.where(lane == AUX_BUCKET, bucket.astype(F32), 0.0) + jnp.where(lane == AUX_RANK, rank, 0.0))


def _route(x, g, sc, sh, w_rg, b_rg, w_re, b_re, tm=512):
    t, d = x.shape
    w_all = jnp.concatenate([w_rg, jnp.moveaxis(w_re, 0, 1).reshape(d, N_EXPERTS)], axis=1)
    b_all = jnp.concatenate([b_rg, b_re.reshape(N_EXPERTS)])
    npad = LANES - w_all.shape[1]
    w_all = jnp.pad(w_all, ((0, 0), (0, npad)))
    b_all = jnp.pad(b_all, (0, npad)).reshape(1, LANES)
    vec = pl.BlockSpec((1, d), lambda i: (0, 0))
    r = lax.broadcasted_iota(I32, (tm, tm), 0)
    c = lax.broadcasted_iota(I32, (tm, tm), 1)
    tri = (c <= r).astype(BF16)
    return pl.pallas_call(
        _route_kernel,
        grid=(t // tm,),
        in_specs=[pl.BlockSpec((tm, d), lambda i: (i, 0)), vec, vec, vec,
                  pl.BlockSpec((d, LANES), lambda i: (0, 0)), pl.BlockSpec((1, LANES), lambda i: (0, 0)),
                  pl.BlockSpec((tm, tm), lambda i: (0, 0))],
        out_specs=[pl.BlockSpec((tm, d), lambda i: (i, 0)), pl.BlockSpec((tm, LANES), lambda i: (i, 0)),
                   pl.BlockSpec((1, LANES), lambda i: (0, 0))],
        out_shape=[jax.ShapeDtypeStruct((t, d), F32), jax.ShapeDtypeStruct((t, LANES), F32),
                   jax.ShapeDtypeStruct((1, LANES), F32)],
        scratch_shapes=[pltpu.VMEM((1, LANES), F32)],
        compiler_params=_cparams(("arbitrary",)),
        name="route",
    )(x, g.reshape(1, d), sc.reshape(1, d), sh.reshape(1, d), w_all, b_all, tri)


def _rope_block(blk, cos, sin, half):
    if half == HEAD_DIM // 2:
        partner = pltpu.roll(blk, HEAD_DIM // 2, 1)
    else:
        lane = lax.broadcasted_iota(I32, blk.shape, 1)
        first = (lane % IDX_DIM) < (IDX_DIM // 2)
        partner = jnp.where(first, pltpu.roll(blk, LANES - IDX_DIM // 2, 1), pltpu.roll(blk, IDX_DIM // 2, 1))
    return blk * cos + partner * sin


def _proj_rope_kernel(a_ref, w_ref, cos_ref, sin_ref, o_ref, *, half):
    acc = jnp.dot(a_ref[...], w_ref[...], preferred_element_type=F32)
    cos = cos_ref[...]
    sin = sin_ref[...]
    for h in range(acc.shape[1] // LANES):
        sl = slice(h * LANES, (h + 1) * LANES)
        o_ref[:, sl] = _rope_block(acc[:, sl], cos, sin, half).astype(o_ref.dtype)


def _proj_rope(a, w, col_block0, n_out, cos, sin, half, out_dtype, tm=512, tn=512):
    t, d = a.shape
    return pl.pallas_call(
        functools.partial(_proj_rope_kernel, half=half),
        grid=(t // tm, n_out // tn),
        in_specs=[pl.BlockSpec((tm, d), lambda i, j: (i, 0)),
                  pl.BlockSpec((d, tn), lambda i, j: (0, col_block0 + j)),
                  pl.BlockSpec((tm, LANES), lambda i, j: (i, 0)),
                  pl.BlockSpec((tm, LANES), lambda i, j: (i, 0))],
        out_specs=pl.BlockSpec((tm, tn), lambda i, j: (i, j)),
        out_shape=jax.ShapeDtypeStruct((t, n_out), out_dtype),
        compiler_params=_cparams(("arbitrary", "arbitrary")),
        name="proj_rope",
    )(a, w, cos, sin)


def _proj_kv_kernel(a_ref, wk_ref, wv_ref, cos_ref, sin_ref, o_ref):
    a = a_ref[...]
    k = jnp.dot(a, wk_ref[...], preferred_element_type=F32)
    v = jnp.dot(a, wv_ref[...], preferred_element_type=F32)
    cos = cos_ref[...]
    sin = sin_ref[...]
    for h in range(k.shape[1] // LANES):
        sl = slice(h * LANES, (h + 1) * LANES)
        kr = _rope_block(k[:, sl], cos, sin, HEAD_DIM // 2)
        kb = lax.bitcast_convert_type(kr.astype(BF16).astype(F32), I32)
        vb = lax.bitcast_convert_type(v[:, sl].astype(BF16).astype(F32), I32)
        o_ref[:, sl] = kb | lax.shift_right_logical(vb, 16)


def _proj_kv(a, w, kblock, vblock, cos, sin, tm=512):
    t, d = a.shape
    n = N_KV_GROUPS * HEAD_DIM
    return pl.pallas_call(
        _proj_kv_kernel,
        grid=(t // tm,),
        in_specs=[pl.BlockSpec((tm, d), lambda i: (i, 0)),
                  pl.BlockSpec((d, n), lambda i: (0, kblock)),
                  pl.BlockSpec((d, n), lambda i: (0, vblock)),
                  pl.BlockSpec((tm, LANES), lambda i: (i, 0)),
                  pl.BlockSpec((tm, LANES), lambda i: (i, 0))],
        out_specs=pl.BlockSpec((tm, n), lambda i: (i, 0)),
        out_shape=jax.ShapeDtypeStruct((t, n), I32),
        compiler_params=_cparams(("arbitrary",)),
        name="proj_kv",
    )(a, w, w, cos, sin)


QB = 256
TK = 512
RC = 128


def _indexer_kernel(qit_ref, ki_ref, wit_ref, tri_ref, idx_ref, nv_ref, d_ref, hi_ref, *, n_keep):
    b = pl.program_id(0)
    q0 = b * QB
    n_t = (q0 + QB + TK - 1) // TK
    n_rows = n_t * TK
    rowi = lax.broadcasted_iota(I32, (TK, QB), 0)
    qchunk = (q0 + lax.broadcasted_iota(I32, (TK, QB), 1)) // CHUNK

    def admissible(k0):
        return ((k0 + rowi) // CHUNK) <= qchunk

    def score_tile(kt, carry):
        k0 = pl.multiple_of(kt * TK, TK)
        ki_t = ki_ref[pl.ds(k0, TK), :]
        acc = jnp.zeros((TK, QB), F32)
        for h in range(IDX_HEADS):
            a = jnp.dot(ki_t, qit_ref[h * IDX_DIM:(h + 1) * IDX_DIM, :], preferred_element_type=F32)
            acc = acc + jnp.maximum(a, 0.0) * wit_ref[h:h + 1, :]
        s = jnp.where(admissible(k0), acc + 0.0, NEG_INF)
        u = lax.bitcast_convert_type(s, I32)
        key = jnp.where(u < 0, u ^ 0x7FFFFFFF, u)
        d_ref[pl.ds(k0, TK), :] = key
        hi_ref[pl.ds(k0, TK), :] = lax.shift_right_arithmetic(key, 16).astype(I16)
        return carry

    def score_tile_pair(kp, carry):
        score_tile(2 * kp, carry)
        return score_tile(2 * kp + 1, carry)

    lax.fori_loop(0, n_t // 2, score_tile_pair, 0)

    @pl.when(n_t % 2 == 1)
    def _():
        score_tile(n_t - 1, 0)

    def count_where(pred):
        def body(kt, acc):
            k0 = pl.multiple_of(kt * TK, TK)
            hit = pred(d_ref[pl.ds(k0, TK), :]).astype(I32)
            return acc + jnp.sum(hit.reshape(TK // 8, 8, QB), axis=0)
        acc = lax.fori_loop(0, n_t, body, jnp.zeros((8, QB), I32))
        return jnp.sum(acc, axis=0, keepdims=True)

    def signed16(v_u):
        return lax.shift_right_arithmetic(jnp.left_shift(v_u ^ 0x8000, 16), 16).astype(I16)

    def add_packed_rows(acc, hit):
        for c in range(TK // 16):
            acc = acc + hit[c * 16:(c + 1) * 16, :]
        return acc

    def count_half_where(pred):
        def body(kt, acc):
            k0 = pl.multiple_of(kt * TK, TK)
            return add_packed_rows(acc, pred(hi_ref[pl.ds(k0, TK), :]).astype(I16))
        acc = lax.fori_loop(0, n_t, body, jnp.zeros((16, QB), I16))
        return jnp.sum(acc.astype(I32), axis=0, keepdims=True)

    def select_half(base):
        def select_bit(it, carry):
            v_u, cnt_v = carry
            cand_u = v_u | jnp.left_shift(jnp.int32(1), 15 - it)
            cand16 = signed16(cand_u)
            cnt = base + count_half_where(lambda half: half >= cand16)
            keep = cnt >= n_keep
            return jnp.where(keep, cand_u, v_u), jnp.where(keep, cnt, cnt_v)
        return select_bit

    zero_i = jnp.zeros((1, QB), I32)
    high_u, cnt_thr = lax.fori_loop(0, 16, select_half(zero_i), (zero_i, jnp.full((1, QB), n_keep + 1, I32)))
    high16 = signed16(high_u)

    def to_low_halves(kt, acc):
        k0 = pl.multiple_of(kt * TK, TK)
        high = hi_ref[pl.ds(k0, TK), :]
        low = signed16(d_ref[pl.ds(k0, TK), :] & 0xFFFF)
        hi_ref[pl.ds(k0, TK), :] = jnp.where(high == high16, low, jnp.int16(-(2 ** 15)))
        return add_packed_rows(acc, (high > high16).astype(I16))

    above = lax.fori_loop(0, n_t, to_low_halves, jnp.zeros((16, QB), I16))
    above = jnp.sum(above.astype(I32), axis=0, keepdims=True)
    low_u, cnt_thr = lax.fori_loop(0, 16, select_half(above), (zero_i, cnt_thr))
    thr = (jnp.left_shift(high_u, 16) | low_u) ^ INT_MIN

    tri = tri_ref[...]
    zero = jnp.zeros((1, QB), F32)

    def place(k0, sel, run_sel):
        pre_sel = jnp.dot(tri, sel.astype(F32).astype(BF16), preferred_element_type=F32)
        rank = (run_sel + pre_sel).astype(I32) - 1
        d_ref[pl.ds(k0, TK), :] = jnp.where(sel, (k0 + rowi) - rank, INVALID_SHIFT)
        return run_sel + pre_sel[TK - 1:TK, :]

    def rank_exact_count():
        def tile(kt, run_sel):
            k0 = pl.multiple_of(kt * TK, TK)
            return place(k0, admissible(k0) & (d_ref[pl.ds(k0, TK), :] >= thr), run_sel)
        return lax.fori_loop(0, n_t, tile, zero)

    def rank_with_ties():
        need = (n_keep - count_where(lambda key: key > thr)).astype(F32)

        def tile(kt, carry):
            run_eq, run_sel = carry
            k0 = pl.multiple_of(kt * TK, TK)
            key = d_ref[pl.ds(k0, TK), :]
            eq = key == thr
            eq_f = eq.astype(F32)
            pre_eq = jnp.dot(tri, eq_f.astype(BF16), preferred_element_type=F32)
            sel = admissible(k0) & ((key > thr) | (eq & ((run_eq + pre_eq - eq_f) < need)))
            return run_eq + pre_eq[TK - 1:TK, :], place(k0, sel, run_sel)
        return lax.fori_loop(0, n_t, tile, (zero, zero))[1]

    n_sel = lax.cond(jnp.max(jnp.abs(cnt_thr - n_keep)) == 0, rank_exact_count, rank_with_ties)

    p_ref = hi_ref
    packed_rows = 16
    d_ref[pl.ds(pl.multiple_of(n_rows, RC), RC), :] = jnp.full((RC, QB), INVALID_SHIFT, I32)
    p_ref[pl.ds(pl.multiple_of(n_rows, RC), RC), :] = jnp.full((RC, QB), INVALID_PACKED, I16)
    n_chunks = n_rows // RC
    win_chunks = -(-n_keep // RC)
    window = win_chunks * RC

    def first_stage(c, carry):
        r0 = pl.multiple_of(c * RC, RC)
        x = d_ref[pl.ds(r0, RC + 8), :]
        own, inc = x[0:RC], x[1:1 + RC]
        take = (inc & (1 | INVALID_SHIFT)) == 1
        stay = (own & (1 | INVALID_SHIFT)) == 0
        moved = jnp.where(take, inc, jnp.where(stay, own, INVALID_SHIFT))
        p_ref[pl.ds(r0, RC), :] = jnp.minimum(moved, INVALID_PACKED).astype(I16)
        return carry

    lax.fori_loop(0, n_chunks, first_stage, 0)
    n_stages = (d_ref.shape[0] - RC - 1).bit_length()
    for s in range(1, n_stages):
        sh = 1 << s
        test = sh | INVALID_PACKED

        def merge(own, inc, sh=sh, test=test):
            take = (inc & test) == sh
            stay = (own & test) == 0
            return jnp.where(take, inc, jnp.where(stay, own, INVALID_PACKED))

        def chunk_with_source(c, carry, sh=sh, merge=merge):
            r0 = pl.multiple_of(c * RC, RC)
            if sh < packed_rows:
                x = p_ref[pl.ds(r0, RC + packed_rows), :]
                own, inc = x[0:RC], x[sh:sh + RC]
            else:
                own = p_ref[pl.ds(r0, RC), :]
                inc = p_ref[pl.ds(pl.multiple_of(r0 + sh, packed_rows), RC), :]
            p_ref[pl.ds(r0, RC), :] = merge(own, inc)
            return carry

        def chunk_without_source(c, carry, merge=merge):
            r0 = pl.multiple_of(c * RC, RC)
            own = p_ref[pl.ds(r0, RC), :]
            p_ref[pl.ds(r0, RC), :] = merge(own, jnp.full_like(own, INVALID_PACKED))
            return carry

        def window_chunk(c, carry, sh=sh, merge=merge):
            r0 = pl.multiple_of((c // win_chunks) * sh + (c % win_chunks) * RC, RC)
            src = r0 + sh
            own = p_ref[pl.ds(r0, RC), :]
            inc = p_ref[pl.ds(pl.multiple_of(jnp.minimum(src, n_rows), RC), RC), :]
            p_ref[pl.ds(r0, RC), :] = merge(own, inc)
            return carry

        if sh < RC:
            lax.fori_loop(0, n_chunks, chunk_with_source, 0)
        elif sh < 2 * window:
            n_src = jnp.maximum(n_rows - sh, 0) // RC
            lax.fori_loop(0, n_src, chunk_with_source, 0)
            lax.fori_loop(n_src, n_chunks, chunk_without_source, 0)
        else:
            lax.fori_loop(0, ((n_rows + sh - 1) // sh) * win_chunks, window_chunk, 0)

    slot = lax.broadcasted_iota(I32, (n_keep, QB), 0)
    dd = p_ref[0:n_keep, :].astype(I32)
    idx_ref[...] = jnp.where(dd < INVALID_PACKED, (slot + dd) * KV_ROWS, 0)
    nv_ref[...] = n_sel.astype(I32)


def _indexer(qit, ki, wit, n_keep):
    t = ki.shape[0]
    assert t <= INVALID_PACKED and t % TK == 0
    r = lax.broadcasted_iota(I32, (TK, TK), 0)
    c = lax.broadcasted_iota(I32, (TK, TK), 1)
    tri = (c <= r).astype(BF16)
    return pl.pallas_call(
        functools.partial(_indexer_kernel, n_keep=n_keep),
        grid=(t // QB,),
        in_specs=[pl.BlockSpec((IDX_HEADS * IDX_DIM, QB), lambda b: (0, b)),
                  pl.BlockSpec((t, IDX_DIM), lambda b: (0, 0)),
                  pl.BlockSpec((IDX_HEADS, QB), lambda b: (0, b)),
                  pl.BlockSpec((TK, TK), lambda b: (0, 0))],
        out_specs=[pl.BlockSpec((n_keep, QB), lambda b: (0, b)), pl.BlockSpec((1, QB), lambda b: (0, b))],
        out_shape=[jax.ShapeDtypeStruct((n_keep, t), I32), jax.ShapeDtypeStruct((1, t), I32)],
        scratch_shapes=[pltpu.VMEM((t + RC, QB), I32), pltpu.VMEM((t + RC, QB), I16)],
        compiler_params=_cparams(("arbitrary",)),
        name="indexer",
    )(qit, ki, wit, tri)


GQ = 128
GATTN_UNROLL = 8


def _gattn_kernel(nv_ref, idx_ref, idx_next_ref, q_ref, kv_hbm, o_ref, kv_vmem, stage_a, stage_b, stage_c, stage_d,
                  s_a, s_b, row_lists, sem_kv, sem_rows, *, n_keep):
    i = pl.program_id(0)
    parity = i % 2
    has_next = i + 1 < pl.num_programs(0)

    def lists_copy(src_ref, half):
        return pltpu.make_async_copy(src_ref, row_lists.at[pl.ds(half * GQ, GQ)], sem_rows)

    @pl.when(i == 0)
    def _():
        cp = pltpu.make_async_copy(kv_hbm, kv_vmem, sem_kv)
        cp.start()
        cp.wait()
        first = lists_copy(idx_ref, 0)
        first.start()
        first.wait()

    @pl.when(has_next)
    def _():
        lists_copy(idx_next_ref, 1 - parity).start()

    n_col = n_keep * KV_ROWS
    colg = lax.broadcasted_iota(I32, (N_HEADS, n_col), 1)
    rowh = lax.broadcasted_iota(I32, (N_HEADS, n_col), 0)
    own_group = (colg % KV_ROWS) == (rowh // HEADS_PER_GROUP)
    col_slot = colg // KV_ROWS
    scale = HEAD_DIM ** -0.5

    last = GQ - 1
    stages = (stage_a, stage_b, stage_c, stage_d)
    s_bufs = (s_a, s_b)

    def gather(r, stage):
        rows = row_lists.at[pl.ds(parity * GQ + jnp.minimum(r, last), 1)]
        for j in range(n_keep):
            row = pl.multiple_of(rows[0, j], KV_ROWS)
            stage[j * KV_ROWS:(j + 1) * KV_ROWS, :] = kv_vmem[pl.ds(row, KV_ROWS), :]

    def scores(r, stage, s_buf):
        k_all = lax.bitcast_convert_type(stage[...] & jnp.int32(-65536), F32).astype(BF16)
        qh = q_ref[pl.ds(pl.multiple_of(r * N_HEADS, N_HEADS), N_HEADS), :]
        s = lax.dot_general(qh, k_all, (((1,), (1,)), ((), ())), preferred_element_type=F32) * scale
        mask = own_group & (col_slot < nv_ref[i * GQ + r])
        s_buf[...] = jnp.where(mask, s, NEG_INF)

    def combine(r, stage, s_buf):
        s = s_buf[...]
        p = jnp.exp(s - jnp.max(s, axis=1, keepdims=True))
        l = jnp.sum(p, axis=1, keepdims=True)
        v_all = lax.bitcast_convert_type(jnp.left_shift(stage[...], 16), F32).astype(BF16)
        o = jnp.dot(p.astype(BF16), v_all, preferred_element_type=F32) / l
        o_ref[pl.ds(pl.multiple_of(r * N_HEADS, N_HEADS), N_HEADS), :] = o.astype(o_ref.dtype)

    gather(0, stages[0])
    gather(1, stages[1])
    scores(0, stages[0], s_bufs[0])

    def query_group(it, carry):
        r = GATTN_UNROLL * it
        for u in range(GATTN_UNROLL):
            gather(r + u + 2, stages[(u + 2) % 4])
            scores(jnp.minimum(r + u + 1, last), stages[(u + 1) % 4], s_bufs[(u + 1) % 2])
            combine(r + u, stages[u % 4], s_bufs[u % 2])
        return carry

    lax.fori_loop(0, GQ // GATTN_UNROLL, query_group, 0)

    @pl.when(has_next)
    def _():
        lists_copy(idx_next_ref, 1 - parity).wait()


def _gattn(nv, idx, q2, kv2, n_keep):
    t = nv.shape[0]
    grid_spec = pltpu.PrefetchScalarGridSpec(
        num_scalar_prefetch=1,
        grid=(t // GQ,),
        in_specs=[pl.BlockSpec((GQ, n_keep), lambda i, nv: (i, 0)),
                  pl.BlockSpec((GQ, n_keep), lambda i, nv: (jnp.minimum(i + 1, t // GQ - 1), 0)),
                  pl.BlockSpec((GQ * N_HEADS, HEAD_DIM), lambda i, nv: (i, 0)),
                  pl.BlockSpec(memory_space=pl.ANY)],
        out_specs=pl.BlockSpec((GQ * N_HEADS, HEAD_DIM), lambda i, nv: (i, 0)),
        scratch_shapes=[pltpu.VMEM(kv2.shape, I32),
                        pltpu.VMEM((n_keep * KV_ROWS, LANES), I32),
                        pltpu.VMEM((n_keep * KV_ROWS, LANES), I32),
                        pltpu.VMEM((n_keep * KV_ROWS, LANES), I32),
                        pltpu.VMEM((n_keep * KV_ROWS, LANES), I32),
                        pltpu.VMEM((N_HEADS, n_keep * KV_ROWS), F32),
                        pltpu.VMEM((N_HEADS, n_keep * KV_ROWS), F32),
                        pltpu.SMEM((2 * GQ, n_keep), I32),
                        pltpu.SemaphoreType.DMA,
                        pltpu.SemaphoreType.DMA],
    )
    return pl.pallas_call(
        functools.partial(_gattn_kernel, n_keep=n_keep),
        grid_spec=grid_spec,
        out_shape=jax.ShapeDtypeStruct((t * N_HEADS, HEAD_DIM), BF16),
        compiler_params=_cparams(("arbitrary",)),
        name="gattn",
    )(nv, idx, idx, q2, kv2)


def _mm_res_kernel(a_ref, w_ref, b_ref, x_ref, g_ref, o_ref):
    y = jnp.dot(a_ref[...], w_ref[...], preferred_element_type=F32) + b_ref[...]
    o_ref[...] = x_ref[...] + g_ref[...] * y


def _mm_res(a, w, bias, x, gate, tm=512, tn=512):
    t, k = a.shape
    n = w.shape[1]
    return pl.pallas_call(
        _mm_res_kernel,
        grid=(t // tm, n // tn),
        in_specs=[pl.BlockSpec((tm, k), lambda i, j: (i, 0)),
                  pl.BlockSpec((k, tn), lambda i, j: (0, j)),
                  pl.BlockSpec((1, tn), lambda i, j: (0, j)),
                  pl.BlockSpec((tm, tn), lambda i, j: (i, j)),
                  pl.BlockSpec((1, tn), lambda i, j: (0, j))],
        out_specs=pl.BlockSpec((tm, tn), lambda i, j: (i, j)),
        out_shape=jax.ShapeDtypeStruct((t, n), F32),
        compiler_params=_cparams(("arbitrary", "arbitrary")),
        name="mm_res",
    )(a, w, bias.reshape(1, n), x, gate.reshape(1, n))


def _glu_kernel(a_ref, wa_ref, wg_ref, ba_ref, bg_ref, o_ref):
    a = a_ref[...]
    ya = jnp.dot(a, wa_ref[...], preferred_element_type=F32) + ba_ref[...]
    yg = jnp.dot(a, wg_ref[...], preferred_element_type=F32) + bg_ref[...]
    o_ref[...] = ya * jax.nn.sigmoid(yg)


def _glu(a, w, b, tm=512, tn=512):
    t, k = a.shape
    n = w.shape[1] // 2
    nb = n // tn
    b2 = b.reshape(1, 2 * n)
    return pl.pallas_call(
        _glu_kernel,
        grid=(t // tm, nb),
        in_specs=[pl.BlockSpec((tm, k), lambda i, j: (i, 0)),
                  pl.BlockSpec((k, tn), lambda i, j: (0, j)),
                  pl.BlockSpec((k, tn), lambda i, j: (0, j + nb)),
                  pl.BlockSpec((1, tn), lambda i, j: (0, j)),
                  pl.BlockSpec((1, tn), lambda i, j: (0, j + nb))],
        out_specs=pl.BlockSpec((tm, tn), lambda i, j: (i, j)),
        out_shape=jax.ShapeDtypeStruct((t, n), F32),
        compiler_params=_cparams(("arbitrary", "arbitrary")),
        name="glu",
    )(a, w, w, b2, b2)


HALO = 32


SUBLANES = 8
CONV_ROWS = 32
CONV_COLS = 512


def _dwconv_ln_kernel(u_ref, prev_ref, w_ref, b_ref, g_ref, beta_ref, o_ref, ext_ref, sh_ref, conv_ref):
    i = pl.program_id(0)
    tm, d = u_ref.shape
    ext_ref[0:HALO, :] = jnp.where(i == 0, 0.0, prev_ref[...])
    ext_ref[HALO:HALO + tm, :] = u_ref[...]
    for phase in range(SUBLANES):
        sh_ref[phase, 0:tm + HALO - phase, :] = ext_ref[phase:tm + HALO, :]
    first = HALO - (CONV_WIDTH - 1)

    def conv_rows(c, carry):
        r0 = pl.multiple_of(c * CONV_ROWS, CONV_ROWS)
        for cb in range(d // CONV_COLS):
            cols = slice(cb * CONV_COLS, (cb + 1) * CONV_COLS)
            acc = jnp.zeros((CONV_ROWS, CONV_COLS), F32) + b_ref[:, cols]
            for k in range(CONV_WIDTH):
                phase = (first + k) % SUBLANES
                start = pl.multiple_of(r0 + (first + k - phase), SUBLANES)
                acc = acc + sh_ref[phase, pl.ds(start, CONV_ROWS), cols] * w_ref[k:k + 1, cols]
            conv_ref[pl.ds(r0, CONV_ROWS), cols] = acc
        return carry

    lax.fori_loop(0, tm // CONV_ROWS, conv_rows, 0)
    acc = conv_ref[...]
    mu = jnp.mean(acc, axis=-1, keepdims=True)
    cen = acc - mu
    var = jnp.mean(cen * cen, axis=-1, keepdims=True)
    y = (cen * lax.rsqrt(var + EPS)) * g_ref[...] + beta_ref[...]
    o_ref[...] = (y * jax.nn.sigmoid(y)).astype(o_ref.dtype)


def _dwconv_ln(u, w_dw, b_dw, ln_g, ln_b, tm=256):
    t, d = u.shape
    ratio = tm // HALO
    vec = pl.BlockSpec((1, d), lambda i: (0, 0))
    return pl.pallas_call(
        _dwconv_ln_kernel,
        grid=(t // tm,),
        in_specs=[pl.BlockSpec((tm, d), lambda i: (i, 0)),
                  pl.BlockSpec((HALO, d), lambda i: (jnp.maximum(i * ratio - 1, 0), 0)),
                  pl.BlockSpec((HALO, d), lambda i: (0, 0)),
                  vec, vec, vec],
        out_specs=pl.BlockSpec((tm, d), lambda i: (i, 0)),
        out_shape=jax.ShapeDtypeStruct((t, d), BF16),
        scratch_shapes=[pltpu.VMEM((HALO + tm, d), F32), pltpu.VMEM((SUBLANES, HALO + tm, d), F32),
                        pltpu.VMEM((tm, d), F32)],
        compiler_params=_cparams(("arbitrary",)),
        name="dwconv_ln",
    )(u, u, jnp.pad(w_dw, ((0, HALO - CONV_WIDTH), (0, 0))), b_dw.reshape(1, d), ln_g.reshape(1, d), ln_b.reshape(1, d))


def _row_copy(src, src_row, dst, dst_row, sem):
    return pltpu.make_async_copy(src.at[pl.ds(src_row, 1), :], dst.at[pl.ds(dst_row, 1), :], sem)


ROW_UNROLL = 8


def _for_rows(n, fn):
    def group(gi, carry):
        for u in range(ROW_UNROLL):
            fn(gi * ROW_UNROLL + u, carry)
        return carry

    lax.fori_loop(0, n // ROW_UNROLL, group, 0)


def _dispatch_kernel(dest_ref, h_ref, aux_ref, xs_in, ws_in, xs_hbm, ws_hbm, sem_x, sem_w):
    del xs_in, ws_in
    tm = h_ref.shape[0]
    t0 = pl.program_id(0) * tm

    def issue(r, carry):
        slot = dest_ref[t0 + r]
        _row_copy(h_ref, r, xs_hbm, slot, sem_x).start()
        _row_copy(aux_ref, r, ws_hbm, slot, sem_w).start()
        return carry

    def drain(r, carry):
        _row_copy(h_ref, r, xs_hbm, 0, sem_x).wait()
        _row_copy(aux_ref, r, ws_hbm, 0, sem_w).wait()
        return carry

    _for_rows(tm, issue)
    _for_rows(tm, drain)


def _dispatch(dest, h, aux, n_rows, tm=256):
    t, d = h.shape
    grid_spec = pltpu.PrefetchScalarGridSpec(
        num_scalar_prefetch=1,
        grid=(t // tm,),
        in_specs=[pl.BlockSpec((tm, d), lambda i, dest: (i, 0)),
                  pl.BlockSpec((tm, LANES), lambda i, dest: (i, 0)),
                  pl.BlockSpec(memory_space=pl.ANY),
                  pl.BlockSpec(memory_space=pl.ANY)],
        out_specs=[pl.BlockSpec(memory_space=pl.ANY), pl.BlockSpec(memory_space=pl.ANY)],
        scratch_shapes=[pltpu.SemaphoreType.DMA, pltpu.SemaphoreType.DMA],
    )
    return pl.pallas_call(
        _dispatch_kernel,
        grid_spec=grid_spec,
        out_shape=[jax.ShapeDtypeStruct((n_rows, d), F32), jax.ShapeDtypeStruct((n_rows, LANES), F32)],
        input_output_aliases={3: 0, 4: 1},
        compiler_params=_cparams(("arbitrary",)),
        name="dispatch",
    )(dest, h, aux, jnp.zeros((n_rows, d), F32), jnp.zeros((n_rows, LANES), F32))


def _ffn_kernel(elo_ref, ehi_ref, live_ref, xs_ref, ws_ref, wg_lo, wu_lo, wd_lo, wg_hi, wu_hi, wd_hi, ys_ref):
    b = pl.program_id(0)

    @pl.when(live_ref[b] == 1)
    def _():
        x = xs_ref[...].astype(BF16)
        ws = ws_ref[...]

        def expert(wg, wu, wd):
            gate = jnp.dot(x, wg[0], preferred_element_type=F32)
            up = jnp.dot(x, wu[0], preferred_element_type=F32)
            hid = (gate * jax.nn.sigmoid(gate)) * up
            return jnp.dot(hid.astype(BF16), wd[0], preferred_element_type=F32)

        ys_ref[...] = (expert(wg_lo, wu_lo, wd_lo) * ws[:, AUX_W_LO:AUX_W_LO + 1]
                       + expert(wg_hi, wu_hi, wd_hi) * ws[:, AUX_W_HI:AUX_W_HI + 1])

    @pl.when(live_ref[b] == 0)
    def _():
        ys_ref[...] = jnp.zeros_like(ys_ref)


def _ffn(e_lo, e_hi, live, xs, ws, w_gate, w_up, w_down):
    n_rows, d = xs.shape
    _, _, de = w_gate.shape
    lo_in = pl.BlockSpec((1, d, de), lambda b, lo, hi, live: (lo[b], 0, 0))
    hi_in = pl.BlockSpec((1, d, de), lambda b, lo, hi, live: (hi[b], 0, 0))
    lo_out = pl.BlockSpec((1, de, d), lambda b, lo, hi, live: (lo[b], 0, 0))
    hi_out = pl.BlockSpec((1, de, d), lambda b, lo, hi, live: (hi[b], 0, 0))
    grid_spec = pltpu.PrefetchScalarGridSpec(
        num_scalar_prefetch=3,
        grid=(n_rows // MOE_TM,),
        in_specs=[pl.BlockSpec((MOE_TM, d), lambda b, lo, hi, live: (b, 0)),
                  pl.BlockSpec((MOE_TM, LANES), lambda b, lo, hi, live: (b, 0)),
                  lo_in, lo_in, lo_out, hi_in, hi_in, hi_out],
        out_specs=pl.BlockSpec((MOE_TM, d), lambda b, lo, hi, live: (b, 0)),
    )
    return pl.pallas_call(
        _ffn_kernel,
        grid_spec=grid_spec,
        out_shape=jax.ShapeDtypeStruct((n_rows, d), F32),
        compiler_params=_cparams(("arbitrary",)),
        name="ffn",
    )(e_lo, e_hi, live, xs, ws, w_gate, w_up, w_down, w_gate, w_up, w_down)


def _combine_kernel(dest_ref, x_ref, g_ref, fg_ref, ys_hbm, o_ref, buf, sem, *, final_norm):
    tm = x_ref.shape[0]
    t0 = pl.program_id(0) * tm

    def issue(r, carry):
        _row_copy(ys_hbm, dest_ref[t0 + r], buf, r, sem).start()
        return carry

    def drain(r, carry):
        _row_copy(ys_hbm, 0, buf, r, sem).wait()
        return carry

    _for_rows(tm, issue)
    _for_rows(tm, drain)
    y = x_ref[...] + g_ref[...] * buf[...]
    if final_norm:
        y = (y * lax.rsqrt(jnp.mean(y * y, axis=-1, keepdims=True) + EPS)) * fg_ref[...]
    o_ref[...] = y


def _combine(dest, x, gate, final_g, ys, final_norm, tm=256):
    t, d = x.shape
    vec = pl.BlockSpec((1, d), lambda i, dest: (0, 0))
    grid_spec = pltpu.PrefetchScalarGridSpec(
        num_scalar_prefetch=1,
        grid=(t // tm,),
        in_specs=[pl.BlockSpec((tm, d), lambda i, dest: (i, 0)), vec, vec, pl.BlockSpec(memory_space=pl.ANY)],
        out_specs=pl.BlockSpec((tm, d), lambda i, dest: (i, 0)),
        scratch_shapes=[pltpu.VMEM((tm, d), F32), pltpu.SemaphoreType.DMA],
    )
    return pl.pallas_call(
        functools.partial(_combine_kernel, final_norm=final_norm),
        grid_spec=grid_spec,
        out_shape=jax.ShapeDtypeStruct((t, d), F32),
        compiler_params=_cparams(("arbitrary",)),
        name="combine",
    )(dest, x, gate.reshape(1, d), final_g.reshape(1, d), ys)


def _rope_tables(t):
    pos = jnp.arange(t, dtype=I32).astype(F32)[:, None]
    lane = jnp.arange(LANES)

    def table(dim):
        half = dim // 2
        inv = ROPE_THETA ** (-jnp.arange(half, dtype=F32) / half)
        ang = pos * inv[None, :]
        j = lane % dim
        cos = jnp.cos(ang)[:, j % half]
        sin = jnp.sin(ang)[:, j % half] * jnp.where(j < half, -1.0, 1.0)[None, :]
        return cos, sin

    return table(HEAD_DIM), table(IDX_DIM)


def _attention_layer(x, mod, norm_g, w_in, w_out, tables):
    t, d = x.shape
    sh1, sc1, g1 = mod[0:d], mod[d:2 * d], mod[2 * d:3 * d]
    (cos_h, sin_h), (cos_i, sin_i) = tables
    hn = _normmod(x, norm_g, sc1, sh1)
    w_bf = w_in.astype(BF16)
    nq = N_HEADS * HEAD_DIM
    nkv = N_KV_GROUPS * HEAD_DIM
    nqi = IDX_HEADS * IDX_DIM
    tn = 512
    q = _proj_rope(hn, w_bf, 0, nq, cos_h, sin_h, HEAD_DIM // 2, BF16)
    kv = _proj_kv(hn, w_bf, nq // nkv, nq // nkv + 1, cos_h, sin_h)
    qi = _proj_rope(hn, w_bf, (nq + 2 * nkv) // tn, nqi, cos_i, sin_i, IDX_DIM // 2, BF16)
    tail0 = nq + 2 * nkv + nqi
    w_tail = jnp.pad(w_bf[:, tail0:], ((0, 0), (0, LANES - (IDX_DIM + IDX_HEADS))))
    lane = jnp.arange(LANES)
    wi_scale = (IDX_HEADS ** -0.5) * (IDX_DIM ** -0.5)
    cos_t = jnp.where(lane[None, :] < IDX_DIM, cos_i, wi_scale)
    sin_t = jnp.where(lane[None, :] < IDX_DIM, sin_i, 0.0)
    tail = _proj_rope(hn, w_tail, 0, LANES, cos_t, sin_t, IDX_DIM // 2, F32, tn=LANES)
    ki = tail[:, :IDX_DIM].astype(BF16)
    wit = tail[:, IDX_DIM:IDX_DIM + IDX_HEADS].T
    n_keep = min(TOPK_KEYS, t // 4)
    idx_t, nv = _indexer(qi.T, ki, wit, n_keep)
    att = _gattn(nv.reshape(t), idx_t.T, q.reshape(t * N_HEADS, HEAD_DIM), kv.reshape(t * KV_ROWS, LANES), n_keep)
    att = att.reshape(t, nq)
    return _mm_res(att, w_out.astype(BF16), jnp.zeros((d,), F32), x, g1)


def _conv_layer(x, mod, norm_g, w_pw1, b_pw1, w_dw, b_dw, ln_g, ln_b, w_pw2, b_pw2):
    t, d = x.shape
    sh1, sc1, g1 = mod[0:d], mod[d:2 * d], mod[2 * d:3 * d]
    hn = _normmod(x, norm_g, sc1, sh1)
    u = _glu(hn, w_pw1.astype(BF16), b_pw1)
    v = _dwconv_ln(u, w_dw, b_dw, ln_g, ln_b)
    return _mm_res(v, w_pw2.astype(BF16), b_pw2, x, g1)


def _moe_layer(x, mod, norm_g, w_rg, b_rg, w_re, b_re, w_gate, w_up, w_down, final_g, final_norm):
    t, d = x.shape
    sh2, sc2, g2 = mod[3 * d:4 * d], mod[4 * d:5 * d], mod[5 * d:6 * d]
    h, aux, counts = _route(x, norm_g, sc2, sh2, w_rg, b_rg, w_re, b_re)
    bucket = aux[:, AUX_BUCKET].astype(I32)
    rank = aux[:, AUX_RANK].astype(I32)
    sizes = counts[0, :N_BUCKETS].astype(I32)
    padded = (sizes + MOE_TM - 1) // MOE_TM * MOE_TM
    ends = jnp.cumsum(padded)
    dest = (ends - padded)[bucket] + rank
    n_blocks = t // MOE_TM + N_BUCKETS
    blk_start = jnp.arange(n_blocks, dtype=I32) * MOE_TM
    blk_bucket = jnp.minimum(jnp.searchsorted(ends, blk_start, side='right'), N_BUCKETS - 1).astype(I32)
    live = (blk_start < ends[-1]).astype(I32)
    group0 = (blk_bucket // N_PAIRS) * EXPERTS_PER_GROUP
    e_lo = group0 + jnp.asarray(PAIR_LO, I32)[blk_bucket % N_PAIRS]
    e_hi = group0 + jnp.asarray(PAIR_HI, I32)[blk_bucket % N_PAIRS]
    xs, ws = _dispatch(dest, h, aux, n_blocks * MOE_TM)
    ys = _ffn(e_lo, e_hi, live, xs, ws, w_gate.astype(BF16), w_up.astype(BF16), w_down.astype(BF16))
    return _combine(dest, x, g2, final_g, ys, final_norm)


def kernel(x, c, ada_w, ada_b, norm1_g, norm2_g, attn_w_in, attn_w_out, conv_w_pw1, conv_b_pw1, conv_w_dw, conv_b_dw, conv_ln_g, conv_ln_b, conv_w_pw2, conv_b_pw2, router_w_group, router_b_group, router_w_expert, router_b_expert, exp_w_gate, exp_w_up, exp_w_down, final_g):
    bsz, t, d = x.shape
    assert bsz == 1 and d == N_HEADS * HEAD_DIM
    depth = ada_w.shape[0]
    mods = _ada_mod(c, ada_w, ada_b)
    tables = _rope_tables(t)
    xf = x.reshape(t, d)
    for i in range(depth):
        j = i // 2
        if i % 2 == 0:
            xf = _attention_layer(xf, mods[i], norm1_g[i], attn_w_in[j], attn_w_out[j], tables)
        else:
            xf = _conv_layer(xf, mods[i], norm1_g[i], conv_w_pw1[j], conv_b_pw1[j], conv_w_dw[j], conv_b_dw[j],
                             conv_ln_g[j], conv_ln_b[j], conv_w_pw2[j], conv_b_pw2[j])
        xf = _moe_layer(xf, mods[i], norm2_g[i], router_w_group[i], router_b_group[i], router_w_expert[i],
                        router_b_expert[i], exp_w_gate[i], exp_w_up[i], exp_w_down[i], final_g, i == depth - 1)
    return xf.reshape(bsz, t, d)
```

```python
import functools

import jax
import jax.numpy as jnp
from jax import lax
from jax.experimental import pallas as pl
from jax.experimental.pallas import tpu as pltpu

EPS = 1e-6
CHUNK = 64
N_HEADS = 16
N_KV_GROUPS = 4
HEADS_PER_GROUP = N_HEADS // N_KV_GROUPS
HEAD_DIM = 128
IDX_HEADS = 16
IDX_DIM = 64
TOPK_KEYS = 256
ROPE_THETA = 10000.0
NEG_INF = -1e30
CONV_WIDTH = 31
N_GROUPS = 4
EXPERTS_PER_GROUP = 4
N_EXPERTS = N_GROUPS * EXPERTS_PER_GROUP

LANES = 128
KV_ROWS = N_KV_GROUPS
INT_MIN = -(2 ** 31)
INVALID_SHIFT = 1 << 30
INVALID_PACKED = 1 << 14
VMEM_LIMIT = 56 * 1024 * 1024

F32 = jnp.float32
BF16 = jnp.bfloat16
I32 = jnp.int32
I16 = jnp.int16


def _cparams(sem):
    return pltpu.CompilerParams(dimension_semantics=sem, vmem_limit_bytes=VMEM_LIMIT)


def _ada_kernel(c_ref, w_ref, b_ref, o_ref):
    c = c_ref[...]
    ca = c * jax.nn.sigmoid(c)
    o_ref[0] = jnp.dot(ca, w_ref[0], preferred_element_type=F32, precision=lax.Precision.HIGHEST) + b_ref[0]


def _ada_mod(c, ada_w, ada_b):
    depth, d, n = ada_w.shape
    tn = 768
    c8 = jnp.zeros((8, d), F32).at[0].set(c[0])
    out = pl.pallas_call(
        _ada_kernel,
        grid=(depth, n // tn),
        in_specs=[pl.BlockSpec((8, d), lambda i, j: (0, 0)),
                  pl.BlockSpec((1, d, tn), lambda i, j: (i, 0, j)),
                  pl.BlockSpec((1, 1, tn), lambda i, j: (i, 0, j))],
        out_specs=pl.BlockSpec((1, 8, tn), lambda i, j: (i, 0, j)),
        out_shape=jax.ShapeDtypeStruct((depth, 8, n), F32),
        compiler_params=_cparams(("arbitrary", "arbitrary")),
        name="ada_mod",
    )(c8, ada_w, ada_b.reshape(depth, 1, n))
    return out[:, 0, :]


def _normmod_kernel(x_ref, g_ref, sc_ref, sh_ref, o_ref):
    x = x_ref[...]
    xn = x * lax.rsqrt(jnp.mean(x * x, axis=-1, keepdims=True) + EPS)
    o_ref[...] = ((xn * g_ref[...]) * (1.0 + sc_ref[...]) + sh_ref[...]).astype(o_ref.dtype)


def _normmod(x, g, sc, sh, tm=512):
    t, d = x.shape
    vec = pl.BlockSpec((1, d), lambda i: (0, 0))
    return pl.pallas_call(
        _normmod_kernel,
        grid=(t // tm,),
        in_specs=[pl.BlockSpec((tm, d), lambda i: (i, 0)), vec, vec, vec],
        out_specs=pl.BlockSpec((tm, d), lambda i: (i, 0)),
        out_shape=jax.ShapeDtypeStruct((t, d), BF16),
        compiler_params=_cparams(("arbitrary",)),
        name="normmod",
    )(x, g.reshape(1, d), sc.reshape(1, d), sh.reshape(1, d))


N_PAIRS = EXPERTS_PER_GROUP * (EXPERTS_PER_GROUP - 1) // 2
N_BUCKETS = N_GROUPS * N_PAIRS
PAIR_LO = (0, 0, 0, 1, 1, 2)
PAIR_HI = (1, 2, 3, 2, 3, 3)
MOE_TM = 256
AUX_W_LO, AUX_W_HI, AUX_BUCKET, AUX_RANK = 0, 1, 2, 3


def _route_kernel(x_ref, g_ref, sc_ref, sh_ref, wr_ref, br_ref, tri_ref, h_ref, aux_ref, cnt_ref, run_ref):
    @pl.when(pl.program_id(0) == 0)
    def _():
        run_ref[...] = jnp.zeros_like(run_ref)

    x = x_ref[...]
    xn = x * lax.rsqrt(jnp.mean(x * x, axis=-1, keepdims=True) + EPS)
    h = (xn * g_ref[...]) * (1.0 + sc_ref[...]) + sh_ref[...]
    h_ref[...] = h.astype(h_ref.dtype)
    h_hi = h.astype(BF16)
    h_lo = (h - h_hi.astype(F32)).astype(BF16)
    w = wr_ref[...]
    w_hi = w.astype(BF16)
    w_lo = (w - w_hi.astype(F32)).astype(BF16)
    logits = (jnp.dot(h_hi, w_hi, preferred_element_type=F32) + jnp.dot(h_lo, w_hi, preferred_element_type=F32)
              + jnp.dot(h_hi, w_lo, preferred_element_type=F32)) + br_ref[...]
    tm = x.shape[0]
    lane4 = lax.broadcasted_iota(I32, (tm, N_GROUPS), 1)

    def first_argmax(v):
        m = jnp.max(v, axis=-1, keepdims=True)
        return m, jnp.min(jnp.where(v == m, lane4, N_GROUPS), axis=-1, keepdims=True)

    gl = logits[:, 0:N_GROUPS]
    gmax, gidx = first_argmax(gl)
    g_w = 1.0 / jnp.sum(jnp.exp(gl - gmax), axis=-1, keepdims=True)
    el = jnp.zeros((tm, EXPERTS_PER_GROUP), F32)
    for g in range(N_GROUPS):
        lo = N_GROUPS + g * EXPERTS_PER_GROUP
        el = jnp.where(gidx == g, logits[:, lo:lo + EXPERTS_PER_GROUP], el)
    emax = jnp.max(el, axis=-1, keepdims=True)
    ee = jnp.exp(el - emax)
    ep = ee / jnp.sum(ee, axis=-1, keepdims=True)
    p1, i1 = first_argmax(ep)
    rest = jnp.where(lane4 == i1, -1.0, ep)
    p2, i2 = first_argmax(rest)
    den = p1 + p2
    w1 = g_w * p1 / den
    w2 = g_w * p2 / den
    first_lo = i1 < i2
    a = jnp.minimum(i1, i2)
    b = jnp.maximum(i1, i2)
    w_lo = jnp.where(first_lo, w1, w2)
    w_hi = jnp.where(first_lo, w2, w1)
    pair = jnp.where(a == 0, b - 1, jnp.where(a == 1, b + 1, N_PAIRS - 1))
    bucket = gidx * N_PAIRS + pair
    lane = lax.broadcasted_iota(I32, (tm, LANES), 1)
    onehot = lane == bucket
    prefix = jnp.dot(tri_ref[...], onehot.astype(F32).astype(BF16), preferred_element_type=F32)
    run = run_ref[...]
    rank = jnp.sum(jnp.where(onehot, prefix + run, 0.0), axis=1, keepdims=True) - 1.0
    run = run + prefix[tm - 1:tm, :]
    run_ref[...] = run
    cnt_ref[...] = run
    aux_ref[...] = (jnp.where(lane == AUX_W_LO, w_lo, 0.0) + jnp.where(lane == AUX_W_HI, w_hi, 0.0)
                    + jnp.where(lane == AUX_BUCKET, bucket.astype(F32), 0.0) + jnp.where(lane == AUX_RANK, rank, 0.0))


def _route(x, g, sc, sh, w_rg, b_rg, w_re, b_re, tm=512):
    t, d = x.shape
    w_all = jnp.concatenate([w_rg, jnp.moveaxis(w_re, 0, 1).reshape(d, N_EXPERTS)], axis=1)
    b_all = jnp.concatenate([b_rg, b_re.reshape(N_EXPERTS)])
    npad = LANES - w_all.shape[1]
    w_all = jnp.pad(w_all, ((0, 0), (0, npad)))
    b_all = jnp.pad(b_all, (0, npad)).reshape(1, LANES)
    vec = pl.BlockSpec((1, d), lambda i: (0, 0))
    r = lax.broadcasted_iota(I32, (tm, tm), 0)
    c = lax.broadcasted_iota(I32, (tm, tm), 1)
    tri = (c <= r).astype(BF16)
    return pl.pallas_call(
        _route_kernel,
        grid=(t // tm,),
        in_specs=[pl.BlockSpec((tm, d), lambda i: (i, 0)), vec, vec, vec,
                  pl.BlockSpec((d, LANES), lambda i: (0, 0)), pl.BlockSpec((1, LANES), lambda i: (0, 0)),
                  pl.BlockSpec((tm, tm), lambda i: (0, 0))],
        out_specs=[pl.BlockSpec((tm, d), lambda i: (i, 0)), pl.BlockSpec((tm, LANES), lambda i: (i, 0)),
                   pl.BlockSpec((1, LANES), lambda i: (0, 0))],
        out_shape=[jax.ShapeDtypeStruct((t, d), F32), jax.ShapeDtypeStruct((t, LANES), F32),
                   jax.ShapeDtypeStruct((1, LANES), F32)],
        scratch_shapes=[pltpu.VMEM((1, LANES), F32)],
        compiler_params=_cparams(("arbitrary",)),
        name="route",
    )(x, g.reshape(1, d), sc.reshape(1, d), sh.reshape(1, d), w_all, b_all, tri)


def _rope_block(blk, cos, sin, half):
    if half == HEAD_DIM // 2:
        partner = pltpu.roll(blk, HEAD_DIM // 2, 1)
    else:
        lane = lax.broadcasted_iota(I32, blk.shape, 1)
        first = (lane % IDX_DIM) < (IDX_DIM // 2)
        partner = jnp.where(first, pltpu.roll(blk, LANES - IDX_DIM // 2, 1), pltpu.roll(blk, IDX_DIM // 2, 1))
    return blk * cos + partner * sin


def _proj_rope_kernel(a_ref, w_ref, cos_ref, sin_ref, o_ref, *, half):
    acc = jnp.dot(a_ref[...], w_ref[...], preferred_element_type=F32)
    cos = cos_ref[...]
    sin = sin_ref[...]
    for h in range(acc.shape[1] // LANES):
        sl = slice(h * LANES, (h + 1) * LANES)
        o_ref[:, sl] = _rope_block(acc[:, sl], cos, sin, half).astype(o_ref.dtype)


def _proj_rope(a, w, col_block0, n_out, cos, sin, half, out_dtype, tm=512, tn=512):
    t, d = a.shape
    return pl.pallas_call(
        functools.partial(_proj_rope_kernel, half=half),
        grid=(t // tm, n_out // tn),
        in_specs=[pl.BlockSpec((tm, d), lambda i, j: (i, 0)),
                  pl.BlockSpec((d, tn), lambda i, j: (0, col_block0 + j)),
                  pl.BlockSpec((tm, LANES), lambda i, j: (i, 0)),
                  pl.BlockSpec((tm, LANES), lambda i, j: (i, 0))],
        out_specs=pl.BlockSpec((tm, tn), lambda i, j: (i, j)),
        out_shape=jax.ShapeDtypeStruct((t, n_out), out_dtype),
        compiler_params=_cparams(("arbitrary", "arbitrary")),
        name="proj_rope",
    )(a, w, cos, sin)


def _proj_kv_kernel(a_ref, wk_ref, wv_ref, cos_ref, sin_ref, o_ref):
    a = a_ref[...]
    k = jnp.dot(a, wk_ref[...], preferred_element_type=F32)
    v = jnp.dot(a, wv_ref[...], preferred_element_type=F32)
    cos = cos_ref[...]
    sin = sin_ref[...]
    for h in range(k.shape[1] // LANES):
        sl = slice(h * LANES, (h + 1) * LANES)
        kr = _rope_block(k[:, sl], cos, sin, HEAD_DIM // 2)
        kb = lax.bitcast_convert_type(kr.astype(BF16).astype(F32), I32)
        vb = lax.bitcast_convert_type(v[:, sl].astype(BF16).astype(F32), I32)
        o_ref[:, sl] = kb | lax.shift_right_logical(vb, 16)


def _proj_kv(a, w, kblock, vblock, cos, sin, tm=512):
    t, d = a.shape
    n = N_KV_GROUPS * HEAD_DIM
    return pl.pallas_call(
        _proj_kv_kernel,
        grid=(t // tm,),
        in_specs=[pl.BlockSpec((tm, d), lambda i: (i, 0)),
                  pl.BlockSpec((d, n), lambda i: (0, kblock)),
                  pl.BlockSpec((d, n), lambda i: (0, vblock)),
                  pl.BlockSpec((tm, LANES), lambda i: (i, 0)),
                  pl.BlockSpec((tm, LANES), lambda i: (i, 0))],
        out_specs=pl.BlockSpec((tm, n), lambda i: (i, 0)),
        out_shape=jax.ShapeDtypeStruct((t, n), I32),
        compiler_params=_cparams(("arbitrary",)),
        name="proj_kv",
    )(a, w, w, cos, sin)


QB = 256
TK = 512
RC = 128


def _indexer_kernel(qit_ref, ki_ref, wit_ref, tri_ref, idx_ref, nv_ref, d_ref, hi_ref, *, n_keep):
    b = pl.program_id(0)
    q0 = b * QB
    n_t = (q0 + QB + TK - 1) // TK
    n_rows = n_t * TK
    rowi = lax.broadcasted_iota(I32, (TK, QB), 0)
    qchunk = (q0 + lax.broadcasted_iota(I32, (TK, QB), 1)) // CHUNK

    def admissible(k0):
        return ((k0 + rowi) // CHUNK) <= qchunk

    def score_tile(kt, carry):
        k0 = pl.multiple_of(kt * TK, TK)
        ki_t = ki_ref[pl.ds(k0, TK), :]
        acc = jnp.zeros((TK, QB), F32)
        for h in range(IDX_HEADS):
            a = jnp.dot(ki_t, qit_ref[h * IDX_DIM:(h + 1) * IDX_DIM, :], preferred_element_type=F32)
            acc = acc + jnp.maximum(a, 0.0) * wit_ref[h:h + 1, :]
        s = jnp.where(admissible(k0), acc + 0.0, NEG_INF)
        u = lax.bitcast_convert_type(s, I32)
        key = jnp.where(u < 0, u ^ 0x7FFFFFFF, u)
        d_ref[pl.ds(k0, TK), :] = key
        hi_ref[pl.ds(k0, TK), :] = lax.shift_right_arithmetic(key, 16).astype(I16)
        return carry

    def score_tile_pair(kp, carry):
        score_tile(2 * kp, carry)
        return score_tile(2 * kp + 1, carry)

    lax.fori_loop(0, n_t // 2, score_tile_pair, 0)

    @pl.when(n_t % 2 == 1)
    def _():
        score_tile(n_t - 1, 0)

    def count_where(pred):
        def body(kt, acc):
            k0 = pl.multiple_of(kt * TK, TK)
            hit = pred(d_ref[pl.ds(k0, TK), :]).astype(I32)
            return acc + jnp.sum(hit.reshape(TK // 8, 8, QB), axis=0)
        acc = lax.fori_loop(0, n_t, body, jnp.zeros((8, QB), I32))
        return jnp.sum(acc, axis=0, keepdims=True)

    def signed16(v_u):
        return lax.shift_right_arithmetic(jnp.left_shift(v_u ^ 0x8000, 16), 16).astype(I16)

    def add_packed_rows(acc, hit):
        for c in range(TK // 16):
            acc = acc + hit[c * 16:(c + 1) * 16, :]
        return acc

    def count_half_where(pred):
        def body(kt, acc):
            k0 = pl.multiple_of(kt * TK, TK)
            return add_packed_rows(acc, pred(hi_ref[pl.ds(k0, TK), :]).astype(I16))
        acc = lax.fori_loop(0, n_t, body, jnp.zeros((16, QB), I16))
        return jnp.sum(acc.astype(I32), axis=0, keepdims=True)

    def select_half(base):
        def select_bit(it, carry):
            v_u, cnt_v = carry
            cand_u = v_u | jnp.left_shift(jnp.int32(1), 15 - it)
            cand16 = signed16(cand_u)
            cnt = base + count_half_where(lambda half: half >= cand16)
            keep = cnt >= n_keep
            return jnp.where(keep, cand_u, v_u), jnp.where(keep, cnt, cnt_v)
        return select_bit

    zero_i = jnp.zeros((1, QB), I32)
    high_u, cnt_thr = lax.fori_loop(0, 16, select_half(zero_i), (zero_i, jnp.full((1, QB), n_keep + 1, I32)))
    high16 = signed16(high_u)

    def to_low_halves(kt, acc):
        k0 = pl.multiple_of(kt * TK, TK)
        high = hi_ref[pl.ds(k0, TK), :]
        low = signed16(d_ref[pl.ds(k0, TK), :] & 0xFFFF)
        hi_ref[pl.ds(k0, TK), :] = jnp.where(high == high16, low, jnp.int16(-(2 ** 15)))
        return add_packed_rows(acc, (high > high16).astype(I16))

    above = lax.fori_loop(0, n_t, to_low_halves, jnp.zeros((16, QB), I16))
    above = jnp.sum(above.astype(I32), axis=0, keepdims=True)
    low_u, cnt_thr = lax.fori_loop(0, 16, select_half(above), (zero_i, cnt_thr))
    thr = (jnp.left_shift(high_u, 16) | low_u) ^ INT_MIN

    tri = tri_ref[...]
    zero = jnp.zeros((1, QB), F32)

    def place(k0, sel, run_sel):
        pre_sel = jnp.dot(tri, sel.astype(F32).astype(BF16), preferred_element_type=F32)
        rank = (run_sel + pre_sel).astype(I32) - 1
        d_ref[pl.ds(k0, TK), :] = jnp.where(sel, (k0 + rowi) - rank, INVALID_SHIFT)
        return run_sel + pre_sel[TK - 1:TK, :]

    def rank_exact_count():
        def tile(kt, run_sel):
            k0 = pl.multiple_of(kt * TK, TK)
            return place(k0, admissible(k0) & (d_ref[pl.ds(k0, TK), :] >= thr), run_sel)
        return lax.fori_loop(0, n_t, tile, zero)

    def rank_with_ties():
        need = (n_keep - count_where(lambda key: key > thr)).astype(F32)

        def tile(kt, carry):
            run_eq, run_sel = carry
            k0 = pl.multiple_of(kt * TK, TK)
            key = d_ref[pl.ds(k0, TK), :]
            eq = key == thr
            eq_f = eq.astype(F32)
            pre_eq = jnp.dot(tri, eq_f.astype(BF16), preferred_element_type=F32)
            sel = admissible(k0) & ((key > thr) | (eq & ((run_eq + pre_eq - eq_f) < need)))
            return run_eq + pre_eq[TK - 1:TK, :], place(k0, sel, run_sel)
        return lax.fori_loop(0, n_t, tile, (zero, zero))[1]

    n_sel = lax.cond(jnp.max(jnp.abs(cnt_thr - n_keep)) == 0, rank_exact_count, rank_with_ties)

    p_ref = hi_ref
    packed_rows = 16
    d_ref[pl.ds(pl.multiple_of(n_rows, RC), RC), :] = jnp.full((RC, QB), INVALID_SHIFT, I32)
    p_ref[pl.ds(pl.multiple_of(n_rows, RC), RC), :] = jnp.full((RC, QB), INVALID_PACKED, I16)
    n_chunks = n_rows // RC
    win_chunks = -(-n_keep // RC)
    window = win_chunks * RC

    def first_stage(c, carry):
        r0 = pl.multiple_of(c * RC, RC)
        x = d_ref[pl.ds(r0, RC + 8), :]
        own, inc = x[0:RC], x[1:1 + RC]
        take = (inc & (1 | INVALID_SHIFT)) == 1
        stay = (own & (1 | INVALID_SHIFT)) == 0
        moved = jnp.where(take, inc, jnp.where(stay, own, INVALID_SHIFT))
        p_ref[pl.ds(r0, RC), :] = jnp.minimum(moved, INVALID_PACKED).astype(I16)
        return carry

    lax.fori_loop(0, n_chunks, first_stage, 0)
    n_stages = (d_ref.shape[0] - RC - 1).bit_length()
    for s in range(1, n_stages):
        sh = 1 << s
        test = sh | INVALID_PACKED

        def merge(own, inc, sh=sh, test=test):
            take = (inc & test) == sh
            stay = (own & test) == 0
            return jnp.where(take, inc, jnp.where(stay, own, INVALID_PACKED))

        def chunk_with_source(c, carry, sh=sh, merge=merge):
            r0 = pl.multiple_of(c * RC, RC)
            if sh < packed_rows:
                x = p_ref[pl.ds(r0, RC + packed_rows), :]
                own, inc = x[0:RC], x[sh:sh + RC]
            else:
                own = p_ref[pl.ds(r0, RC), :]
                inc = p_ref[pl.ds(pl.multiple_of(r0 + sh, packed_rows), RC), :]
            p_ref[pl.ds(r0, RC), :] = merge(own, inc)
            return carry

        def chunk_without_source(c, carry, merge=merge):
            r0 = pl.multiple_of(c * RC, RC)
            own = p_ref[pl.ds(r0, RC), :]
            p_ref[pl.ds(r0, RC), :] = merge(own, jnp.full_like(own, INVALID_PACKED))
            return carry

        def window_chunk(c, carry, sh=sh, merge=merge):
            r0 = pl.multiple_of((c // win_chunks) * sh + (c % win_chunks) * RC, RC)
            src = r0 + sh
            own = p_ref[pl.ds(r0, RC), :]
            inc = p_ref[pl.ds(pl.multiple_of(jnp.minimum(src, n_rows), RC), RC), :]
            p_ref[pl.ds(r0, RC), :] = merge(own, inc)
            return carry

        if sh < RC:
            lax.fori_loop(0, n_chunks, chunk_with_source, 0)
        elif sh < 2 * window:
            n_src = jnp.maximum(n_rows - sh, 0) // RC
            lax.fori_loop(0, n_src, chunk_with_source, 0)
            lax.fori_loop(n_src, n_chunks, chunk_without_source, 0)
        else:
            lax.fori_loop(0, ((n_rows + sh - 1) // sh) * win_chunks, window_chunk, 0)

    slot = lax.broadcasted_iota(I32, (n_keep, QB), 0)
    dd = p_ref[0:n_keep, :].astype(I32)
    idx_ref[...] = jnp.where(dd < INVALID_PACKED, (slot + dd) * KV_ROWS, 0)
    nv_ref[...] = n_sel.astype(I32)


def _indexer(qit, ki, wit, n_keep):
    t = ki.shape[0]
    assert t <= INVALID_PACKED and t % TK == 0
    r = lax.broadcasted_iota(I32, (TK, TK), 0)
    c = lax.broadcasted_iota(I32, (TK, TK), 1)
    tri = (c <= r).astype(BF16)
    return pl.pallas_call(
        functools.partial(_indexer_kernel, n_keep=n_keep),
        grid=(t // QB,),
        in_specs=[pl.BlockSpec((IDX_HEADS * IDX_DIM, QB), lambda b: (0, b)),
                  pl.BlockSpec((t, IDX_DIM), lambda b: (0, 0)),
                  pl.BlockSpec((IDX_HEADS, QB), lambda b: (0, b)),
                  pl.BlockSpec((TK, TK), lambda b: (0, 0))],
        out_specs=[pl.BlockSpec((n_keep, QB), lambda b: (0, b)), pl.BlockSpec((1, QB), lambda b: (0, b))],
        out_shape=[jax.ShapeDtypeStruct((n_keep, t), I32), jax.ShapeDtypeStruct((1, t), I32)],
        scratch_shapes=[pltpu.VMEM((t + RC, QB), I32), pltpu.VMEM((t + RC, QB), I16)],
        compiler_params=_cparams(("arbitrary",)),
        name="indexer",
    )(qit, ki, wit, tri)


GQ = 128
GATTN_UNROLL = 8


def _gattn_kernel(nv_ref, idx_ref, idx_next_ref, q_ref, kv_hbm, o_ref, kv_vmem, stage_a, stage_b, stage_c, stage_d,
                  s_a, s_b, row_lists, sem_kv, sem_rows, *, n_keep):
    i = pl.program_id(0)
    parity = i % 2
    has_next = i + 1 < pl.num_programs(0)

    def lists_copy(src_ref, half):
        return pltpu.make_async_copy(src_ref, row_lists.at[pl.ds(half * GQ, GQ)], sem_rows)

    @pl.when(i == 0)
    def _():
        cp = pltpu.make_async_copy(kv_hbm, kv_vmem, sem_kv)
        cp.start()
        cp.wait()
        first = lists_copy(idx_ref, 0)
        first.start()
        first.wait()

    @pl.when(has_next)
    def _():
        lists_copy(idx_next_ref, 1 - parity).start()

    n_col = n_keep * KV_ROWS
    colg = lax.broadcasted_iota(I32, (N_HEADS, n_col), 1)
    rowh = lax.broadcasted_iota(I32, (N_HEADS, n_col), 0)
    own_group = (colg % KV_ROWS) == (rowh // HEADS_PER_GROUP)
    col_slot = colg // KV_ROWS
    scale = HEAD_DIM ** -0.5

    last = GQ - 1
    stages = (stage_a, stage_b, stage_c, stage_d)
    s_bufs = (s_a, s_b)

    def gather(r, stage):
        rows = row_lists.at[pl.ds(parity * GQ + jnp.minimum(r, last), 1)]
        for j in range(n_keep):
            row = pl.multiple_of(rows[0, j], KV_ROWS)
            stage[j * KV_ROWS:(j + 1) * KV_ROWS, :] = kv_vmem[pl.ds(row, KV_ROWS), :]

    def scores(r, stage, s_buf):
        k_all = lax.bitcast_convert_type(stage[...] & jnp.int32(-65536), F32).astype(BF16)
        qh = q_ref[pl.ds(pl.multiple_of(r * N_HEADS, N_HEADS), N_HEADS), :]
        s = lax.dot_general(qh, k_all, (((1,), (1,)), ((), ())), preferred_element_type=F32) * scale
        mask = own_group & (col_slot < nv_ref[i * GQ + r])
        s_buf[...] = jnp.where(mask, s, NEG_INF)

    def combine(r, stage, s_buf):
        s = s_buf[...]
        p = jnp.exp(s - jnp.max(s, axis=1, keepdims=True))
        l = jnp.sum(p, axis=1, keepdims=True)
        v_all = lax.bitcast_convert_type(jnp.left_shift(stage[...], 16), F32).astype(BF16)
        o = jnp.dot(p.astype(BF16), v_all, preferred_element_type=F32) / l
        o_ref[pl.ds(pl.multiple_of(r * N_HEADS, N_HEADS), N_HEADS), :] = o.astype(o_ref.dtype)

    gather(0, stages[0])
    gather(1, stages[1])
    scores(0, stages[0], s_bufs[0])

    def query_group(it, carry):
        r = GATTN_UNROLL * it
        for u in range(GATTN_UNROLL):
            gather(r + u + 2, stages[(u + 2) % 4])
            scores(jnp.minimum(r + u + 1, last), stages[(u + 1) % 4], s_bufs[(u + 1) % 2])
            combine(r + u, stages[u % 4], s_bufs[u % 2])
        return carry

    lax.fori_loop(0, GQ // GATTN_UNROLL, query_group, 0)

    @pl.when(has_next)
    def _():
        lists_copy(idx_next_ref, 1 - parity).wait()


def _gattn(nv, idx, q2, kv2, n_keep):
    t = nv.shape[0]
    grid_spec = pltpu.PrefetchScalarGridSpec(
        num_scalar_prefetch=1,
        grid=(t // GQ,),
        in_specs=[pl.BlockSpec((GQ, n_keep), lambda i, nv: (i, 0)),
                  pl.BlockSpec((GQ, n_keep), lambda i, nv: (jnp.minimum(i + 1, t // GQ - 1), 0)),
                  pl.BlockSpec((GQ * N_HEADS, HEAD_DIM), lambda i, nv: (i, 0)),
                  pl.BlockSpec(memory_space=pl.ANY)],
        out_specs=pl.BlockSpec((GQ * N_HEADS, HEAD_DIM), lambda i, nv: (i, 0)),
        scratch_shapes=[pltpu.VMEM(kv2.shape, I32),
                        pltpu.VMEM((n_keep * KV_ROWS, LANES), I32),
                        pltpu.VMEM((n_keep * KV_ROWS, LANES), I32),
                        pltpu.VMEM((n_keep * KV_ROWS, LANES), I32),
                        pltpu.VMEM((n_keep * KV_ROWS, LANES), I32),
                        pltpu.VMEM((N_HEADS, n_keep * KV_ROWS), F32),
                        pltpu.VMEM((N_HEADS, n_keep * KV_ROWS), F32),
                        pltpu.SMEM((2 * GQ, n_keep), I32),
                        pltpu.SemaphoreType.DMA,
                        pltpu.SemaphoreType.DMA],
    )
    return pl.pallas_call(
        functools.partial(_gattn_kernel, n_keep=n_keep),
        grid_spec=grid_spec,
        out_shape=jax.ShapeDtypeStruct((t * N_HEADS, HEAD_DIM), BF16),
        compiler_params=_cparams(("arbitrary",)),
        name="gattn",
    )(nv, idx, idx, q2, kv2)


def _mm_res_kernel(a_ref, w_ref, b_ref, x_ref, g_ref, o_ref):
    y = jnp.dot(a_ref[...], w_ref[...], preferred_element_type=F32) + b_ref[...]
    o_ref[...] = x_ref[...] + g_ref[...] * y


def _mm_res(a, w, bias, x, gate, tm=512, tn=512):
    t, k = a.shape
    n = w.shape[1]
    return pl.pallas_call(
        _mm_res_kernel,
        grid=(t // tm, n // tn),
        in_specs=[pl.BlockSpec((tm, k), lambda i, j: (i, 0)),
                  pl.BlockSpec((k, tn), lambda i, j: (0, j)),
                  pl.BlockSpec((1, tn), lambda i, j: (0, j)),
                  pl.BlockSpec((tm, tn), lambda i, j: (i, j)),
                  pl.BlockSpec((1, tn), lambda i, j: (0, j))],
        out_specs=pl.BlockSpec((tm, tn), lambda i, j: (i, j)),
        out_shape=jax.ShapeDtypeStruct((t, n), F32),
        compiler_params=_cparams(("arbitrary", "arbitrary")),
        name="mm_res",
    )(a, w, bias.reshape(1, n), x, gate.reshape(1, n))


def _glu_kernel(a_ref, wa_ref, wg_ref, ba_ref, bg_ref, o_ref):
    a = a_ref[...]
    ya = jnp.dot(a, wa_ref[...], preferred_element_type=F32) + ba_ref[...]
    yg = jnp.dot(a, wg_ref[...], preferred_element_type=F32) + bg_ref[...]
    o_ref[...] = ya * jax.nn.sigmoid(yg)


def _glu(a, w, b, tm=512, tn=512):
    t, k = a.shape
    n = w.shape[1] // 2
    nb = n // tn
    b2 = b.reshape(1, 2 * n)
    return pl.pallas_call(
        _glu_kernel,
        grid=(t // tm, nb),
        in_specs=[pl.BlockSpec((tm, k), lambda i, j: (i, 0)),
                  pl.BlockSpec((k, tn), lambda i, j: (0, j)),
                  pl.BlockSpec((k, tn), lambda i, j: (0, j + nb)),
                  pl.BlockSpec((1, tn), lambda i, j: (0, j)),
                  pl.BlockSpec((1, tn), lambda i, j: (0, j + nb))],
        out_specs=pl.BlockSpec((tm, tn), lambda i, j: (i, j)),
        out_shape=jax.ShapeDtypeStruct((t, n), F32),
        compiler_params=_cparams(("arbitrary", "arbitrary")),
        name="glu",
    )(a, w, w, b2, b2)


HALO = 32


SUBLANES = 8
CONV_ROWS = 32
CONV_COLS = 512


def _dwconv_ln_kernel(u_ref, prev_ref, w_ref, b_ref, g_ref, beta_ref, o_ref, ext_ref, sh_ref, conv_ref):
    i = pl.program_id(0)
    tm, d = u_ref.shape
    ext_ref[0:HALO, :] = jnp.where(i == 0, 0.0, prev_ref[...])
    ext_ref[HALO:HALO + tm, :] = u_ref[...]
    for phase in range(SUBLANES):
        sh_ref[phase, 0:tm + HALO - phase, :] = ext_ref[phase:tm + HALO, :]
    first = HALO - (CONV_WIDTH - 1)

    def conv_rows(c, carry):
        r0 = pl.multiple_of(c * CONV_ROWS, CONV_ROWS)
        for cb in range(d // CONV_COLS):
            cols = slice(cb * CONV_COLS, (cb + 1) * CONV_COLS)
            acc = jnp.zeros((CONV_ROWS, CONV_COLS), F32) + b_ref[:, cols]
            for k in range(CONV_WIDTH):
                phase = (first + k) % SUBLANES
                start = pl.multiple_of(r0 + (first + k - phase), SUBLANES)
                acc = acc + sh_ref[phase, pl.ds(start, CONV_ROWS), cols] * w_ref[k:k + 1, cols]
            conv_ref[pl.ds(r0, CONV_ROWS), cols] = acc
        return carry

    lax.fori_loop(0, tm // CONV_ROWS, conv_rows, 0)
    acc = conv_ref[...]
    mu = jnp.mean(acc, axis=-1, keepdims=True)
    cen = acc - mu
    var = jnp.mean(cen * cen, axis=-1, keepdims=True)
    y = (cen * lax.rsqrt(var + EPS)) * g_ref[...] + beta_ref[...]
    o_ref[...] = (y * jax.nn.sigmoid(y)).astype(o_ref.dtype)


def _dwconv_ln(u, w_dw, b_dw, ln_g, ln_b, tm=256):
    t, d = u.shape
    ratio = tm // HALO
    vec = pl.BlockSpec((1, d), lambda i: (0, 0))
    return pl.pallas_call(
        _dwconv_ln_kernel,
        grid=(t // tm,),
        in_specs=[pl.BlockSpec((tm, d), lambda i: (i, 0)),
                  pl.BlockSpec((HALO, d), lambda i: (jnp.maximum(i * ratio - 1, 0), 0)),
                  pl.BlockSpec((HALO, d), lambda i: (0, 0)),
                  vec, vec, vec],
        out_specs=pl.BlockSpec((tm, d), lambda i: (i, 0)),
        out_shape=jax.ShapeDtypeStruct((t, d), BF16),
        scratch_shapes=[pltpu.VMEM((HALO + tm, d), F32), pltpu.VMEM((SUBLANES, HALO + tm, d), F32),
                        pltpu.VMEM((tm, d), F32)],
        compiler_params=_cparams(("arbitrary",)),
        name="dwconv_ln",
    )(u, u, jnp.pad(w_dw, ((0, HALO - CONV_WIDTH), (0, 0))), b_dw.reshape(1, d), ln_g.reshape(1, d), ln_b.reshape(1, d))


def _row_copy(src, src_row, dst, dst_row, sem):
    return pltpu.make_async_copy(src.at[pl.ds(src_row, 1), :], dst.at[pl.ds(dst_row, 1), :], sem)


ROW_UNROLL = 8


def _for_rows(n, fn):
    def group(gi, carry):
        for u in range(ROW_UNROLL):
            fn(gi * ROW_UNROLL + u, u % 2)
        return carry

    lax.fori_loop(0, n // ROW_UNROLL, group, 0)


def _dispatch_kernel(dest_ref, h_ref, aux_ref, xs_in, ws_in, xs_hbm, ws_hbm, sem_x, sem_w):
    del xs_in, ws_in
    tm = h_ref.shape[0]
    t0 = pl.program_id(0) * tm

    def issue(r, queue):
        slot = dest_ref[t0 + r]
        _row_copy(h_ref, r, xs_hbm, slot, sem_x).start(priority=queue)
        _row_copy(aux_ref, r, ws_hbm, slot, sem_w).start(priority=queue)

    def drain(r, queue):
        _row_copy(h_ref, r, xs_hbm, 0, sem_x).wait()
        _row_copy(aux_ref, r, ws_hbm, 0, sem_w).wait()

    _for_rows(tm, issue)
    _for_rows(tm, drain)


def _dispatch(dest, h, aux, n_rows, tm=256):
    t, d = h.shape
    grid_spec = pltpu.PrefetchScalarGridSpec(
        num_scalar_prefetch=1,
        grid=(t // tm,),
        in_specs=[pl.BlockSpec((tm, d), lambda i, dest: (i, 0)),
                  pl.BlockSpec((tm, LANES), lambda i, dest: (i, 0)),
                  pl.BlockSpec(memory_space=pl.ANY),
                  pl.BlockSpec(memory_space=pl.ANY)],
        out_specs=[pl.BlockSpec(memory_space=pl.ANY), pl.BlockSpec(memory_space=pl.ANY)],
        scratch_shapes=[pltpu.SemaphoreType.DMA, pltpu.SemaphoreType.DMA],
    )
    return pl.pallas_call(
        _dispatch_kernel,
        grid_spec=grid_spec,
        out_shape=[jax.ShapeDtypeStruct((n_rows, d), F32), jax.ShapeDtypeStruct((n_rows, LANES), F32)],
        input_output_aliases={3: 0, 4: 1},
        compiler_params=_cparams(("arbitrary",)),
        name="dispatch",
    )(dest, h, aux, jnp.zeros((n_rows, d), F32), jnp.zeros((n_rows, LANES), F32))


def _ffn_kernel(elo_ref, ehi_ref, live_ref, xs_ref, ws_ref, wg_lo, wu_lo, wd_lo, wg_hi, wu_hi, wd_hi, ys_ref):
    b = pl.program_id(0)

    @pl.when(live_ref[b] == 1)
    def _():
        x = xs_ref[...].astype(BF16)
        ws = ws_ref[...]

        def expert(wg, wu, wd):
            gate = jnp.dot(x, wg[0], preferred_element_type=F32)
            up = jnp.dot(x, wu[0], preferred_element_type=F32)
            hid = (gate * jax.nn.sigmoid(gate)) * up
            return jnp.dot(hid.astype(BF16), wd[0], preferred_element_type=F32)

        ys_ref[...] = (expert(wg_lo, wu_lo, wd_lo) * ws[:, AUX_W_LO:AUX_W_LO + 1]
                       + expert(wg_hi, wu_hi, wd_hi) * ws[:, AUX_W_HI:AUX_W_HI + 1])

    @pl.when(live_ref[b] == 0)
    def _():
        ys_ref[...] = jnp.zeros_like(ys_ref)


def _ffn(e_lo, e_hi, live, xs, ws, w_gate, w_up, w_down):
    n_rows, d = xs.shape
    _, _, de = w_gate.shape
    lo_in = pl.BlockSpec((1, d, de), lambda b, lo, hi, live: (lo[b], 0, 0))
    hi_in = pl.BlockSpec((1, d, de), lambda b, lo, hi, live: (hi[b], 0, 0))
    lo_out = pl.BlockSpec((1, de, d), lambda b, lo, hi, live: (lo[b], 0, 0))
    hi_out = pl.BlockSpec((1, de, d), lambda b, lo, hi, live: (hi[b], 0, 0))
    grid_spec = pltpu.PrefetchScalarGridSpec(
        num_scalar_prefetch=3,
        grid=(n_rows // MOE_TM,),
        in_specs=[pl.BlockSpec((MOE_TM, d), lambda b, lo, hi, live: (b, 0)),
                  pl.BlockSpec((MOE_TM, LANES), lambda b, lo, hi, live: (b, 0)),
                  lo_in, lo_in, lo_out, hi_in, hi_in, hi_out],
        out_specs=pl.BlockSpec((MOE_TM, d), lambda b, lo, hi, live: (b, 0)),
    )
    return pl.pallas_call(
        _ffn_kernel,
        grid_spec=grid_spec,
        out_shape=jax.ShapeDtypeStruct((n_rows, d), F32),
        compiler_params=_cparams(("arbitrary",)),
        name="ffn",
    )(e_lo, e_hi, live, xs, ws, w_gate, w_up, w_down, w_gate, w_up, w_down)


def _combine_kernel(dest_ref, x_ref, g_ref, fg_ref, ys_hbm, o_ref, buf, sem, *, final_norm):
    tm = x_ref.shape[0]
    t0 = pl.program_id(0) * tm

    def issue(r, queue):
        _row_copy(ys_hbm, dest_ref[t0 + r], buf, r, sem).start(priority=queue)

    def drain(r, queue):
        _row_copy(ys_hbm, 0, buf, r, sem).wait()

    _for_rows(tm, issue)
    _for_rows(tm, drain)
    y = x_ref[...] + g_ref[...] * buf[...]
    if final_norm:
        y = (y * lax.rsqrt(jnp.mean(y * y, axis=-1, keepdims=True) + EPS)) * fg_ref[...]
    o_ref[...] = y


def _combine(dest, x, gate, final_g, ys, final_norm, tm=256):
    t, d = x.shape
    vec = pl.BlockSpec((1, d), lambda i, dest: (0, 0))
    grid_spec = pltpu.PrefetchScalarGridSpec(
        num_scalar_prefetch=1,
        grid=(t // tm,),
        in_specs=[pl.BlockSpec((tm, d), lambda i, dest: (i, 0)), vec, vec, pl.BlockSpec(memory_space=pl.ANY)],
        out_specs=pl.BlockSpec((tm, d), lambda i, dest: (i, 0)),
        scratch_shapes=[pltpu.VMEM((tm, d), F32), pltpu.SemaphoreType.DMA],
    )
    return pl.pallas_call(
        functools.partial(_combine_kernel, final_norm=final_norm),
        grid_spec=grid_spec,
        out_shape=jax.ShapeDtypeStruct((t, d), F32),
        compiler_params=_cparams(("arbitrary",)),
        name="combine",
    )(dest, x, gate.reshape(1, d), final_g.reshape(1, d), ys)


def _rope_tables(t):
    pos = jnp.arange(t, dtype=I32).astype(F32)[:, None]
    lane = jnp.arange(LANES)

    def table(dim):
        half = dim // 2
        inv = ROPE_THETA ** (-jnp.arange(half, dtype=F32) / half)
        ang = pos * inv[None, :]
        j = lane % dim
        cos = jnp.cos(ang)[:, j % half]
        sin = jnp.sin(ang)[:, j % half] * jnp.where(j < half, -1.0, 1.0)[None, :]
        return cos, sin

    return table(HEAD_DIM), table(IDX_DIM)


def _attention_layer(x, mod, norm_g, w_in, w_out, tables):
    t, d = x.shape
    sh1, sc1, g1 = mod[0:d], mod[d:2 * d], mod[2 * d:3 * d]
    (cos_h, sin_h), (cos_i, sin_i) = tables
    hn = _normmod(x, norm_g, sc1, sh1)
    w_bf = w_in.astype(BF16)
    nq = N_HEADS * HEAD_DIM
    nkv = N_KV_GROUPS * HEAD_DIM
    nqi = IDX_HEADS * IDX_DIM
    tn = 512
    q = _proj_rope(hn, w_bf, 0, nq, cos_h, sin_h, HEAD_DIM // 2, BF16)
    kv = _proj_kv(hn, w_bf, nq // nkv, nq // nkv + 1, cos_h, sin_h)
    qi = _proj_rope(hn, w_bf, (nq + 2 * nkv) // tn, nqi, cos_i, sin_i, IDX_DIM // 2, BF16)
    tail0 = nq + 2 * nkv + nqi
    w_tail = jnp.pad(w_bf[:, tail0:], ((0, 0), (0, LANES - (IDX_DIM + IDX_HEADS))))
    lane = jnp.arange(LANES)
    wi_scale = (IDX_HEADS ** -0.5) * (IDX_DIM ** -0.5)
    cos_t = jnp.where(lane[None, :] < IDX_DIM, cos_i, wi_scale)
    sin_t = jnp.where(lane[None, :] < IDX_DIM, sin_i, 0.0)
    tail = _proj_rope(hn, w_tail, 0, LANES, cos_t, sin_t, IDX_DIM // 2, F32, tn=LANES)
    ki = tail[:, :IDX_DIM].astype(BF16)
    wit = tail[:, IDX_DIM:IDX_DIM + IDX_HEADS].T
    n_keep = min(TOPK_KEYS, t // 4)
    idx_t, nv = _indexer(qi.T, ki, wit, n_keep)
    att = _gattn(nv.reshape(t), idx_t.T, q.reshape(t * N_HEADS, HEAD_DIM), kv.reshape(t * KV_ROWS, LANES), n_keep)
    att = att.reshape(t, nq)
    return _mm_res(att, w_out.astype(BF16), jnp.zeros((d,), F32), x, g1)


def _conv_layer(x, mod, norm_g, w_pw1, b_pw1, w_dw, b_dw, ln_g, ln_b, w_pw2, b_pw2):
    t, d = x.shape
    sh1, sc1, g1 = mod[0:d], mod[d:2 * d], mod[2 * d:3 * d]
    hn = _normmod(x, norm_g, sc1, sh1)
    u = _glu(hn, w_pw1.astype(BF16), b_pw1)
    v = _dwconv_ln(u, w_dw, b_dw, ln_g, ln_b)
    return _mm_res(v, w_pw2.astype(BF16), b_pw2, x, g1)


def _moe_layer(x, mod, norm_g, w_rg, b_rg, w_re, b_re, w_gate, w_up, w_down, final_g, final_norm):
    t, d = x.shape
    sh2, sc2, g2 = mod[3 * d:4 * d], mod[4 * d:5 * d], mod[5 * d:6 * d]
    h, aux, counts = _route(x, norm_g, sc2, sh2, w_rg, b_rg, w_re, b_re)
    bucket = aux[:, AUX_BUCKET].astype(I32)
    rank = aux[:, AUX_RANK].astype(I32)
    sizes = counts[0, :N_BUCKETS].astype(I32)
    padded = (sizes + MOE_TM - 1) // MOE_TM * MOE_TM
    ends = jnp.cumsum(padded)
    dest = (ends - padded)[bucket] + rank
    n_blocks = t // MOE_TM + N_BUCKETS
    blk_start = jnp.arange(n_blocks, dtype=I32) * MOE_TM
    blk_bucket = jnp.minimum(jnp.searchsorted(ends, blk_start, side='right'), N_BUCKETS - 1).astype(I32)
    live = (blk_start < ends[-1]).astype(I32)
    group0 = (blk_bucket // N_PAIRS) * EXPERTS_PER_GROUP
    e_lo = group0 + jnp.asarray(PAIR_LO, I32)[blk_bucket % N_PAIRS]
    e_hi = group0 + jnp.asarray(PAIR_HI, I32)[blk_bucket % N_PAIRS]
    xs, ws = _dispatch(dest, h, aux, n_blocks * MOE_TM)
    ys = _ffn(e_lo, e_hi, live, xs, ws, w_gate.astype(BF16), w_up.astype(BF16), w_down.astype(BF16))
    return _combine(dest, x, g2, final_g, ys, final_norm)


def kernel(x, c, ada_w, ada_b, norm1_g, norm2_g, attn_w_in, attn_w_out, conv_w_pw1, conv_b_pw1, conv_w_dw, conv_b_dw, conv_ln_g, conv_ln_b, conv_w_pw2, conv_b_pw2, router_w_group, router_b_group, router_w_expert, router_b_expert, exp_w_gate, exp_w_up, exp_w_down, final_g):
    bsz, t, d = x.shape
    assert bsz == 1 and d == N_HEADS * HEAD_DIM
    depth = ada_w.shape[0]
    mods = _ada_mod(c, ada_w, ada_b)
    tables = _rope_tables(t)
    xf = x.reshape(t, d)
    for i in range(depth):
        j = i // 2
        if i % 2 == 0:
            xf = _attention_layer(xf, mods[i], norm1_g[i], attn_w_in[j], attn_w_out[j], tables)
        else:
            xf = _conv_layer(xf, mods[i], norm1_g[i], conv_w_pw1[j], conv_b_pw1[j], conv_w_dw[j], conv_b_dw[j],
                             conv_ln_g[j], conv_ln_b[j], conv_w_pw2[j], conv_b_pw2[j])
        xf = _moe_layer(xf, mods[i], norm2_g[i], router_w_group[i], router_b_group[i], router_w_expert[i],
                        router_b_expert[i], exp_w_gate[i], exp_w_up[i], exp_w_down[i], final_g, i == depth - 1)
    return xf.reshape(bsz, t, d)
```
